```python
import math
import jax, jax.numpy as jnp
from jax import lax
import numpy as np

D_MODEL = 1024
BATCH = 16
SEQ = 2048
DEPTH = 1

MOBA_HEADS = 8
MOBA_HEAD_DIM = 64
MOBA_BLOCK = 256
MOBA_TOPK = 3
MOBA_Q_CHUNK = 64
MOBA_WIDTH = MOBA_HEADS * MOBA_HEAD_DIM
DIFF_HEADS = 4
DIFF_HEAD_DIM = 64
DIFF_Q_BLOCK = 128
DIFF_QK_WIDTH = DIFF_HEADS * 2 * DIFF_HEAD_DIM
DIFF_V_WIDTH = DIFF_HEADS * 2 * DIFF_HEAD_DIM
N_BRANCHES = 2
IN_COLS = 3 * MOBA_WIDTH + 2 * DIFF_QK_WIDTH + DIFF_V_WIDTH + N_BRANCHES * D_MODEL
ROPE_THETA = 10000.0
N_GROUPS = 4
EXPERTS_PER_GROUP = 8
N_EXPERTS = N_GROUPS * EXPERTS_PER_GROUP
EXPERT_TOPK = 2
D_EXPERT = 512
MOE_ROW_BLOCK = 128

EPS = 1e-6
NEG = -1e30

kernel_name = "hybrid_moba_diffattn_hmoe_block"


def rmsnorm(x, g):
    xf = x.astype(jnp.float32)
    var = jnp.mean(xf * xf, axis=-1, keepdims=True)
    return (xf * lax.rsqrt(var + EPS)).astype(x.dtype) * g


def rope_tables(seq, dim, dtype):
    inv = 1.0 / (ROPE_THETA ** (jnp.arange(0, dim, 2, dtype=jnp.float32) / dim))
    ang = jnp.arange(seq, dtype=jnp.float32)[:, None] * inv[None, :]
    ang = jnp.concatenate([ang, ang], axis=-1)
    return jnp.cos(ang).astype(dtype), jnp.sin(ang).astype(dtype)


def apply_rope(x, cos, sin):
    half = x.shape[-1] // 2
    rot = jnp.concatenate([-x[..., half:], x[..., :half]], axis=-1)
    return x * cos[:, None, :] + rot * sin[:, None, :]


def moba_attention(q, k, v):
    B, S, H, dh = q.shape
    nb = max(-(-S // MOBA_BLOCK), MOBA_TOPK)
    L = nb * MOBA_BLOCK
    pad = ((0, 0), (0, L - S), (0, 0), (0, 0))
    kb = jnp.pad(k, pad).reshape(B, nb, MOBA_BLOCK, H, dh).transpose(0, 3, 1, 2, 4)
    vb = jnp.pad(v, pad).reshape(B, nb, MOBA_BLOCK, H, dh).transpose(0, 3, 1, 2, 4)
    kmean = jnp.mean(kb.astype(jnp.float32), axis=3)
    qh = q.transpose(0, 2, 1, 3)
    nq = S // MOBA_Q_CHUNK
    scale = dh ** -0.5
    hidx = jnp.arange(H)[:, None, None]
    blk_ids = jnp.arange(nb)

    def one_batch(args):
        qb, kbh, vbh, kmh = args
        qcs = qb.reshape(H, nq, MOBA_Q_CHUNK, dh).transpose(1, 0, 2, 3)
        starts = jnp.arange(nq, dtype=jnp.int32) * MOBA_Q_CHUNK

        def one_chunk(a):
            qc, start = a
            blk = start // MOBA_BLOCK
            gate = jnp.einsum('hqd,hnd->hqn', qc.astype(jnp.float32), kmh)
            gate = jnp.where(blk_ids < blk, gate, -jnp.inf)
            _, sel = lax.top_k(gate, MOBA_TOPK)
            sel_valid = sel < blk
            ksel = kbh[hidx, sel].reshape(H, MOBA_Q_CHUNK, MOBA_TOPK * MOBA_BLOCK, dh)
            vsel = vbh[hidx, sel].reshape(H, MOBA_Q_CHUNK, MOBA_TOPK * MOBA_BLOCK, dh)
            s_sel = jnp.einsum('hqd,hqkd->hqk', qc, ksel).astype(jnp.float32) * scale
            sel_mask = jnp.repeat(sel_valid, MOBA_BLOCK, axis=-1)
            s_sel = jnp.where(sel_mask, s_sel, NEG)
            kown = lax.dynamic_index_in_dim(kbh, blk, axis=1, keepdims=False)
            vown = lax.dynamic_index_in_dim(vbh, blk, axis=1, keepdims=False)
            s_own = jnp.einsum('hqd,hkd->hqk', qc, kown).astype(jnp.float32) * scale
            qpos = start + jnp.arange(MOBA_Q_CHUNK)
            kpos = blk * MOBA_BLOCK + jnp.arange(MOBA_BLOCK)
            s_own = jnp.where(kpos[None, None, :] <= qpos[None, :, None], s_own, NEG)
            p = jax.nn.softmax(jnp.concatenate([s_sel, s_own], axis=-1), axis=-1).astype(v.dtype)
            n_sel = MOBA_TOPK * MOBA_BLOCK
            out = (jnp.einsum('hqk,hqkd->hqd', p[..., :n_sel], vsel)
                   + jnp.einsum('hqk,hkd->hqd', p[..., n_sel:], vown))
            return out

        o = lax.map(one_chunk, (qcs, starts))
        return o.transpose(0, 2, 1, 3).reshape(S, H, dh)

    return lax.map(one_batch, (qh, kb, vb, kmean))


def diff_attention(q1, q2, k1, k2, v, lam):
    B, S, H, dh = q1.shape
    nq = S // DIFF_Q_BLOCK
    scale = dh ** -0.5
    kpos = jnp.arange(S)
    q1b = q1.reshape(B, nq, DIFF_Q_BLOCK, H, dh).transpose(1, 0, 2, 3, 4)
    q2b = q2.reshape(B, nq, DIFF_Q_BLOCK, H, dh).transpose(1, 0, 2, 3, 4)
    starts = jnp.arange(nq, dtype=jnp.int32) * DIFF_Q_BLOCK

    def one_block(a):
        qb1, qb2, start = a
        qpos = start + jnp.arange(DIFF_Q_BLOCK)
        mask = kpos[None, :] <= qpos[:, None]
        s1 = jnp.einsum('bqhd,bkhd->bhqk', qb1, k1).astype(jnp.float32) * scale
        s2 = jnp.einsum('bqhd,bkhd->bhqk', qb2, k2).astype(jnp.float32) * scale
        p1 = jax.nn.softmax(jnp.where(mask, s1, NEG), axis=-1)
        p2 = jax.nn.softmax(jnp.where(mask, s2, NEG), axis=-1)
        pd = (p1 - lam * p2).astype(v.dtype)
        return jnp.einsum('bhqk,bkhe->bqhe', pd, v)

    o = lax.map(one_block, (q1b, q2b, starts))
    return o.transpose(1, 0, 2, 3, 4).reshape(B, S, H, 2 * dh)


def hier_moe(h, w_group, b_group, w_router, b_router, w1, w3, w2):
    B, S, D = h.shape
    T = B * S
    xt = h.reshape(T, D)
    g_logits = jnp.matmul(xt, w_group).astype(jnp.float32) + b_group
    p_group = jax.nn.softmax(g_logits, axis=-1)
    g_sel = jnp.argmax(g_logits, axis=-1)
    p_g = jnp.take_along_axis(p_group, g_sel[:, None], axis=-1)
    e_logits = (jnp.matmul(xt, w_router).astype(jnp.float32) + b_router).reshape(T, N_GROUPS, EXPERTS_PER_GROUP)
    e_logits = jnp.take_along_axis(e_logits, g_sel[:, None, None], axis=1)[:, 0]
    p_in = jax.nn.softmax(e_logits, axis=-1)
    top_p, top_e = lax.top_k(p_in, EXPERT_TOPK)
    top_w = top_p / jnp.sum(top_p, axis=-1, keepdims=True) * p_g
    expert_id = g_sel[:, None].astype(jnp.int32) * EXPERTS_PER_GROUP + top_e.astype(jnp.int32)

    A = T * EXPERT_TOPK
    m = MOE_ROW_BLOCK
    ids = expert_id.reshape(A)
    tok = jnp.repeat(jnp.arange(T, dtype=jnp.int32), EXPERT_TOPK)
    wts = top_w.reshape(A)
    order = jnp.argsort(ids)
    ids_s, tok_s, w_s = ids[order], tok[order], wts[order]
    counts = jax.ops.segment_sum(jnp.ones((A,), jnp.int32), ids, num_segments=N_EXPERTS)
    starts = jnp.cumsum(counts) - counts
    padded = (counts + m - 1) // m * m
    pends = jnp.cumsum(padded)
    pstarts = pends - padded
    dest = pstarts[ids_s] + (jnp.arange(A, dtype=jnp.int32) - starts[ids_s])
    P = (-(-A // m)) * m + N_EXPERTS * m
    nblk = P // m
    x_disp = jnp.zeros((P, D), h.dtype).at[dest].set(xt[tok_s])
    w_disp = jnp.zeros((P,), jnp.float32).at[dest].set(w_s)
    tok_disp = jnp.zeros((P,), jnp.int32).at[dest].set(tok_s)
    blk_expert = jnp.minimum(jnp.searchsorted(pends, jnp.arange(nblk, dtype=jnp.int32) * m, side='right'),
                             N_EXPERTS - 1).astype(jnp.int32)

    def expert_block(a):
        xblk, e = a
        gate = jnp.matmul(xblk, w1[e])
        up = jnp.matmul(xblk, w3[e])
        return jnp.matmul(jax.nn.silu(gate) * up, w2[e])

    yb = lax.map(expert_block, (x_disp.reshape(nblk, m, D), blk_expert))
    y = jax.ops.segment_sum(yb.reshape(P, D) * w_disp[:, None].astype(h.dtype), tok_disp, num_segments=T)
    return y.reshape(B, S, D)


def setup_inputs(seed: int = 0) -> dict:
    key = jax.random.key(seed)
    ks = jax.random.split(key, 24)
    f32 = jnp.float32
    L, D = DEPTH, D_MODEL

    def nrm(k, shape, scale):
        return jax.random.normal(k, shape, f32) * scale

    return {
        "x": nrm(ks[0], (BATCH, SEQ, D), 1.0),
        "g_mix": 1.0 + nrm(ks[1], (L, D), 0.02),
        "w_in": nrm(ks[2], (L, D, IN_COLS), D ** -0.5),
        "w_branch_moba": nrm(ks[3], (L, MOBA_WIDTH, D), MOBA_WIDTH ** -0.5),
        "w_branch_diff": nrm(ks[4], (L, DIFF_V_WIDTH, D), DIFF_V_WIDTH ** -0.5),
        "w_out": nrm(ks[5], (L, D, D), D ** -0.5),
        "diff_lambda_q1": nrm(ks[6], (L, DIFF_HEAD_DIM), 0.1),
        "diff_lambda_k1": nrm(ks[7], (L, DIFF_HEAD_DIM), 0.1),
        "diff_lambda_q2": nrm(ks[8], (L, DIFF_HEAD_DIM), 0.1),
        "diff_lambda_k2": nrm(ks[9], (L, DIFF_HEAD_DIM), 0.1),
        "diff_subln_g": 1.0 + nrm(ks[10], (L, 2 * DIFF_HEAD_DIM), 0.02),
        "g_ffn": 1.0 + nrm(ks[11], (L, D), 0.02),
        "w_group": nrm(ks[12], (L, D, N_GROUPS), D ** -0.5),
        "b_group": nrm(ks[13], (L, N_GROUPS), 0.01),
        "w_router": nrm(ks[14], (L, D, N_EXPERTS), D ** -0.5),
        "b_router": nrm(ks[15], (L, N_EXPERTS), 0.01),
        "w_expert_gate": nrm(ks[16], (L, N_EXPERTS, D, D_EXPERT), D ** -0.5),
        "w_expert_up": nrm(ks[17], (L, N_EXPERTS, D, D_EXPERT), D ** -0.5),
        "w_expert_down": nrm(ks[18], (L, N_EXPERTS, D_EXPERT, D), D_EXPERT ** -0.5),
        "g_final": 1.0 + nrm(ks[19], (D,), 0.02),
    }


def reference(x, g_mix, w_in, w_branch_moba, w_branch_diff, w_out,
              diff_lambda_q1, diff_lambda_k1, diff_lambda_q2, diff_lambda_k2, diff_subln_g,
              g_ffn, w_group, b_group, w_router, b_router,
              w_expert_gate, w_expert_up, w_expert_down, g_final):
    B, S, D = x.shape
    cos, sin = rope_tables(S, MOBA_HEAD_DIM, x.dtype)
    splits = [int(s) for s in np.cumsum([MOBA_WIDTH, MOBA_WIDTH, MOBA_WIDTH,
                                         DIFF_QK_WIDTH, DIFF_QK_WIDTH, DIFF_V_WIDTH])]
    for l in range(DEPTH):
        h = rmsnorm(x, g_mix[l])
        proj = jnp.matmul(h, w_in[l])
        qa, ka, va, qd, kd, vd, gates = jnp.split(proj, splits, axis=-1)
        qa = apply_rope(qa.reshape(B, S, MOBA_HEADS, MOBA_HEAD_DIM), cos, sin)
        ka = apply_rope(ka.reshape(B, S, MOBA_HEADS, MOBA_HEAD_DIM), cos, sin)
        va = va.reshape(B, S, MOBA_HEADS, MOBA_HEAD_DIM)
        o_a = moba_attention(qa, ka, va).reshape(B, S, MOBA_WIDTH)
        qd = apply_rope(qd.reshape(B, S, DIFF_HEADS * 2, DIFF_HEAD_DIM), cos, sin).reshape(B, S, DIFF_HEADS, 2, DIFF_HEAD_DIM)
        kd = apply_rope(kd.reshape(B, S, DIFF_HEADS * 2, DIFF_HEAD_DIM), cos, sin).reshape(B, S, DIFF_HEADS, 2, DIFF_HEAD_DIM)
        vd = vd.reshape(B, S, DIFF_HEADS, 2 * DIFF_HEAD_DIM)
        lambda_init = 0.8 - 0.6 * math.exp(-0.3 * l)
        lam = (jnp.exp(jnp.sum(diff_lambda_q1[l].astype(jnp.float32) * diff_lambda_k1[l].astype(jnp.float32)))
               - jnp.exp(jnp.sum(diff_lambda_q2[l].astype(jnp.float32) * diff_lambda_k2[l].astype(jnp.float32)))
               + lambda_init)
        o_d = diff_attention(qd[..., 0, :], qd[..., 1, :], kd[..., 0, :], kd[..., 1, :], vd, lam)
        o_d = (rmsnorm(o_d, diff_subln_g[l]) * (1.0 - lambda_init)).reshape(B, S, DIFF_V_WIDTH)
        gsig = jax.nn.sigmoid(gates).reshape(B, S, N_BRANCHES, D)
        merged = (gsig[:, :, 0] * jnp.matmul(o_a, w_branch_moba[l])
                  + gsig[:, :, 1] * jnp.matmul(o_d, w_branch_diff[l]))
        x = x + jnp.matmul(merged, w_out[l])
        h2 = rmsnorm(x, g_ffn[l])
        x = x + hier_moe(h2, w_group[l], b_group[l], w_router[l], b_router[l],
                         w_expert_gate[l], w_expert_up[l], w_expert_down[l])
    return rmsnorm(x, g_final)
```

```python
import functools

import jax
import jax.numpy as jnp
from jax import lax
from jax.experimental import pallas as pl
from jax.experimental.pallas import tpu as pltpu

F32 = jnp.float32
BF16 = jnp.bfloat16
I32 = jnp.int32
U32 = jnp.uint32

LANES = 128
HEAD_DIM = 64
ATT_BLK = 256
ATT_BLK_SHIFT = 8
MOBA_TOPK = 3
N_GROUPS = 4
EXPERTS_PER_GROUP = 8
N_EXPERTS = N_GROUPS * EXPERTS_PER_GROUP
EXPERT_TOPK = 2
MOE_BLK = 256
TOK_TILE = 256
QKV_TILE = 512
ROUTE_COLS = 8
EPS = 1e-6
NEG = -1e30
ROPE_THETA = 10000.0
LAMBDA_INIT = 0.8 - 0.6 * 1.0
ATT_SCALE = HEAD_DIM ** -0.5
VMEM_LIMIT = 48 * 1024 * 1024


def _dot(a, b):
    return jnp.dot(a, b, preferred_element_type=F32)


def _dot_nt(a, b):
    return lax.dot_general(a, b, (((1,), (1,)), ((), ())), preferred_element_type=F32)


def _rms(x, g):
    var = jnp.mean(x * x, axis=-1, keepdims=True)
    return (x * lax.rsqrt(var + EPS)) * g


def _params(sem):
    return pltpu.CompilerParams(dimension_semantics=sem, vmem_limit_bytes=VMEM_LIMIT)


def _qkv_kernel(x_ref, g_ref, w_ref, cos_ref, sin_ref, qa_ref, ka_ref, va_ref, qd_ref, kd_ref, vd_ref):
    h = _rms(x_ref[...], g_ref[...]).astype(BF16)
    cos = cos_ref[...]
    sin = sin_ref[...]
    lane = lax.broadcasted_iota(I32, cos.shape, 1)
    first = (lane & (HEAD_DIM - 1)) < HEAD_DIM // 2
    width = qa_ref.shape[1]
    outs = ((qa_ref, True, ATT_SCALE), (ka_ref, True, 1.0), (va_ref, False, 1.0),
            (qd_ref, True, ATT_SCALE), (kd_ref, True, 1.0), (vd_ref, False, 1.0))
    for j, (o_ref, rope, scale) in enumerate(outs):
        y = _dot(h, w_ref[:, j * width:(j + 1) * width])
        for c in range(width // LANES):
            yc = y[:, c * LANES:(c + 1) * LANES]
            if rope:
                rot = jnp.where(first, pltpu.roll(yc, LANES - HEAD_DIM // 2, 1), pltpu.roll(yc, HEAD_DIM // 2, 1))
                yc = yc * cos + rot * sin
            if scale != 1.0:
                yc = yc * scale
            o_ref[:, c * LANES:(c + 1) * LANES] = yc.astype(BF16)


def _qkv_proj(x2, g_mix, w_qkv, cos, sin, seq):
    T, D = x2.shape
    width = w_qkv.shape[1] // 6
    tm = min(QKV_TILE, seq)
    assert seq % tm == 0 and T % tm == 0
    spt = seq // tm
    out = jax.ShapeDtypeStruct((T, width), BF16)
    return pl.pallas_call(
        _qkv_kernel,
        grid=(T // tm,),
        in_specs=[
            pl.BlockSpec((tm, D), lambda i: (i, 0)),
            pl.BlockSpec((1, D), lambda i: (0, 0)),
            pl.BlockSpec(w_qkv.shape, lambda i: (0, 0)),
            pl.BlockSpec((tm, LANES), lambda i: (i % spt, 0)),
            pl.BlockSpec((tm, LANES), lambda i: (i % spt, 0)),
        ],
        out_specs=[pl.BlockSpec((tm, width), lambda i: (i, 0))] * 6,
        out_shape=[out] * 6,
        compiler_params=_params(("arbitrary",)),
        name="qkv_proj",
    )(x2, g_mix, w_qkv, cos, sin)


def _softmax_first(s, v):
    m = jnp.max(s, axis=1, keepdims=True)
    p = jnp.exp(s - m)
    l = jnp.sum(p, axis=1, keepdims=True)
    return m, l, _dot(p.astype(BF16), v)


def _softmax_next(state, s, v):
    m, l, acc = state
    m_new = jnp.maximum(m, jnp.max(s, axis=1, keepdims=True))
    alpha = jnp.exp(m - m_new)
    p = jnp.exp(s - m_new)
    l = alpha * l + jnp.sum(p, axis=1, keepdims=True)
    acc = alpha * acc + _dot(p.astype(BF16), v)
    return m_new, l, acc


def _causal_bias():
    row = lax.broadcasted_iota(I32, (ATT_BLK, ATT_BLK), 0)
    col = lax.broadcasted_iota(I32, (ATT_BLK, ATT_BLK), 1)
    return jnp.where(col <= row, 0.0, NEG).astype(F32)


def _moba_kernel(q_ref, k_ref, v_ref, o_ref, kaug_ref, kmf_ref, kmhi_ref, kmlo_ref, cbias_ref):
    S = k_ref.shape[1]
    nb = S // ATT_BLK
    kf = k_ref[0].astype(F32)
    lane_s = lax.broadcasted_iota(I32, (S, LANES), 1)
    blk_s = lax.broadcasted_iota(I32, (S, LANES), 0) >> ATT_BLK_SHIFT
    kmean = jnp.mean(kf.reshape(nb, ATT_BLK, LANES), axis=1)
    lane_n = lax.broadcasted_iota(I32, (nb, LANES), 1)
    for hh in range(2):
        base = HEAD_DIM * (1 - hh)
        own_s = (lane_s < HEAD_DIM) if hh == 0 else (lane_s >= HEAD_DIM)
        own_n = (lane_n < HEAD_DIM) if hh == 0 else (lane_n >= HEAD_DIM)
        onehot = jnp.where((lane_s - base) == blk_s, 1.0, 0.0)
        kaug_ref[hh] = jnp.where(own_s, kf, onehot).astype(BF16)
        kmf_ref[...] = jnp.zeros((LANES, LANES), F32)
        kmf_ref[base:base + nb, :] = jnp.where(own_n, kmean, 0.0)
        km = kmf_ref[...]
        hi = km.astype(BF16)
        kmhi_ref[hh] = hi
        kmlo_ref[hh] = (km - hi.astype(F32)).astype(BF16)
    cbias_ref[...] = _causal_bias()

    lane = lax.broadcasted_iota(I32, (ATT_BLK, LANES), 1)

    def qblock(i, carry):
        r0 = pl.multiple_of(i * ATT_BLK, ATT_BLK)
        q = q_ref[0, pl.ds(r0, ATT_BLK), :].astype(F32)
        v_own = v_ref[0, pl.ds(r0, ATT_BLK), :]
        qaug = []
        for hh in range(2):
            base = HEAD_DIM * (1 - hh)
            own = (lane < HEAD_DIM) if hh == 0 else (lane >= HEAD_DIM)
            qh = jnp.where(own, q, 0.0)
            qhb = qh.astype(BF16)
            gate = _dot_nt(qhb, kmhi_ref[hh]) + _dot_nt(qhb, kmlo_ref[hh])
            cand = (lane >= base) & (lane < base + i)
            g = jnp.where(cand, gate, -jnp.inf)
            rank = jnp.zeros((ATT_BLK, LANES), I32)
            for r in range(1, nb - 1):
                rank = rank + jnp.where(pltpu.roll(g, r, 1) >= g, 1, 0)
                rank = rank + jnp.where(pltpu.roll(g, LANES - r, 1) > g, 1, 0)
            drop = cand & (rank >= MOBA_TOPK)
            qaug.append(jnp.where(drop, NEG, qh).astype(BF16))
        state = []
        for hh in range(2):
            s = _dot_nt(qaug[hh], kaug_ref[hh, pl.ds(r0, ATT_BLK), :]) + cbias_ref[...]
            state.extend(_softmax_first(s, v_own))

        def past(n, st):
            c0 = pl.multiple_of(n * ATT_BLK, ATT_BLK)
            vn = v_ref[0, pl.ds(c0, ATT_BLK), :]
            out = []
            for hh in range(2):
                s = _dot_nt(qaug[hh], kaug_ref[hh, pl.ds(c0, ATT_BLK), :])
                out.extend(_softmax_next(st[3 * hh:3 * hh + 3], s, vn))
            return tuple(out)

        st = lax.fori_loop(0, i, past, tuple(state))
        o = jnp.where(lane < HEAD_DIM, st[2] / st[1], st[5] / st[4])
        o_ref[0, pl.ds(r0, ATT_BLK), :] = o.astype(BF16)
        return carry

    lax.fori_loop(0, nb, qblock, 0)


def _moba_attention(qa, ka, va):
    B, S, W = qa.shape
    assert S % ATT_BLK == 0 and S // ATT_BLK <= HEAD_DIM // 2 and W % LANES == 0
    spec = pl.BlockSpec((1, S, LANES), lambda b, j: (b, 0, j))
    return pl.pallas_call(
        _moba_kernel,
        grid=(B, W // LANES),
        in_specs=[spec, spec, spec],
        out_specs=spec,
        out_shape=jax.ShapeDtypeStruct((B, S, W), BF16),
        scratch_shapes=[
            pltpu.VMEM((2, S, LANES), BF16),
            pltpu.VMEM((LANES, LANES), F32),
            pltpu.VMEM((2, LANES, LANES), BF16),
            pltpu.VMEM((2, LANES, LANES), BF16),
            pltpu.VMEM((ATT_BLK, ATT_BLK), F32),
        ],
        compiler_params=_params(("arbitrary", "arbitrary")),
        name="moba_attention",
    )(qa, ka, va)


def _diff_kernel(lq1_ref, lk1_ref, lq2_ref, lk2_ref, g_ref, q_ref, k_ref, v_ref, o_ref, cbias_ref):
    S = k_ref.shape[1]
    nb = S // ATT_BLK
    lam = (jnp.exp(jnp.sum(lq1_ref[...] * lk1_ref[...], axis=1, keepdims=True))
           - jnp.exp(jnp.sum(lq2_ref[...] * lk2_ref[...], axis=1, keepdims=True)) + LAMBDA_INIT)
    cbias_ref[...] = _causal_bias()
    lane = lax.broadcasted_iota(I32, (ATT_BLK, LANES), 1)

    def qblock(i, carry):
        r0 = pl.multiple_of(i * ATT_BLK, ATT_BLK)
        q = q_ref[0, pl.ds(r0, ATT_BLK), :].astype(F32)
        qm = [jnp.where(lane < HEAD_DIM, q, 0.0).astype(BF16), jnp.where(lane >= HEAD_DIM, q, 0.0).astype(BF16)]
        k_own = k_ref[0, pl.ds(r0, ATT_BLK), :]
        v_own = v_ref[0, pl.ds(r0, ATT_BLK), :]
        state = []
        for mp in range(2):
            state.extend(_softmax_first(_dot_nt(qm[mp], k_own) + cbias_ref[...], v_own))

        def past(n, st):
            c0 = pl.multiple_of(n * ATT_BLK, ATT_BLK)
            kn = k_ref[0, pl.ds(c0, ATT_BLK), :]
            vn = v_ref[0, pl.ds(c0, ATT_BLK), :]
            out = []
            for mp in range(2):
                out.extend(_softmax_next(st[3 * mp:3 * mp + 3], _dot_nt(qm[mp], kn), vn))
            return tuple(out)

        st = lax.fori_loop(0, i, past, tuple(state))
        o = st[2] / st[1] - lam * (st[5] / st[4])
        o_ref[0, pl.ds(r0, ATT_BLK), :] = (_rms(o, g_ref[...]) * (1.0 - LAMBDA_INIT)).astype(BF16)
        return carry

    lax.fori_loop(0, nb, qblock, 0)


def _diff_attention(qd, kd, vd, lq1, lk1, lq2, lk2, subln_g):
    B, S, W = qd.shape
    assert S % ATT_BLK == 0 and W % LANES == 0
    spec = pl.BlockSpec((1, S, LANES), lambda b, j: (b, 0, j))
    small = pl.BlockSpec((1, HEAD_DIM), lambda b, j: (0, 0))
    return pl.pallas_call(
        _diff_kernel,
        grid=(B, W // LANES),
        in_specs=[small, small, small, small, pl.BlockSpec((1, LANES), lambda b, j: (0, 0)), spec, spec, spec],
        out_specs=spec,
        out_shape=jax.ShapeDtypeStruct((B, S, W), BF16),
        scratch_shapes=[pltpu.VMEM((ATT_BLK, ATT_BLK), F32)],
        compiler_params=_params(("arbitrary", "arbitrary")),
        name="diff_attention",
    )(lq1, lk1, lq2, lk2, subln_g, qd, kd, vd)


def _pack_bf16_pairs(x):
    n = x.shape[1] // 2
    lo = lax.bitcast_convert_type(x[:, :n].astype(BF16).astype(F32), U32)
    hi = lax.bitcast_convert_type(x[:, n:].astype(BF16).astype(F32), U32)
    return (lo >> 16) | hi


def _unpack_bf16_pairs(w):
    lo = lax.bitcast_convert_type(w << 16, F32)
    hi = lax.bitcast_convert_type(w & jnp.uint32(0xFFFF0000), F32)
    return jnp.concatenate([lo, hi], axis=1).astype(BF16)


def _merge_kernel(x_ref, oa_ref, od_ref, gmix_ref, wg_ref, wbm_ref, wbd_ref, wout_ref, gffn_ref,
                  wrhi_ref, wrlo_ref, br_ref,
                  x1_ref, h2_ref, route_ref, cnt_ref, base_ref, ltri_ref):
    tm, D = x_ref.shape

    @pl.when(pl.program_id(0) == 0)
    def _():
        base_ref[...] = jnp.zeros_like(base_ref)
        row = lax.broadcasted_iota(I32, (tm, tm), 0)
        col = lax.broadcasted_iota(I32, (tm, tm), 1)
        ltri_ref[...] = jnp.where(col < row, 1.0, 0.0).astype(BF16)

    x = x_ref[...]
    h = _rms(x, gmix_ref[...]).astype(BF16)
    sig = jax.nn.sigmoid(_dot(h, wg_ref[...]))
    merged = sig[:, :D] * _dot(oa_ref[...], wbm_ref[...]) + sig[:, D:] * _dot(od_ref[...], wbd_ref[...])
    x1 = x + _dot(merged.astype(BF16), wout_ref[...])
    x1_ref[...] = x1
    h2 = _rms(x1, gffn_ref[...])
    h2_ref[...] = _pack_bf16_pairs(h2)

    hhi = h2.astype(BF16)
    hlo = (h2 - hhi.astype(F32)).astype(BF16)
    lg = _dot(hhi, wrhi_ref[...]) + _dot(hhi, wrlo_ref[...]) + _dot(hlo, wrhi_ref[...]) + br_ref[...]
    lane = lax.broadcasted_iota(I32, (tm, LANES), 1)
    ninf = -jnp.inf

    def first_argmax(vals):
        m = jnp.max(vals, axis=1, keepdims=True)
        return m, jnp.min(jnp.where(vals == m, lane, LANES), axis=1, keepdims=True)

    is_group = lane < N_GROUPS
    gm, gsel = first_argmax(jnp.where(is_group, lg, ninf))
    p_group = 1.0 / jnp.sum(jnp.where(is_group, jnp.exp(lg - gm), 0.0), axis=1, keepdims=True)
    lo = N_GROUPS + gsel * EXPERTS_PER_GROUP
    el = jnp.where((lane >= lo) & (lane < lo + EXPERTS_PER_GROUP), lg, ninf)
    m1, i1 = first_argmax(el)
    m2, i2 = first_argmax(jnp.where(lane == i1, ninf, el))
    e2x = jnp.exp(m2 - m1)
    w1 = p_group / (1.0 + e2x)
    w2 = p_group * e2x / (1.0 + e2x)
    e1 = i1 - N_GROUPS
    e2 = i2 - N_GROUPS

    oh1 = jnp.where(lane == e1, 1.0, 0.0)
    oh2 = jnp.where(lane == e2, 1.0, 0.0)
    c1 = _dot(ltri_ref[...], oh1.astype(BF16))
    c2 = _dot(ltri_ref[...], oh2.astype(BF16))
    base = base_ref[...]
    tot1 = jnp.sum(oh1, axis=0, keepdims=True)
    tot2 = jnp.sum(oh2, axis=0, keepdims=True)
    r1 = jnp.sum(oh1 * (c1 + base), axis=1, keepdims=True)
    r2 = jnp.sum(oh2 * (c2 + base + tot1), axis=1, keepdims=True)
    base = base + tot1 + tot2
    base_ref[...] = base
    cnt_ref[...] = base

    cols = (e1.astype(F32), e2.astype(F32), w1, w2, r1, r2)
    route = jnp.zeros((tm, LANES), F32)
    for c, val in enumerate(cols):
        route = jnp.where(lane == c, val, route)
    route_ref[...] = route[:, :ROUTE_COLS]


def _merge_route(x2, o_a, o_d, g_mix, w_g, w_bm, w_bd, w_out, g_ffn, wr_hi, wr_lo, b_r):
    T, D = x2.shape
    tm = min(TOK_TILE, T)
    assert T % tm == 0
    W = o_a.shape[1]
    tile = lambda w: pl.BlockSpec((tm, w), lambda i: (i, 0))
    full = lambda a: pl.BlockSpec(a.shape, lambda i: (0, 0))
    return pl.pallas_call(
        _merge_kernel,
        grid=(T // tm,),
        in_specs=[tile(D), tile(W), tile(W), full(g_mix), full(w_g), full(w_bm), full(w_bd), full(w_out),
                  full(g_ffn), full(wr_hi), full(wr_lo), full(b_r)],
        out_specs=[tile(D), tile(D // 2), tile(ROUTE_COLS), pl.BlockSpec((1, LANES), lambda i: (0, 0))],
        out_shape=[jax.ShapeDtypeStruct((T, D), F32), jax.ShapeDtypeStruct((T, D // 2), U32),
                   jax.ShapeDtypeStruct((T, ROUTE_COLS), F32), jax.ShapeDtypeStruct((1, LANES), F32)],
        scratch_shapes=[pltpu.VMEM((1, LANES), F32), pltpu.VMEM((tm, tm), BF16)],
        compiler_params=_params(("arbitrary",)),
        name="merge_route",
    )(x2, o_a, o_d, g_mix, w_g, w_bm, w_bd, w_out, g_ffn, wr_hi, wr_lo, b_r)


def _index_prefetch(dest_hbm, idx_ref, isem):
    i = pl.program_id(0)
    slot = i % 2

    def copy(step, sl):
        return pltpu.make_async_copy(dest_hbm.at[step], idx_ref.at[sl], isem.at[sl])

    @pl.when(i == 0)
    def _():
        copy(0, 0).start()

    @pl.when(i + 1 < pl.num_programs(0))
    def _():
        copy(i + 1, 1 - slot).start()

    copy(i, slot).wait()
    return slot


def _dispatch_kernel(dest_hbm, h2_ref, xz_hbm, xd_hbm, idx_ref, isem, rsem):
    del xz_hbm
    tm = h2_ref.shape[0]
    slot = _index_prefetch(dest_hbm, idx_ref, isem)

    def row(r, carry):
        for kk in range(EXPERT_TOPK):
            d = idx_ref[slot, EXPERT_TOPK * r + kk]
            pltpu.make_async_copy(h2_ref.at[pl.ds(r, 1), :], xd_hbm.at[pl.ds(d, 1), :], rsem).start()
        return carry

    lax.fori_loop(0, tm, row, 0, unroll=8)
    for kk in range(EXPERT_TOPK):
        pltpu.make_async_copy(h2_ref, xd_hbm.at[pl.ds(0, tm), :], rsem).wait()


def _dispatch(dest2, h2p, n_rows):
    T, Wp = h2p.shape
    nt, per = dest2.shape
    tm = per // EXPERT_TOPK
    xz = jnp.zeros((n_rows, Wp), U32)
    return pl.pallas_call(
        _dispatch_kernel,
        grid=(nt,),
        in_specs=[pl.BlockSpec(memory_space=pl.ANY), pl.BlockSpec((tm, Wp), lambda i: (i, 0)),
                  pl.BlockSpec(memory_space=pl.ANY)],
        out_specs=pl.BlockSpec(memory_space=pl.ANY),
        out_shape=jax.ShapeDtypeStruct((n_rows, Wp), U32),
        scratch_shapes=[pltpu.SMEM((2, per), I32), pltpu.SemaphoreType.DMA((2,)), pltpu.SemaphoreType.DMA(())],
        input_output_aliases={2: 0},
        compiler_params=_params(("arbitrary",)),
        name="moe_dispatch",
    )(dest2, h2p, xz)


def _expert_kernel(be_ref, nu_ref, xd_ref, w1_ref, w3_ref, w2_ref, yb_ref):
    del be_ref
    i = pl.program_id(0)

    @pl.when(i < nu_ref[0])
    def _():
        x = _unpack_bf16_pairs(xd_ref[...])
        act = jax.nn.silu(_dot(x, w1_ref[0])) * _dot(x, w3_ref[0])
        yb_ref[...] = _dot(act.astype(BF16), w2_ref[0])

    @pl.when(i >= nu_ref[0])
    def _():
        yb_ref[...] = jnp.zeros_like(yb_ref)


def _experts(blk_expert, n_used, xd, w1, w3, w2):
    P, Wp = xd.shape
    E, D, De = w1.shape
    nblk = P // MOE_BLK
    return pl.pallas_call(
        _expert_kernel,
        grid_spec=pltpu.PrefetchScalarGridSpec(
            num_scalar_prefetch=2,
            grid=(nblk,),
            in_specs=[
                pl.BlockSpec((MOE_BLK, Wp), lambda i, be, nu: (i, 0)),
                pl.BlockSpec((1, D, De), lambda i, be, nu: (be[i], 0, 0)),
                pl.BlockSpec((1, D, De), lambda i, be, nu: (be[i], 0, 0)),
                pl.BlockSpec((1, De, D), lambda i, be, nu: (be[i], 0, 0)),
            ],
            out_specs=pl.BlockSpec((MOE_BLK, D), lambda i, be, nu: (i, 0)),
        ),
        out_shape=jax.ShapeDtypeStruct((P, D), F32),
        compiler_params=_params(("arbitrary",)),
        name="moe_experts",
    )(blk_expert, n_used, xd, w1, w3, w2)


def _combine_kernel(dest_hbm, route_ref, x1_ref, g_ref, yb_hbm, o_ref, idx_ref, buf_ref, isem, rsem):
    tm = x1_ref.shape[0]
    slot = _index_prefetch(dest_hbm, idx_ref, isem)

    def row(r, carry):
        for kk in range(EXPERT_TOPK):
            d = idx_ref[slot, EXPERT_TOPK * r + kk]
            pltpu.make_async_copy(yb_hbm.at[pl.ds(d, 1), :], buf_ref.at[kk, pl.ds(r, 1), :], rsem).start()
        return carry

    lax.fori_loop(0, tm, row, 0, unroll=8)
    for kk in range(EXPERT_TOPK):
        pltpu.make_async_copy(yb_hbm.at[pl.ds(0, tm), :], buf_ref.at[kk], rsem).wait()
    route = route_ref[...]
    y = route[:, 2:3] * buf_ref[0] + route[:, 3:4] * buf_ref[1]
    o_ref[...] = _rms(x1_ref[...] + y, g_ref[...])


def _combine(dest2, route, x1, g_final, yb):
    T, D = x1.shape
    nt, per = dest2.shape
    tm = per // EXPERT_TOPK
    return pl.pallas_call(
        _combine_kernel,
        grid=(nt,),
        in_specs=[pl.BlockSpec(memory_space=pl.ANY), pl.BlockSpec((tm, ROUTE_COLS), lambda i: (i, 0)),
                  pl.BlockSpec((tm, D), lambda i: (i, 0)), pl.BlockSpec((1, D), lambda i: (0, 0)),
                  pl.BlockSpec(memory_space=pl.ANY)],
        out_specs=pl.BlockSpec((tm, D), lambda i: (i, 0)),
        out_shape=jax.ShapeDtypeStruct((T, D), F32),
        scratch_shapes=[pltpu.SMEM((2, per), I32), pltpu.VMEM((EXPERT_TOPK, tm, D), F32),
                        pltpu.SemaphoreType.DMA((2,)), pltpu.SemaphoreType.DMA(())],
        compiler_params=_params(("arbitrary",)),
        name="moe_combine",
    )(dest2, route, x1, g_final, yb)


def _moe(h2p, route, counts, x1, g_final, w1, w3, w2):
    T = h2p.shape[0]
    tm = min(TOK_TILE, T)
    A = T * EXPERT_TOPK
    P = -(-A // MOE_BLK) * MOE_BLK + N_EXPERTS * MOE_BLK
    nblk = P // MOE_BLK
    cnt = counts[0, :N_EXPERTS].astype(I32)
    padded = (cnt + MOE_BLK - 1) // MOE_BLK * MOE_BLK
    pends = jnp.cumsum(padded)
    pstarts = pends - padded
    ids = route[:, 0:EXPERT_TOPK].astype(I32)
    ranks = route[:, 4:4 + EXPERT_TOPK].astype(I32)
    dest2 = (pstarts[ids] + ranks).reshape(T // tm, tm * EXPERT_TOPK)
    blk_expert = jnp.minimum(jnp.searchsorted(pends, jnp.arange(nblk, dtype=I32) * MOE_BLK, side="right"),
                             N_EXPERTS - 1).astype(I32)
    n_used = (pends[-1:] // MOE_BLK).astype(I32)
    xd = _dispatch(dest2, h2p, P)
    yb = _experts(blk_expert, n_used, xd, w1, w3, w2)
    return _combine(dest2, route, x1, g_final, yb)


def _rope_tables(seq):
    inv = 1.0 / (ROPE_THETA ** (jnp.arange(0, HEAD_DIM, 2, dtype=F32) / HEAD_DIM))
    ang = jnp.arange(seq, dtype=F32)[:, None] * inv[None, :]
    ang = jnp.concatenate([ang, ang], axis=-1)
    sign = jnp.concatenate([-jnp.ones((HEAD_DIM // 2,), F32), jnp.ones((HEAD_DIM // 2,), F32)])
    reps = LANES // HEAD_DIM
    return jnp.tile(jnp.cos(ang), (1, reps)), jnp.tile(jnp.sin(ang) * sign, (1, reps))


def kernel(x, g_mix, w_in, w_branch_moba, w_branch_diff, w_out, diff_lambda_q1, diff_lambda_k1, diff_lambda_q2, diff_lambda_k2, diff_subln_g, g_ffn, w_group, b_group, w_router, b_router, w_expert_gate, w_expert_up, w_expert_down, g_final):
    B, S, D = x.shape
    assert w_in.shape[0] == 1, "one layer"
    T = B * S
    x2 = x.reshape(T, D)
    n_qkv = w_in.shape[2] - 2 * D
    w_qkv = w_in[0, :, :n_qkv].astype(BF16)
    w_g = w_in[0, :, n_qkv:].astype(BF16)
    cos, sin = _rope_tables(S)

    qa, ka, va, qd, kd, vd = _qkv_proj(x2, g_mix, w_qkv, cos, sin, S)
    W = qa.shape[1]
    o_a = _moba_attention(qa.reshape(B, S, W), ka.reshape(B, S, W), va.reshape(B, S, W)).reshape(T, W)
    o_d = _diff_attention(qd.reshape(B, S, W), kd.reshape(B, S, W), vd.reshape(B, S, W),
                          diff_lambda_q1, diff_lambda_k1, diff_lambda_q2, diff_lambda_k2, diff_subln_g).reshape(T, W)

    w_r = jnp.concatenate([w_group[0], w_router[0]], axis=1)
    w_r = jnp.pad(w_r, ((0, 0), (0, LANES - w_r.shape[1])))
    wr_hi = w_r.astype(BF16)
    wr_lo = (w_r - wr_hi.astype(F32)).astype(BF16)
    b_r = jnp.pad(jnp.concatenate([b_group[0], b_router[0]])[None, :], ((0, 0), (0, LANES - N_GROUPS - N_EXPERTS)))

    x1, h2p, route, counts = _merge_route(
        x2, o_a, o_d, g_mix, w_g, w_branch_moba[0].astype(BF16), w_branch_diff[0].astype(BF16),
        w_out[0].astype(BF16), g_ffn, wr_hi, wr_lo, b_r)

    out = _moe(h2p, route, counts, x1, g_final[None, :],
               w_expert_gate[0].astype(BF16), w_expert_up[0].astype(BF16), w_expert_down[0].astype(BF16))
    return out.reshape(B, S, D)
```

```python
import functools

import jax
import jax.numpy as jnp
from jax import lax
from jax.experimental import pallas as pl
from jax.experimental.pallas import tpu as pltpu

F32 = jnp.float32
BF16 = jnp.bfloat16
I32 = jnp.int32
U32 = jnp.uint32

LANES = 128
HEAD_DIM = 64
ATT_BLK = 256
ATT_BLK_SHIFT = 8
MOBA_TOPK = 3
N_GROUPS = 4
EXPERTS_PER_GROUP = 8
N_EXPERTS = N_GROUPS * EXPERTS_PER_GROUP
EXPERT_TOPK = 2
MOE_BLK = 256
TOK_TILE = 256
QKV_TILE = 512
ROUTE_COLS = 8
EPS = 1e-6
NEG = -1e30
ROPE_THETA = 10000.0
LAMBDA_INIT = 0.8 - 0.6 * 1.0
ATT_SCALE = HEAD_DIM ** -0.5
VMEM_LIMIT = 48 * 1024 * 1024


def _dot(a, b):
    return jnp.dot(a, b, preferred_element_type=F32)


def _dot_nt(a, b):
    return lax.dot_general(a, b, (((1,), (1,)), ((), ())), preferred_element_type=F32)


def _rms(x, g):
    var = jnp.mean(x * x, axis=-1, keepdims=True)
    return (x * lax.rsqrt(var + EPS)) * g


def _params(sem):
    return pltpu.CompilerParams(dimension_semantics=sem, vmem_limit_bytes=VMEM_LIMIT)


def _qkv_kernel(x_ref, g_ref, w_ref, cos_ref, sin_ref, qa_ref, ka_ref, va_ref, qd_ref, kd_ref, vd_ref):
    h = _rms(x_ref[...], g_ref[...]).astype(BF16)
    cos = cos_ref[...]
    sin = sin_ref[...]
    lane = lax.broadcasted_iota(I32, cos.shape, 1)
    first = (lane & (HEAD_DIM - 1)) < HEAD_DIM // 2
    width = qa_ref.shape[1]
    outs = ((qa_ref, True, ATT_SCALE), (ka_ref, True, 1.0), (va_ref, False, 1.0),
            (qd_ref, True, ATT_SCALE), (kd_ref, True, 1.0), (vd_ref, False, 1.0))
    for j, (o_ref, rope, scale) in enumerate(outs):
        y = _dot(h, w_ref[:, j * width:(j + 1) * width])
        for c in range(width // LANES):
            yc = y[:, c * LANES:(c + 1) * LANES]
            if rope:
                rot = jnp.where(first, pltpu.roll(yc, LANES - HEAD_DIM // 2, 1), pltpu.roll(yc, HEAD_DIM // 2, 1))
                yc = yc * cos + rot * sin
            if scale != 1.0:
                yc = yc * scale
            o_ref[:, c * LANES:(c + 1) * LANES] = yc.astype(BF16)


def _qkv_proj(x2, g_mix, w_qkv, cos, sin, seq):
    T, D = x2.shape
    width = w_qkv.shape[1] // 6
    tm = min(QKV_TILE, seq)
    assert seq % tm == 0 and T % tm == 0
    spt = seq // tm
    out = jax.ShapeDtypeStruct((T, width), BF16)
    return pl.pallas_call(
        _qkv_kernel,
        grid=(T // tm,),
        in_specs=[
            pl.BlockSpec((tm, D), lambda i: (i, 0)),
            pl.BlockSpec((1, D), lambda i: (0, 0)),
            pl.BlockSpec(w_qkv.shape, lambda i: (0, 0)),
            pl.BlockSpec((tm, LANES), lambda i: (i % spt, 0)),
            pl.BlockSpec((tm, LANES), lambda i: (i % spt, 0)),
        ],
        out_specs=[pl.BlockSpec((tm, width), lambda i: (i, 0))] * 6,
        out_shape=[out] * 6,
        compiler_params=_params(("arbitrary",)),
        name="qkv_proj",
    )(x2, g_mix, w_qkv, cos, sin)


def _attend(qm, k_block, v_rows, s_ref, p_ref, cbias_ref, i):
    mx = None
    for n in range(i + 1):
        sb = _dot_nt(qm, k_block(n))
        if n == i:
            sb = sb + cbias_ref[...]
        s_ref[:, n * ATT_BLK:(n + 1) * ATT_BLK] = sb
        for c in range(ATT_BLK // LANES):
            part = sb[:, c * LANES:(c + 1) * LANES]
            mx = part if mx is None else jnp.maximum(mx, part)
    mb = jnp.broadcast_to(jnp.max(mx, axis=1, keepdims=True), (ATT_BLK, LANES))
    ps = None
    for c in range((i + 1) * ATT_BLK // LANES):
        p = jnp.exp(s_ref[:, c * LANES:(c + 1) * LANES] - mb)
        ps = p if ps is None else ps + p
        p_ref[:, c * LANES:(c + 1) * LANES] = p.astype(BF16)
    l = jnp.sum(ps, axis=1, keepdims=True)
    L = (i + 1) * ATT_BLK
    return _dot(p_ref[:, :L], v_rows(L)) * (1.0 / l)


def _causal_bias():
    row = lax.broadcasted_iota(I32, (ATT_BLK, ATT_BLK), 0)
    col = lax.broadcasted_iota(I32, (ATT_BLK, ATT_BLK), 1)
    return jnp.where(col <= row, 0.0, NEG).astype(F32)


def _moba_kernel(q_ref, k_ref, v_ref, o_ref, kaug_ref, kmf_ref, kmhi_ref, kmlo_ref, cbias_ref, s_ref, p_ref):
    S = k_ref.shape[1]
    nb = S // ATT_BLK
    kf = k_ref[0].astype(F32)
    lane_s = lax.broadcasted_iota(I32, (S, LANES), 1)
    blk_s = lax.broadcasted_iota(I32, (S, LANES), 0) >> ATT_BLK_SHIFT
    kmean = jnp.mean(kf.reshape(nb, ATT_BLK, LANES), axis=1)
    lane_n = lax.broadcasted_iota(I32, (nb, LANES), 1)
    for hh in range(2):
        base = HEAD_DIM * (1 - hh)
        own_s = (lane_s < HEAD_DIM) if hh == 0 else (lane_s >= HEAD_DIM)
        own_n = (lane_n < HEAD_DIM) if hh == 0 else (lane_n >= HEAD_DIM)
        onehot = jnp.where((lane_s - base) == blk_s, 1.0, 0.0)
        kaug_ref[hh] = jnp.where(own_s, kf, onehot).astype(BF16)
        kmf_ref[...] = jnp.zeros((LANES, LANES), F32)
        kmf_ref[base:base + nb, :] = jnp.where(own_n, kmean, 0.0)
        km = kmf_ref[...]
        hi = km.astype(BF16)
        kmhi_ref[hh] = hi
        kmlo_ref[hh] = (km - hi.astype(F32)).astype(BF16)
    cbias_ref[...] = _causal_bias()

    lane = lax.broadcasted_iota(I32, (ATT_BLK, LANES), 1)

    for i in range(nb):
        r0 = i * ATT_BLK
        q = q_ref[0, r0:r0 + ATT_BLK, :].astype(F32)
        heads = []
        for hh in range(2):
            base = HEAD_DIM * (1 - hh)
            own = (lane < HEAD_DIM) if hh == 0 else (lane >= HEAD_DIM)
            qh = jnp.where(own, q, 0.0)
            if i > MOBA_TOPK:
                qhb = qh.astype(BF16)
                gate = _dot_nt(qhb, kmhi_ref[hh]) + _dot_nt(qhb, kmlo_ref[hh])
                cand = (lane >= base) & (lane < base + i)
                g = jnp.where(cand, gate, -jnp.inf)
                rank = jnp.zeros((ATT_BLK, LANES), I32)
                for r in range(1, i):
                    rank = rank + jnp.where(pltpu.roll(g, r, 1) >= g, 1, 0)
                    rank = rank + jnp.where(pltpu.roll(g, LANES - r, 1) > g, 1, 0)
                qh = jnp.where(cand & (rank >= MOBA_TOPK), NEG, qh)
            heads.append(_attend(
                qh.astype(BF16),
                lambda n, hh=hh: kaug_ref[hh, n * ATT_BLK:(n + 1) * ATT_BLK, :],
                lambda L: v_ref[0, :L, :],
                s_ref.at[hh], p_ref.at[hh], cbias_ref, i))
        o_ref[0, r0:r0 + ATT_BLK, :] = jnp.where(lane < HEAD_DIM, heads[0], heads[1]).astype(BF16)


def _moba_attention(qa, ka, va):
    B, S, W = qa.shape
    assert S % ATT_BLK == 0 and S // ATT_BLK <= HEAD_DIM // 2 and W % LANES == 0
    spec = pl.BlockSpec((1, S, LANES), lambda b, j: (b, 0, j))
    return pl.pallas_call(
        _moba_kernel,
        grid=(B, W // LANES),
        in_specs=[spec, spec, spec],
        out_specs=spec,
        out_shape=jax.ShapeDtypeStruct((B, S, W), BF16),
        scratch_shapes=[
            pltpu.VMEM((2, S, LANES), BF16),
            pltpu.VMEM((LANES, LANES), F32),
            pltpu.VMEM((2, LANES, LANES), BF16),
            pltpu.VMEM((2, LANES, LANES), BF16),
            pltpu.VMEM((ATT_BLK, ATT_BLK), F32),
            pltpu.VMEM((2, ATT_BLK, S), F32),
            pltpu.VMEM((2, ATT_BLK, S), BF16),
        ],
        compiler_params=_params(("arbitrary", "arbitrary")),
        name="moba_attention",
    )(qa, ka, va)


def _diff_kernel(lq1_ref, lk1_ref, lq2_ref, lk2_ref, g_ref, q_ref, k_ref, v_ref, o_ref, cbias_ref, s_ref, p_ref):
    S = k_ref.shape[1]
    nb = S // ATT_BLK
    lam = (jnp.exp(jnp.sum(lq1_ref[...] * lk1_ref[...], axis=1, keepdims=True))
           - jnp.exp(jnp.sum(lq2_ref[...] * lk2_ref[...], axis=1, keepdims=True)) + LAMBDA_INIT)
    cbias_ref[...] = _causal_bias()
    lane = lax.broadcasted_iota(I32, (ATT_BLK, LANES), 1)

    for i in range(nb):
        r0 = i * ATT_BLK
        q = q_ref[0, r0:r0 + ATT_BLK, :].astype(F32)
        maps = []
        for mp in range(2):
            own = (lane < HEAD_DIM) if mp == 0 else (lane >= HEAD_DIM)
            maps.append(_attend(
                jnp.where(own, q, 0.0).astype(BF16),
                lambda n: k_ref[0, n * ATT_BLK:(n + 1) * ATT_BLK, :],
                lambda L: v_ref[0, :L, :],
                s_ref.at[mp], p_ref.at[mp], cbias_ref, i))
        o = maps[0] - lam * maps[1]
        o_ref[0, r0:r0 + ATT_BLK, :] = (_rms(o, g_ref[...]) * (1.0 - LAMBDA_INIT)).astype(BF16)


def _diff_attention(qd, kd, vd, lq1, lk1, lq2, lk2, subln_g):
    B, S, W = qd.shape
    assert S % ATT_BLK == 0 and W % LANES == 0
    spec = pl.BlockSpec((1, S, LANES), lambda b, j: (b, 0, j))
    small = pl.BlockSpec((1, HEAD_DIM), lambda b, j: (0, 0))
    return pl.pallas_call(
        _diff_kernel,
        grid=(B, W // LANES),
        in_specs=[small, small, small, small, pl.BlockSpec((1, LANES), lambda b, j: (0, 0)), spec, spec, spec],
        out_specs=spec,
        out_shape=jax.ShapeDtypeStruct((B, S, W), BF16),
        scratch_shapes=[pltpu.VMEM((ATT_BLK, ATT_BLK), F32), pltpu.VMEM((2, ATT_BLK, S), F32),
                        pltpu.VMEM((2, ATT_BLK, S), BF16)],
        compiler_params=_params(("arbitrary", "arbitrary")),
        name="diff_attention",
    )(lq1, lk1, lq2, lk2, subln_g, qd, kd, vd)


def _pack_bf16_pairs(x):
    n = x.shape[1] // 2
    lo = lax.bitcast_convert_type(x[:, :n].astype(BF16).astype(F32), U32)
    hi = lax.bitcast_convert_type(x[:, n:].astype(BF16).astype(F32), U32)
    return (lo >> 16) | hi


def _unpack_bf16_pairs(w):
    lo = lax.bitcast_convert_type(w << 16, F32)
    hi = lax.bitcast_convert_type(w & jnp.uint32(0xFFFF0000), F32)
    return jnp.concatenate([lo, hi], axis=1).astype(BF16)


def _merge_kernel(x_ref, oa_ref, od_ref, gmix_ref, wg_ref, wbm_ref, wbd_ref, wout_ref, gffn_ref,
                  wrhi_ref, wrlo_ref, br_ref,
                  x1_ref, h2_ref, route_ref, cnt_ref, base_ref, ltri_ref):
    tm, D = x_ref.shape

    @pl.when(pl.program_id(0) == 0)
    def _():
        base_ref[...] = jnp.zeros_like(base_ref)
        row = lax.broadcasted_iota(I32, (tm, tm), 0)
        col = lax.broadcasted_iota(I32, (tm, tm), 1)
        ltri_ref[...] = jnp.where(col < row, 1.0, 0.0).astype(BF16)

    x = x_ref[...]
    h = _rms(x, gmix_ref[...]).astype(BF16)
    sig = jax.nn.sigmoid(_dot(h, wg_ref[...]))
    merged = sig[:, :D] * _dot(oa_ref[...], wbm_ref[...]) + sig[:, D:] * _dot(od_ref[...], wbd_ref[...])
    x1 = x + _dot(merged.astype(BF16), wout_ref[...])
    x1_ref[...] = x1
    h2 = _rms(x1, gffn_ref[...])
    h2_ref[...] = _pack_bf16_pairs(h2)

    hhi = h2.astype(BF16)
    hlo = (h2 - hhi.astype(F32)).astype(BF16)
    lg = _dot(hhi, wrhi_ref[...]) + _dot(hhi, wrlo_ref[...]) + _dot(hlo, wrhi_ref[...]) + br_ref[...]
    lane = lax.broadcasted_iota(I32, (tm, LANES), 1)
    ninf = -jnp.inf

    def first_argmax(vals):
        m = jnp.max(vals, axis=1, keepdims=True)
        return m, jnp.min(jnp.where(vals == m, lane, LANES), axis=1, keepdims=True)

    is_group = lane < N_GROUPS
    gm, gsel = first_argmax(jnp.where(is_group, lg, ninf))
    p_group = 1.0 / jnp.sum(jnp.where(is_group, jnp.exp(lg - gm), 0.0), axis=1, keepdims=True)
    lo = N_GROUPS + gsel * EXPERTS_PER_GROUP
    el = jnp.where((lane >= lo) & (lane < lo + EXPERTS_PER_GROUP), lg, ninf)
    m1, i1 = first_argmax(el)
    m2, i2 = first_argmax(jnp.where(lane == i1, ninf, el))
    e2x = jnp.exp(m2 - m1)
    w1 = p_group / (1.0 + e2x)
    w2 = p_group * e2x / (1.0 + e2x)
    e1 = i1 - N_GROUPS
    e2 = i2 - N_GROUPS

    oh1 = jnp.where(lane == e1, 1.0, 0.0)
    oh2 = jnp.where(lane == e2, 1.0, 0.0)
    c1 = _dot(ltri_ref[...], oh1.astype(BF16))
    c2 = _dot(ltri_ref[...], oh2.astype(BF16))
    base = base_ref[...]
    tot1 = jnp.sum(oh1, axis=0, keepdims=True)
    tot2 = jnp.sum(oh2, axis=0, keepdims=True)
    r1 = jnp.sum(oh1 * (c1 + base), axis=1, keepdims=True)
    r2 = jnp.sum(oh2 * (c2 + base + tot1), axis=1, keepdims=True)
    base = base + tot1 + tot2
    base_ref[...] = base
    cnt_ref[...] = base

    cols = (e1.astype(F32), e2.astype(F32), w1, w2, r1, r2)
    route = jnp.zeros((tm, LANES), F32)
    for c, val in enumerate(cols):
        route = jnp.where(lane == c, val, route)
    route_ref[...] = route[:, :ROUTE_COLS]


def _merge_route(x2, o_a, o_d, g_mix, w_g, w_bm, w_bd, w_out, g_ffn, wr_hi, wr_lo, b_r):
    T, D = x2.shape
    tm = min(TOK_TILE, T)
    assert T % tm == 0
    W = o_a.shape[1]
    tile = lambda w: pl.BlockSpec((tm, w), lambda i: (i, 0))
    full = lambda a: pl.BlockSpec(a.shape, lambda i: (0, 0))
    return pl.pallas_call(
        _merge_kernel,
        grid=(T // tm,),
        in_specs=[tile(D), tile(W), tile(W), full(g_mix), full(w_g), full(w_bm), full(w_bd), full(w_out),
                  full(g_ffn), full(wr_hi), full(wr_lo), full(b_r)],
        out_specs=[tile(D), tile(D // 2), tile(ROUTE_COLS), pl.BlockSpec((1, LANES), lambda i: (0, 0))],
        out_shape=[jax.ShapeDtypeStruct((T, D), F32), jax.ShapeDtypeStruct((T, D // 2), U32),
                   jax.ShapeDtypeStruct((T, ROUTE_COLS), F32), jax.ShapeDtypeStruct((1, LANES), F32)],
        scratch_shapes=[pltpu.VMEM((1, LANES), F32), pltpu.VMEM((tm, tm), BF16)],
        compiler_params=_params(("arbitrary",)),
        name="merge_route",
    )(x2, o_a, o_d, g_mix, w_g, w_bm, w_bd, w_out, g_ffn, wr_hi, wr_lo, b_r)


def _index_prefetch(dest_hbm, idx_ref, isem):
    i = pl.program_id(0)
    slot = i % 2

    def copy(step, sl):
        return pltpu.make_async_copy(dest_hbm.at[step], idx_ref.at[sl], isem.at[sl])

    @pl.when(i == 0)
    def _():
        copy(0, 0).start()

    @pl.when(i + 1 < pl.num_programs(0))
    def _():
        copy(i + 1, 1 - slot).start()

    copy(i, slot).wait()
    return slot


def _dispatch_kernel(dest_hbm, h2_ref, xz_hbm, xd_hbm, idx_ref, isem, rsem):
    del xz_hbm
    tm = h2_ref.shape[0]
    slot = _index_prefetch(dest_hbm, idx_ref, isem)

    def row(r, carry):
        for kk in range(EXPERT_TOPK):
            d = idx_ref[slot, EXPERT_TOPK * r + kk]
            pltpu.make_async_copy(h2_ref.at[pl.ds(r, 1), :], xd_hbm.at[pl.ds(d, 1), :], rsem).start()
        return carry

    lax.fori_loop(0, tm, row, 0, unroll=8)
    for kk in range(EXPERT_TOPK):
        pltpu.make_async_copy(h2_ref, xd_hbm.at[pl.ds(0, tm), :], rsem).wait()


def _dispatch(dest2, h2p, n_rows):
    T, Wp = h2p.shape
    nt, per = dest2.shape
    tm = per // EXPERT_TOPK
    xz = jnp.zeros((n_rows, Wp), U32)
    return pl.pallas_call(
        _dispatch_kernel,
        grid=(nt,),
        in_specs=[pl.BlockSpec(memory_space=pl.ANY), pl.BlockSpec((tm, Wp), lambda i: (i, 0)),
                  pl.BlockSpec(memory_space=pl.ANY)],
        out_specs=pl.BlockSpec(memory_space=pl.ANY),
        out_shape=jax.ShapeDtypeStruct((n_rows, Wp), U32),
        scratch_shapes=[pltpu.SMEM((2, per), I32), pltpu.SemaphoreType.DMA((2,)), pltpu.SemaphoreType.DMA(())],
        input_output_aliases={2: 0},
        compiler_params=_params(("arbitrary",)),
        name="moe_dispatch",
    )(dest2, h2p, xz)


def _expert_kernel(be_ref, nu_ref, xd_ref, w1_ref, w3_ref, w2_ref, yb_ref):
    del be_ref
    i = pl.program_id(0)

    @pl.when(i < nu_ref[0])
    def _():
        x = _unpack_bf16_pairs(xd_ref[...])
        act = jax.nn.silu(_dot(x, w1_ref[0])) * _dot(x, w3_ref[0])
        yb_ref[...] = _dot(act.astype(BF16), w2_ref[0])

    @pl.when(i >= nu_ref[0])
    def _():
        yb_ref[...] = jnp.zeros_like(yb_ref)


def _experts(blk_expert, n_used, xd, w1, w3, w2):
    P, Wp = xd.shape
    E, D, De = w1.shape
    nblk = P // MOE_BLK
    return pl.pallas_call(
        _expert_kernel,
        grid_spec=pltpu.PrefetchScalarGridSpec(
            num_scalar_prefetch=2,
            grid=(nblk,),
            in_specs=[
                pl.BlockSpec((MOE_BLK, Wp), lambda i, be, nu: (i, 0)),
                pl.BlockSpec((1, D, De), lambda i, be, nu: (be[i], 0, 0)),
                pl.BlockSpec((1, D, De), lambda i, be, nu: (be[i], 0, 0)),
                pl.BlockSpec((1, De, D), lambda i, be, nu: (be[i], 0, 0)),
            ],
            out_specs=pl.BlockSpec((MOE_BLK, D), lambda i, be, nu: (i, 0)),
        ),
        out_shape=jax.ShapeDtypeStruct((P, D), F32),
        compiler_params=_params(("arbitrary",)),
        name="moe_experts",
    )(blk_expert, n_used, xd, w1, w3, w2)


def _combine_kernel(dest_hbm, route_ref, x1_ref, g_ref, yb_hbm, o_ref, idx_ref, buf_ref, isem, rsem):
    tm = x1_ref.shape[0]
    slot = _index_prefetch(dest_hbm, idx_ref, isem)

    def row(r, carry):
        for kk in range(EXPERT_TOPK):
            d = idx_ref[slot, EXPERT_TOPK * r + kk]
            pltpu.make_async_copy(yb_hbm.at[pl.ds(d, 1), :], buf_ref.at[kk, pl.ds(r, 1), :], rsem).start()
        return carry

    lax.fori_loop(0, tm, row, 0, unroll=8)
    for kk in range(EXPERT_TOPK):
        pltpu.make_async_copy(yb_hbm.at[pl.ds(0, tm), :], buf_ref.at[kk], rsem).wait()
    route = route_ref[...]
    y = route[:, 2:3] * buf_ref[0] + route[:, 3:4] * buf_ref[1]
    o_ref[...] = _rms(x1_ref[...] + y, g_ref[...])


def _combine(dest2, route, x1, g_final, yb):
    T, D = x1.shape
    nt, per = dest2.shape
    tm = per // EXPERT_TOPK
    return pl.pallas_call(
        _combine_kernel,
        grid=(nt,),
        in_specs=[pl.BlockSpec(memory_space=pl.ANY), pl.BlockSpec((tm, ROUTE_COLS), lambda i: (i, 0)),
                  pl.BlockSpec((tm, D), lambda i: (i, 0)), pl.BlockSpec((1, D), lambda i: (0, 0)),
                  pl.BlockSpec(memory_space=pl.ANY)],
        out_specs=pl.BlockSpec((tm, D), lambda i: (i, 0)),
        out_shape=jax.ShapeDtypeStruct((T, D), F32),
        scratch_shapes=[pltpu.SMEM((2, per), I32), pltpu.VMEM((EXPERT_TOPK, tm, D), F32),
                        pltpu.SemaphoreType.DMA((2,)), pltpu.SemaphoreType.DMA(())],
        compiler_params=_params(("arbitrary",)),
        name="moe_combine",
    )(dest2, route, x1, g_final, yb)


def _moe(h2p, route, counts, x1, g_final, w1, w3, w2):
    T = h2p.shape[0]
    tm = min(TOK_TILE, T)
    A = T * EXPERT_TOPK
    P = -(-A // MOE_BLK) * MOE_BLK + N_EXPERTS * MOE_BLK
    nblk = P // MOE_BLK
    cnt = counts[0, :N_EXPERTS].astype(I32)
    padded = (cnt + MOE_BLK - 1) // MOE_BLK * MOE_BLK
    pends = jnp.cumsum(padded)
    pstarts = pends - padded
    ids = route[:, 0:EXPERT_TOPK].astype(I32)
    ranks = route[:, 4:4 + EXPERT_TOPK].astype(I32)
    dest2 = (pstarts[ids] + ranks).reshape(T // tm, tm * EXPERT_TOPK)
    blk_expert = jnp.minimum(jnp.searchsorted(pends, jnp.arange(nblk, dtype=I32) * MOE_BLK, side="right"),
                             N_EXPERTS - 1).astype(I32)
    n_used = (pends[-1:] // MOE_BLK).astype(I32)
    xd = _dispatch(dest2, h2p, P)
    yb = _experts(blk_expert, n_used, xd, w1, w3, w2)
    return _combine(dest2, route, x1, g_final, yb)


def _rope_tables(seq):
    inv = 1.0 / (ROPE_THETA ** (jnp.arange(0, HEAD_DIM, 2, dtype=F32) / HEAD_DIM))
    ang = jnp.arange(seq, dtype=F32)[:, None] * inv[None, :]
    ang = jnp.concatenate([ang, ang], axis=-1)
    sign = jnp.concatenate([-jnp.ones((HEAD_DIM // 2,), F32), jnp.ones((HEAD_DIM // 2,), F32)])
    reps = LANES // HEAD_DIM
    return jnp.tile(jnp.cos(ang), (1, reps)), jnp.tile(jnp.sin(ang) * sign, (1, reps))


def kernel(x, g_mix, w_in, w_branch_moba, w_branch_diff, w_out, diff_lambda_q1, diff_lambda_k1, diff_lambda_q2, diff_lambda_k2, diff_subln_g, g_ffn, w_group, b_group, w_router, b_router, w_expert_gate, w_expert_up, w_expert_down, g_final):
    B, S, D = x.shape
    assert w_in.shape[0] == 1, "one layer"
    T = B * S
    x2 = x.reshape(T, D)
    n_qkv = w_in.shape[2] - 2 * D
    w_qkv = w_in[0, :, :n_qkv].astype(BF16)
    w_g = w_in[0, :, n_qkv:].astype(BF16)
    cos, sin = _rope_tables(S)

    qa, ka, va, qd, kd, vd = _qkv_proj(x2, g_mix, w_qkv, cos, sin, S)
    W = qa.shape[1]
    o_a = _moba_attention(qa.reshape(B, S, W), ka.reshape(B, S, W), va.reshape(B, S, W)).reshape(T, W)
    o_d = _diff_attention(qd.reshape(B, S, W), kd.reshape(B, S, W), vd.reshape(B, S, W),
                          diff_lambda_q1, diff_lambda_k1, diff_lambda_q2, diff_lambda_k2, diff_subln_g).reshape(T, W)

    w_r = jnp.concatenate([w_group[0], w_router[0]], axis=1)
    w_r = jnp.pad(w_r, ((0, 0), (0, LANES - w_r.shape[1])))
    wr_hi = w_r.astype(BF16)
    wr_lo = (w_r - wr_hi.astype(F32)).astype(BF16)
    b_r = jnp.pad(jnp.concatenate([b_group[0], b_router[0]])[None, :], ((0, 0), (0, LANES - N_GROUPS - N_EXPERTS)))

    x1, h2p, route, counts = _merge_route(
        x2, o_a, o_d, g_mix, w_g, w_branch_moba[0].astype(BF16), w_branch_diff[0].astype(BF16),
        w_out[0].astype(BF16), g_ffn, wr_hi, wr_lo, b_r)

    out = _moe(h2p, route, counts, x1, g_final[None, :],
               w_expert_gate[0].astype(BF16), w_expert_up[0].astype(BF16), w_expert_down[0].astype(BF16))
    return out.reshape(B, S, D)
```

```python
import functools

import jax
import jax.numpy as jnp
from jax import lax
from jax.experimental import pallas as pl
from jax.experimental.pallas import tpu as pltpu

F32 = jnp.float32
BF16 = jnp.bfloat16
I32 = jnp.int32
U32 = jnp.uint32

LANES = 128
HEAD_DIM = 64
ATT_BLK = 256
ATT_BLK_SHIFT = 8
MOBA_TOPK = 3
N_GROUPS = 4
EXPERTS_PER_GROUP = 8
N_EXPERTS = N_GROUPS * EXPERTS_PER_GROUP
EXPERT_TOPK = 2
MOE_BLK = 256
TOK_TILE = 256
QKV_TILE = 512
ROUTE_COLS = 8
EPS = 1e-6
NEG = -1e30
ROPE_THETA = 10000.0
LAMBDA_INIT = 0.8 - 0.6 * 1.0
ATT_SCALE = HEAD_DIM ** -0.5
VMEM_LIMIT = 48 * 1024 * 1024


def _dot(a, b):
    return jnp.dot(a, b, preferred_element_type=F32)


def _dot_nt(a, b):
    return lax.dot_general(a, b, (((1,), (1,)), ((), ())), preferred_element_type=F32)


def _rms(x, g):
    var = jnp.mean(x * x, axis=-1, keepdims=True)
    return (x * lax.rsqrt(var + EPS)) * g


def _params(sem):
    return pltpu.CompilerParams(dimension_semantics=sem, vmem_limit_bytes=VMEM_LIMIT)


def _qkv_kernel(x_ref, g_ref, w_ref, cos_ref, sin_ref, qa_ref, ka_ref, va_ref, qd_ref, kd_ref, vd_ref):
    h = _rms(x_ref[...], g_ref[...]).astype(BF16)
    cos = cos_ref[...]
    sin = sin_ref[...]
    lane = lax.broadcasted_iota(I32, cos.shape, 1)
    first = (lane & (HEAD_DIM - 1)) < HEAD_DIM // 2
    width = qa_ref.shape[1]
    outs = ((qa_ref, True, ATT_SCALE), (ka_ref, True, 1.0), (va_ref, False, 1.0),
            (qd_ref, True, ATT_SCALE), (kd_ref, True, 1.0), (vd_ref, False, 1.0))
    for j, (o_ref, rope, scale) in enumerate(outs):
        y = _dot(h, w_ref[:, j * width:(j + 1) * width])
        for c in range(width // LANES):
            yc = y[:, c * LANES:(c + 1) * LANES]
            if rope:
                rot = jnp.where(first, pltpu.roll(yc, LANES - HEAD_DIM // 2, 1), pltpu.roll(yc, HEAD_DIM // 2, 1))
                yc = yc * cos + rot * sin
            if scale != 1.0:
                yc = yc * scale
            o_ref[:, c * LANES:(c + 1) * LANES] = yc.astype(BF16)


def _qkv_proj(x2, g_mix, w_qkv, cos, sin, seq):
    T, D = x2.shape
    width = w_qkv.shape[1] // 6
    tm = min(QKV_TILE, seq)
    assert seq % tm == 0 and T % tm == 0
    spt = seq // tm
    out = jax.ShapeDtypeStruct((T, width), BF16)
    return pl.pallas_call(
        _qkv_kernel,
        grid=(T // tm,),
        in_specs=[
            pl.BlockSpec((tm, D), lambda i: (i, 0)),
            pl.BlockSpec((1, D), lambda i: (0, 0)),
            pl.BlockSpec(w_qkv.shape, lambda i: (0, 0)),
            pl.BlockSpec((tm, LANES), lambda i: (i % spt, 0)),
            pl.BlockSpec((tm, LANES), lambda i: (i % spt, 0)),
        ],
        out_specs=[pl.BlockSpec((tm, width), lambda i: (i, 0))] * 6,
        out_shape=[out] * 6,
        compiler_params=_params(("arbitrary",)),
        name="qkv_proj",
    )(x2, g_mix, w_qkv, cos, sin)


def _attend(qm, k_block, v_rows, s_ref, p_ref, cbias_ref, i):
    mx = None
    for n in range(i + 1):
        sb = _dot_nt(qm, k_block(n))
        if n == i:
            sb = sb + cbias_ref[...]
        s_ref[:, n * ATT_BLK:(n + 1) * ATT_BLK] = sb
        for c in range(ATT_BLK // LANES):
            part = sb[:, c * LANES:(c + 1) * LANES]
            mx = part if mx is None else jnp.maximum(mx, part)
    mb = jnp.broadcast_to(jnp.max(mx, axis=1, keepdims=True), (ATT_BLK, LANES))
    ps = None
    for c in range((i + 1) * ATT_BLK // LANES):
        p = jnp.exp(s_ref[:, c * LANES:(c + 1) * LANES] - mb)
        ps = p if ps is None else ps + p
        p_ref[:, c * LANES:(c + 1) * LANES] = p.astype(BF16)
    l = jnp.sum(ps, axis=1, keepdims=True)
    L = (i + 1) * ATT_BLK
    return _dot(p_ref[:, :L], v_rows(L)) * (1.0 / l)


def _causal_bias():
    row = lax.broadcasted_iota(I32, (ATT_BLK, ATT_BLK), 0)
    col = lax.broadcasted_iota(I32, (ATT_BLK, ATT_BLK), 1)
    return jnp.where(col <= row, 0.0, NEG).astype(F32)


def _moba_kernel(q_ref, k_ref, v_ref, o_ref, kaug_ref, kmf_ref, kmhi_ref, kmlo_ref, cbias_ref, s_ref, p_ref):
    S = k_ref.shape[1]
    nb = S // ATT_BLK
    kf = k_ref[0].astype(F32)
    lane_s = lax.broadcasted_iota(I32, (S, LANES), 1)
    blk_s = lax.broadcasted_iota(I32, (S, LANES), 0) >> ATT_BLK_SHIFT
    kmean = jnp.mean(kf.reshape(nb, ATT_BLK, LANES), axis=1)
    lane_n = lax.broadcasted_iota(I32, (nb, LANES), 1)
    for hh in range(2):
        base = HEAD_DIM * (1 - hh)
        own_s = (lane_s < HEAD_DIM) if hh == 0 else (lane_s >= HEAD_DIM)
        own_n = (lane_n < HEAD_DIM) if hh == 0 else (lane_n >= HEAD_DIM)
        onehot = jnp.where((lane_s - base) == blk_s, 1.0, 0.0)
        kaug_ref[hh] = jnp.where(own_s, kf, onehot).astype(BF16)
        kmf_ref[...] = jnp.zeros((LANES, LANES), F32)
        kmf_ref[base:base + nb, :] = jnp.where(own_n, kmean, 0.0)
        km = kmf_ref[...]
        hi = km.astype(BF16)
        kmhi_ref[hh] = hi
        kmlo_ref[hh] = (km - hi.astype(F32)).astype(BF16)
    cbias_ref[...] = _causal_bias()

    lane = lax.broadcasted_iota(I32, (ATT_BLK, LANES), 1)

    for i in range(nb):
        r0 = i * ATT_BLK
        q = q_ref[0, r0:r0 + ATT_BLK, :].astype(F32)
        heads = []
        for hh in range(2):
            base = HEAD_DIM * (1 - hh)
            own = (lane < HEAD_DIM) if hh == 0 else (lane >= HEAD_DIM)
            qh = jnp.where(own, q, 0.0)
            if i > MOBA_TOPK:
                qhb = qh.astype(BF16)
                gate = _dot_nt(qhb, kmhi_ref[hh]) + _dot_nt(qhb, kmlo_ref[hh])
                cand = (lane >= base) & (lane < base + i)
                g = jnp.where(cand, gate, -jnp.inf)
                rank = jnp.zeros((ATT_BLK, LANES), I32)
                for r in range(1, i):
                    rank = rank + jnp.where(pltpu.roll(g, r, 1) >= g, 1, 0)
                    rank = rank + jnp.where(pltpu.roll(g, LANES - r, 1) > g, 1, 0)
                qh = jnp.where(cand & (rank >= MOBA_TOPK), NEG, qh)
            heads.append(_attend(
                qh.astype(BF16),
                lambda n, hh=hh: kaug_ref[hh, n * ATT_BLK:(n + 1) * ATT_BLK, :],
                lambda L: v_ref[0, :L, :],
                s_ref.at[hh], p_ref.at[hh], cbias_ref, i))
        o_ref[0, r0:r0 + ATT_BLK, :] = jnp.where(lane < HEAD_DIM, heads[0], heads[1]).astype(BF16)


def _moba_attention(qa, ka, va):
    B, S, W = qa.shape
    assert S % ATT_BLK == 0 and S // ATT_BLK <= HEAD_DIM // 2 and W % LANES == 0
    spec = pl.BlockSpec((1, S, LANES), lambda b, j: (b, 0, j))
    return pl.pallas_call(
        _moba_kernel,
        grid=(B, W // LANES),
        in_specs=[spec, spec, spec],
        out_specs=spec,
        out_shape=jax.ShapeDtypeStruct((B, S, W), BF16),
        scratch_shapes=[
            pltpu.VMEM((2, S, LANES), BF16),
            pltpu.VMEM((LANES, LANES), F32),
            pltpu.VMEM((2, LANES, LANES), BF16),
            pltpu.VMEM((2, LANES, LANES), BF16),
            pltpu.VMEM((ATT_BLK, ATT_BLK), F32),
            pltpu.VMEM((2, ATT_BLK, S), F32),
            pltpu.VMEM((2, ATT_BLK, S), BF16),
        ],
        compiler_params=_params(("arbitrary", "arbitrary")),
        name="moba_attention",
    )(qa, ka, va)


def _diff_kernel(lq1_ref, lk1_ref, lq2_ref, lk2_ref, g_ref, q_ref, k_ref, v_ref, o_ref, cbias_ref, s_ref, p_ref):
    S = k_ref.shape[1]
    nb = S // ATT_BLK
    lam = (jnp.exp(jnp.sum(lq1_ref[...] * lk1_ref[...], axis=1, keepdims=True))
           - jnp.exp(jnp.sum(lq2_ref[...] * lk2_ref[...], axis=1, keepdims=True)) + LAMBDA_INIT)
    cbias_ref[...] = _causal_bias()
    lane = lax.broadcasted_iota(I32, (ATT_BLK, LANES), 1)

    for i in range(nb):
        r0 = i * ATT_BLK
        q = q_ref[0, r0:r0 + ATT_BLK, :].astype(F32)
        maps = []
        for mp in range(2):
            own = (lane < HEAD_DIM) if mp == 0 else (lane >= HEAD_DIM)
            maps.append(_attend(
                jnp.where(own, q, 0.0).astype(BF16),
                lambda n: k_ref[0, n * ATT_BLK:(n + 1) * ATT_BLK, :],
                lambda L: v_ref[0, :L, :],
                s_ref.at[mp], p_ref.at[mp], cbias_ref, i))
        o = maps[0] - lam * maps[1]
        o_ref[0, r0:r0 + ATT_BLK, :] = (_rms(o, g_ref[...]) * (1.0 - LAMBDA_INIT)).astype(BF16)


def _diff_attention(qd, kd, vd, lq1, lk1, lq2, lk2, subln_g):
    B, S, W = qd.shape
    assert S % ATT_BLK == 0 and W % LANES == 0
    spec = pl.BlockSpec((1, S, LANES), lambda b, j: (b, 0, j))
    small = pl.BlockSpec((1, HEAD_DIM), lambda b, j: (0, 0))
    return pl.pallas_call(
        _diff_kernel,
        grid=(B, W // LANES),
        in_specs=[small, small, small, small, pl.BlockSpec((1, LANES), lambda b, j: (0, 0)), spec, spec, spec],
        out_specs=spec,
        out_shape=jax.ShapeDtypeStruct((B, S, W), BF16),
        scratch_shapes=[pltpu.VMEM((ATT_BLK, ATT_BLK), F32), pltpu.VMEM((2, ATT_BLK, S), F32),
                        pltpu.VMEM((2, ATT_BLK, S), BF16)],
        compiler_params=_params(("arbitrary", "arbitrary")),
        name="diff_attention",
    )(lq1, lk1, lq2, lk2, subln_g, qd, kd, vd)


def _pack_bf16_pairs(x):
    n = x.shape[1] // 2
    lo = lax.bitcast_convert_type(x[:, :n].astype(BF16).astype(F32), U32)
    hi = lax.bitcast_convert_type(x[:, n:].astype(BF16).astype(F32), U32)
    return (lo >> 16) | hi


def _unpack_bf16_pairs(w):
    lo = lax.bitcast_convert_type(w << 16, F32)
    hi = lax.bitcast_convert_type(w & jnp.uint32(0xFFFF0000), F32)
    return jnp.concatenate([lo, hi], axis=1).astype(BF16)


def _merge_kernel(x_ref, oa_ref, od_ref, gmix_ref, wg_ref, wbm_ref, wbd_ref, wout_ref, gffn_ref,
                  wrhi_ref, wrlo_ref, br_ref,
                  x1_ref, h2_ref, route_ref, routet_ref, cnt_ref, base_ref, ltri_ref):
    tm, D = x_ref.shape

    @pl.when(pl.program_id(0) == 0)
    def _():
        base_ref[...] = jnp.zeros_like(base_ref)
        row = lax.broadcasted_iota(I32, (tm, tm), 0)
        col = lax.broadcasted_iota(I32, (tm, tm), 1)
        ltri_ref[...] = jnp.where(col < row, 1.0, 0.0).astype(BF16)

    x = x_ref[...]
    h = _rms(x, gmix_ref[...]).astype(BF16)
    sig = jax.nn.sigmoid(_dot(h, wg_ref[...]))
    merged = sig[:, :D] * _dot(oa_ref[...], wbm_ref[...]) + sig[:, D:] * _dot(od_ref[...], wbd_ref[...])
    x1 = x + _dot(merged.astype(BF16), wout_ref[...])
    x1_ref[...] = x1
    h2 = _rms(x1, gffn_ref[...])
    h2_ref[...] = _pack_bf16_pairs(h2)

    hhi = h2.astype(BF16)
    hlo = (h2 - hhi.astype(F32)).astype(BF16)
    lg = _dot(hhi, wrhi_ref[...]) + _dot(hhi, wrlo_ref[...]) + _dot(hlo, wrhi_ref[...]) + br_ref[...]
    lane = lax.broadcasted_iota(I32, (tm, LANES), 1)
    ninf = -jnp.inf

    def first_argmax(vals):
        m = jnp.max(vals, axis=1, keepdims=True)
        return m, jnp.min(jnp.where(vals == m, lane, LANES), axis=1, keepdims=True)

    is_group = lane < N_GROUPS
    gm, gsel = first_argmax(jnp.where(is_group, lg, ninf))
    p_group = 1.0 / jnp.sum(jnp.where(is_group, jnp.exp(lg - gm), 0.0), axis=1, keepdims=True)
    lo = N_GROUPS + gsel * EXPERTS_PER_GROUP
    el = jnp.where((lane >= lo) & (lane < lo + EXPERTS_PER_GROUP), lg, ninf)
    m1, i1 = first_argmax(el)
    m2, i2 = first_argmax(jnp.where(lane == i1, ninf, el))
    e2x = jnp.exp(m2 - m1)
    w1 = p_group / (1.0 + e2x)
    w2 = p_group * e2x / (1.0 + e2x)
    e1 = i1 - N_GROUPS
    e2 = i2 - N_GROUPS

    oh1 = jnp.where(lane == e1, 1.0, 0.0)
    oh2 = jnp.where(lane == e2, 1.0, 0.0)
    c1 = _dot(ltri_ref[...], oh1.astype(BF16))
    c2 = _dot(ltri_ref[...], oh2.astype(BF16))
    base = base_ref[...]
    tot1 = jnp.sum(oh1, axis=0, keepdims=True)
    tot2 = jnp.sum(oh2, axis=0, keepdims=True)
    r1 = jnp.sum(oh1 * (c1 + base), axis=1, keepdims=True)
    r2 = jnp.sum(oh2 * (c2 + base + tot1), axis=1, keepdims=True)
    base = base + tot1 + tot2
    base_ref[...] = base
    cnt_ref[...] = base

    cols = (e1.astype(F32), e2.astype(F32), w1, w2, r1, r2)
    route = jnp.zeros((tm, LANES), F32)
    for c, val in enumerate(cols):
        route = jnp.where(lane == c, val, route)
    route_ref[...] = route[:, :ROUTE_COLS]
    routet_ref[...] = route.T[:ROUTE_COLS, :]


def _merge_route(x2, o_a, o_d, g_mix, w_g, w_bm, w_bd, w_out, g_ffn, wr_hi, wr_lo, b_r):
    T, D = x2.shape
    tm = min(TOK_TILE, T)
    assert T % tm == 0
    W = o_a.shape[1]
    tile = lambda w: pl.BlockSpec((tm, w), lambda i: (i, 0))
    full = lambda a: pl.BlockSpec(a.shape, lambda i: (0, 0))
    return pl.pallas_call(
        _merge_kernel,
        grid=(T // tm,),
        in_specs=[tile(D), tile(W), tile(W), full(g_mix), full(w_g), full(w_bm), full(w_bd), full(w_out),
                  full(g_ffn), full(wr_hi), full(wr_lo), full(b_r)],
        out_specs=[tile(D), tile(D // 2), tile(ROUTE_COLS), pl.BlockSpec((ROUTE_COLS, tm), lambda i: (0, i)),
                   pl.BlockSpec((1, LANES), lambda i: (0, 0))],
        out_shape=[jax.ShapeDtypeStruct((T, D), F32), jax.ShapeDtypeStruct((T, D // 2), U32),
                   jax.ShapeDtypeStruct((T, ROUTE_COLS), F32), jax.ShapeDtypeStruct((ROUTE_COLS, T), F32),
                   jax.ShapeDtypeStruct((1, LANES), F32)],
        scratch_shapes=[pltpu.VMEM((1, LANES), F32), pltpu.VMEM((tm, tm), BF16)],
        compiler_params=_params(("arbitrary",)),
        name="merge_route",
    )(x2, o_a, o_d, g_mix, w_g, w_bm, w_bd, w_out, g_ffn, wr_hi, wr_lo, b_r)


def _index_prefetch(dest_hbm, idx_ref, isem):
    i = pl.program_id(0)
    per = dest_hbm.shape[1]
    slot = i % 2

    def copy(step, sl):
        return pltpu.make_async_copy(dest_hbm.at[step], idx_ref.at[pl.ds(sl * per, per)], isem.at[sl])

    @pl.when(i == 0)
    def _():
        copy(0, 0).start()

    @pl.when(i + 1 < pl.num_programs(0))
    def _():
        copy(i + 1, 1 - slot).start()

    copy(i, slot).wait()
    return slot * per


def _dispatch_kernel(zs_ref, nu_ref, dest_hbm, h2_ref, xd_hbm, idx_ref, zero_ref, isem, rsem, zsem):
    tm = h2_ref.shape[0] * 8
    nblk = xd_hbm.shape[0] // MOE_BLK

    @pl.when(pl.program_id(0) == 0)
    def _():
        zero_ref[...] = jnp.zeros_like(zero_ref)

        def zcopy(row):
            return pltpu.make_async_copy(zero_ref, xd_hbm.at[pl.ds(pl.multiple_of(row, MOE_BLK), MOE_BLK), :], zsem)

        def tail_start(blk, carry):
            zcopy(blk * MOE_BLK).start()
            return carry

        def tail_wait(blk, carry):
            zcopy(0).wait()
            return carry

        for e in range(N_EXPERTS):
            @pl.when(zs_ref[e] >= 0)
            def _():
                zcopy(zs_ref[e]).start()
        lax.fori_loop(nu_ref[0], nblk, tail_start, 0)
        for e in range(N_EXPERTS):
            @pl.when(zs_ref[e] >= 0)
            def _():
                zcopy(0).wait()
        lax.fori_loop(nu_ref[0], nblk, tail_wait, 0)

    base = _index_prefetch(dest_hbm, idx_ref, isem)

    def group(j, carry):
        for u in range(8):
            for kk in range(EXPERT_TOPK):
                d = idx_ref[base + kk * tm + 8 * j + u]
                pltpu.make_async_copy(h2_ref.at[j, pl.ds(u, 1), :], xd_hbm.at[pl.ds(d, 1), :],
                                      rsem).start(priority=kk)
        return carry

    lax.fori_loop(0, tm // 8, group, 0)
    for kk in range(EXPERT_TOPK):
        pltpu.make_async_copy(h2_ref, h2_ref, rsem).wait()


def _dispatch(zstart, n_used, dest2, h2p, n_rows):
    T, Wp = h2p.shape
    nt, per = dest2.shape
    tm = per // EXPERT_TOPK
    return pl.pallas_call(
        _dispatch_kernel,
        grid_spec=pltpu.PrefetchScalarGridSpec(
            num_scalar_prefetch=2,
            grid=(nt,),
            in_specs=[pl.BlockSpec(memory_space=pl.ANY),
                      pl.BlockSpec((tm // 8, 8, Wp), lambda i, zs, nu: (i, 0, 0))],
            out_specs=pl.BlockSpec(memory_space=pl.ANY),
            scratch_shapes=[pltpu.SMEM((2 * per,), I32), pltpu.VMEM((MOE_BLK, Wp), U32),
                            pltpu.SemaphoreType.DMA((2,)), pltpu.SemaphoreType.DMA(()),
                            pltpu.SemaphoreType.DMA(())],
        ),
        out_shape=jax.ShapeDtypeStruct((n_rows, Wp), U32),
        compiler_params=_params(("arbitrary",)),
        name="moe_dispatch",
    )(zstart, n_used, dest2, h2p.reshape(T // 8, 8, Wp))


def _expert_kernel(be_ref, nu_ref, xd_ref, w1_ref, w3_ref, w2_ref, yb_ref):
    del be_ref
    i = pl.program_id(0)

    @pl.when(i < nu_ref[0])
    def _():
        x = _unpack_bf16_pairs(xd_ref[...])
        act = jax.nn.silu(_dot(x, w1_ref[0])) * _dot(x, w3_ref[0])
        yb_ref[...] = _dot(act.astype(BF16), w2_ref[0])

    @pl.when(i >= nu_ref[0])
    def _():
        yb_ref[...] = jnp.zeros_like(yb_ref)


def _experts(blk_expert, n_used, xd, w1, w3, w2):
    P, Wp = xd.shape
    E, D, De = w1.shape
    nblk = P // MOE_BLK
    return pl.pallas_call(
        _expert_kernel,
        grid_spec=pltpu.PrefetchScalarGridSpec(
            num_scalar_prefetch=2,
            grid=(nblk,),
            in_specs=[
                pl.BlockSpec((MOE_BLK, Wp), lambda i, be, nu: (jnp.minimum(i, nu[0] - 1), 0)),
                pl.BlockSpec((1, D, De), lambda i, be, nu: (be[i], 0, 0)),
                pl.BlockSpec((1, D, De), lambda i, be, nu: (be[i], 0, 0)),
                pl.BlockSpec((1, De, D), lambda i, be, nu: (be[i], 0, 0)),
            ],
            out_specs=pl.BlockSpec((MOE_BLK, D), lambda i, be, nu: (i, 0)),
        ),
        out_shape=jax.ShapeDtypeStruct((P, D), F32),
        compiler_params=_params(("arbitrary",)),
        name="moe_experts",
    )(blk_expert, n_used, xd, w1, w3, w2)


def _combine_kernel(dest_hbm, route_ref, x1_ref, g_ref, yb_hbm, o_ref, idx_ref, buf_ref, isem, rsem):
    tm, D = x1_ref.shape
    base = _index_prefetch(dest_hbm, idx_ref, isem)

    def group(j, carry):
        for u in range(8):
            for kk in range(EXPERT_TOPK):
                d = idx_ref[base + kk * tm + 8 * j + u]
                pltpu.make_async_copy(yb_hbm.at[pl.ds(d, 1), :], buf_ref.at[kk, j, pl.ds(u, 1), :],
                                      rsem).start(priority=kk)
        return carry

    lax.fori_loop(0, tm // 8, group, 0)
    for kk in range(EXPERT_TOPK):
        pltpu.make_async_copy(buf_ref.at[kk], buf_ref.at[kk], rsem).wait()
    route = route_ref[...]
    y = route[:, 2:3] * buf_ref[0].reshape(tm, D) + route[:, 3:4] * buf_ref[1].reshape(tm, D)
    o_ref[...] = _rms(x1_ref[...] + y, g_ref[...])


def _combine(dest2, route, x1, g_final, yb):
    T, D = x1.shape
    nt, per = dest2.shape
    tm = per // EXPERT_TOPK
    return pl.pallas_call(
        _combine_kernel,
        grid=(nt,),
        in_specs=[pl.BlockSpec(memory_space=pl.ANY), pl.BlockSpec((tm, ROUTE_COLS), lambda i: (i, 0)),
                  pl.BlockSpec((tm, D), lambda i: (i, 0)), pl.BlockSpec((1, D), lambda i: (0, 0)),
                  pl.BlockSpec(memory_space=pl.ANY)],
        out_specs=pl.BlockSpec((tm, D), lambda i: (i, 0)),
        out_shape=jax.ShapeDtypeStruct((T, D), F32),
        scratch_shapes=[pltpu.SMEM((2 * per,), I32), pltpu.VMEM((EXPERT_TOPK, tm // 8, 8, D), F32),
                        pltpu.SemaphoreType.DMA((2,)), pltpu.SemaphoreType.DMA(())],
        compiler_params=_params(("arbitrary",)),
        name="moe_combine",
    )(dest2, route, x1, g_final, yb)


def _moe(h2p, route, route_t, counts, x1, g_final, w1, w3, w2):
    T = h2p.shape[0]
    tm = min(TOK_TILE, T)
    nt = T // tm
    A = T * EXPERT_TOPK
    P = -(-A // MOE_BLK) * MOE_BLK + N_EXPERTS * MOE_BLK
    nblk = P // MOE_BLK
    cnt = counts[0, :N_EXPERTS].astype(I32)
    padded = (cnt + MOE_BLK - 1) // MOE_BLK * MOE_BLK
    pends = jnp.cumsum(padded)
    pstarts = pends - padded
    ids = route_t[0:EXPERT_TOPK].astype(I32)
    ranks = route_t[4:4 + EXPERT_TOPK].astype(I32)
    dest = jnp.take(pstarts, ids) + ranks
    dest2 = dest.reshape(EXPERT_TOPK, nt, tm).transpose(1, 0, 2).reshape(nt, EXPERT_TOPK * tm)
    blk_row = jnp.arange(nblk, dtype=I32) * MOE_BLK
    blk_expert = jnp.minimum(jnp.sum((pends[None, :] <= blk_row[:, None]).astype(I32), axis=1), N_EXPERTS - 1)
    n_used = (pends[-1:] // MOE_BLK).astype(I32)
    zstart = jnp.where(padded > 0, pends - MOE_BLK, -1).astype(I32)
    xd = _dispatch(zstart, n_used, dest2, h2p, P)
    yb = _experts(blk_expert, n_used, xd, w1, w3, w2)
    return _combine(dest2, route, x1, g_final, yb)


def _rope_tables(seq):
    inv = 1.0 / (ROPE_THETA ** (jnp.arange(0, HEAD_DIM, 2, dtype=F32) / HEAD_DIM))
    ang = jnp.arange(seq, dtype=F32)[:, None] * inv[None, :]
    ang = jnp.concatenate([ang, ang], axis=-1)
    sign = jnp.concatenate([-jnp.ones((HEAD_DIM // 2,), F32), jnp.ones((HEAD_DIM // 2,), F32)])
    reps = LANES // HEAD_DIM
    return jnp.tile(jnp.cos(ang), (1, reps)), jnp.tile(jnp.sin(ang) * sign, (1, reps))


def kernel(x, g_mix, w_in, w_branch_moba, w_branch_diff, w_out, diff_lambda_q1, diff_lambda_k1, diff_lambda_q2, diff_lambda_k2, diff_subln_g, g_ffn, w_group, b_group, w_router, b_router, w_expert_gate, w_expert_up, w_expert_down, g_final):
    B, S, D = x.shape
    assert w_in.shape[0] == 1, "one layer"
    T = B * S
    x2 = x.reshape(T, D)
    n_qkv = w_in.shape[2] - 2 * D
    w_qkv = w_in[0, :, :n_qkv].astype(BF16)
    w_g = w_in[0, :, n_qkv:].astype(BF16)
    cos, sin = _rope_tables(S)

    qa, ka, va, qd, kd, vd = _qkv_proj(x2, g_mix, w_qkv, cos, sin, S)
    W = qa.shape[1]
    o_a = _moba_attention(qa.reshape(B, S, W), ka.reshape(B, S, W), va.reshape(B, S, W)).reshape(T, W)
    o_d = _diff_attention(qd.reshape(B, S, W), kd.reshape(B, S, W), vd.reshape(B, S, W),
                          diff_lambda_q1, diff_lambda_k1, diff_lambda_q2, diff_lambda_k2, diff_subln_g).reshape(T, W)

    w_r = jnp.concatenate([w_group[0], w_router[0]], axis=1)
    w_r = jnp.pad(w_r, ((0, 0), (0, LANES - w_r.shape[1])))
    wr_hi = w_r.astype(BF16)
    wr_lo = (w_r - wr_hi.astype(F32)).astype(BF16)
    b_r = jnp.pad(jnp.concatenate([b_group[0], b_router[0]])[None, :], ((0, 0), (0, LANES - N_GROUPS - N_EXPERTS)))

    x1, h2p, route, route_t, counts = _merge_route(
        x2, o_a, o_d, g_mix, w_g, w_branch_moba[0].astype(BF16), w_branch_diff[0].astype(BF16),
        w_out[0].astype(BF16), g_ffn, wr_hi, wr_lo, b_r)

    out = _moe(h2p, route, route_t, counts, x1, g_final[None, :],
               w_expert_gate[0].astype(BF16), w_expert_up[0].astype(BF16), w_expert_down[0].astype(BF16))
    return out.reshape(B, S, D)
```

```python
import functools

import jax
import jax.numpy as jnp
from jax import lax
from jax.experimental import pallas as pl
from jax.experimental.pallas import tpu as pltpu

F32 = jnp.float32
BF16 = jnp.bfloat16
I32 = jnp.int32
U32 = jnp.uint32

LANES = 128
HEAD_DIM = 64
ATT_BLK = 256
ATT_BLK_SHIFT = 8
MOBA_TOPK = 3
N_GROUPS = 4
EXPERTS_PER_GROUP = 8
N_EXPERTS = N_GROUPS * EXPERTS_PER_GROUP
EXPERT_TOPK = 2
MOE_BLK = 256
TOK_TILE = 256
MERGE_TILE = 512
EXP_GROUP = 2
QKV_TILE = 512
ROUTE_COLS = 8
EPS = 1e-6
NEG = -1e30
ROPE_THETA = 10000.0
LAMBDA_INIT = 0.8 - 0.6 * 1.0
ATT_SCALE = HEAD_DIM ** -0.5
VMEM_LIMIT = 48 * 1024 * 1024
EXPERT_VMEM_LIMIT = 56 * 1024 * 1024


def _dot(a, b):
    return jnp.dot(a, b, preferred_element_type=F32)


def _dot_nt(a, b):
    return lax.dot_general(a, b, (((1,), (1,)), ((), ())), preferred_element_type=F32)


def _rms(x, g):
    var = jnp.mean(x * x, axis=-1, keepdims=True)
    return (x * lax.rsqrt(var + EPS)) * g


def _params(sem, vmem=VMEM_LIMIT):
    return pltpu.CompilerParams(dimension_semantics=sem, vmem_limit_bytes=vmem)


def _qkv_kernel(x_ref, g_ref, w_ref, cos_ref, sin_ref, qa_ref, ka_ref, va_ref, qd_ref, kd_ref, vd_ref):
    h = _rms(x_ref[...], g_ref[...]).astype(BF16)
    cos = cos_ref[...]
    sin = sin_ref[...]
    lane = lax.broadcasted_iota(I32, cos.shape, 1)
    first = (lane & (HEAD_DIM - 1)) < HEAD_DIM // 2
    width = qa_ref.shape[1]
    outs = ((qa_ref, True, ATT_SCALE), (ka_ref, True, 1.0), (va_ref, False, 1.0),
            (qd_ref, True, ATT_SCALE), (kd_ref, True, 1.0), (vd_ref, False, 1.0))
    for j, (o_ref, rope, scale) in enumerate(outs):
        y = _dot(h, w_ref[:, j * width:(j + 1) * width])
        for c in range(width // LANES):
            yc = y[:, c * LANES:(c + 1) * LANES]
            if rope:
                rot = jnp.where(first, pltpu.roll(yc, LANES - HEAD_DIM // 2, 1), pltpu.roll(yc, HEAD_DIM // 2, 1))
                yc = yc * cos + rot * sin
            if scale != 1.0:
                yc = yc * scale
            o_ref[:, c * LANES:(c + 1) * LANES] = yc.astype(BF16)


def _qkv_proj(x2, g_mix, w_qkv, cos, sin, seq):
    T, D = x2.shape
    width = w_qkv.shape[1] // 6
    tm = min(QKV_TILE, seq)
    assert seq % tm == 0 and T % tm == 0
    spt = seq // tm
    out = jax.ShapeDtypeStruct((T, width), BF16)
    return pl.pallas_call(
        _qkv_kernel,
        grid=(T // tm,),
        in_specs=[
            pl.BlockSpec((tm, D), lambda i: (i, 0)),
            pl.BlockSpec((1, D), lambda i: (0, 0)),
            pl.BlockSpec(w_qkv.shape, lambda i: (0, 0)),
            pl.BlockSpec((tm, LANES), lambda i: (i % spt, 0)),
            pl.BlockSpec((tm, LANES), lambda i: (i % spt, 0)),
        ],
        out_specs=[pl.BlockSpec((tm, width), lambda i: (i, 0))] * 6,
        out_shape=[out] * 6,
        compiler_params=_params(("arbitrary",)),
        name="qkv_proj",
    )(x2, g_mix, w_qkv, cos, sin)


def _attend(qm, k_block, v_rows, s_ref, p_ref, cbias_ref, i):
    mx = None
    for n in range(i + 1):
        sb = _dot_nt(qm, k_block(n))
        if n == i:
            sb = sb + cbias_ref[...]
        s_ref[:, n * ATT_BLK:(n + 1) * ATT_BLK] = sb
        for c in range(ATT_BLK // LANES):
            part = sb[:, c * LANES:(c + 1) * LANES]
            mx = part if mx is None else jnp.maximum(mx, part)
    mb = jnp.broadcast_to(jnp.max(mx, axis=1, keepdims=True), (ATT_BLK, LANES))
    ps = None
    for c in range((i + 1) * ATT_BLK // LANES):
        p = jnp.exp(s_ref[:, c * LANES:(c + 1) * LANES] - mb)
        ps = p if ps is None else ps + p
        p_ref[:, c * LANES:(c + 1) * LANES] = p.astype(BF16)
    l = jnp.sum(ps, axis=1, keepdims=True)
    L = (i + 1) * ATT_BLK
    return _dot(p_ref[:, :L], v_rows(L)) * (1.0 / l)


def _causal_bias():
    row = lax.broadcasted_iota(I32, (ATT_BLK, ATT_BLK), 0)
    col = lax.broadcasted_iota(I32, (ATT_BLK, ATT_BLK), 1)
    return jnp.where(col <= row, 0.0, NEG).astype(F32)


def _moba_kernel(q_ref, k_ref, v_ref, o_ref, kaug_ref, kmf_ref, kmhi_ref, kmlo_ref, cbias_ref, s_ref, p_ref):
    S = k_ref.shape[1]
    nb = S // ATT_BLK
    kf = k_ref[0].astype(F32)
    lane_s = lax.broadcasted_iota(I32, (S, LANES), 1)
    blk_s = lax.broadcasted_iota(I32, (S, LANES), 0) >> ATT_BLK_SHIFT
    kmean = jnp.mean(kf.reshape(nb, ATT_BLK, LANES), axis=1)
    lane_n = lax.broadcasted_iota(I32, (nb, LANES), 1)
    for hh in range(2):
        base = HEAD_DIM * (1 - hh)
        own_s = (lane_s < HEAD_DIM) if hh == 0 else (lane_s >= HEAD_DIM)
        own_n = (lane_n < HEAD_DIM) if hh == 0 else (lane_n >= HEAD_DIM)
        onehot = jnp.where((lane_s - base) == blk_s, 1.0, 0.0)
        kaug_ref[hh] = jnp.where(own_s, kf, onehot).astype(BF16)
        kmf_ref[...] = jnp.zeros((LANES, LANES), F32)
        kmf_ref[base:base + nb, :] = jnp.where(own_n, kmean, 0.0)
        km = kmf_ref[...]
        hi = km.astype(BF16)
        kmhi_ref[hh] = hi
        kmlo_ref[hh] = (km - hi.astype(F32)).astype(BF16)
    cbias_ref[...] = _causal_bias()

    lane = lax.broadcasted_iota(I32, (ATT_BLK, LANES), 1)

    for i in range(nb):
        r0 = i * ATT_BLK
        q = q_ref[0, r0:r0 + ATT_BLK, :].astype(F32)
        heads = []
        for hh in range(2):
            base = HEAD_DIM * (1 - hh)
            own = (lane < HEAD_DIM) if hh == 0 else (lane >= HEAD_DIM)
            qh = jnp.where(own, q, 0.0)
            if i > MOBA_TOPK:
                qhb = qh.astype(BF16)
                gate = _dot_nt(qhb, kmhi_ref[hh]) + _dot_nt(qhb, kmlo_ref[hh])
                cand = (lane >= base) & (lane < base + i)
                g = jnp.where(cand, gate, -jnp.inf)
                rank = jnp.zeros((ATT_BLK, LANES), I32)
                for r in range(1, i):
                    rank = rank + jnp.where(pltpu.roll(g, r, 1) >= g, 1, 0)
                    rank = rank + jnp.where(pltpu.roll(g, LANES - r, 1) > g, 1, 0)
                qh = jnp.where(cand & (rank >= MOBA_TOPK), NEG, qh)
            heads.append(_attend(
                qh.astype(BF16),
                lambda n, hh=hh: kaug_ref[hh, n * ATT_BLK:(n + 1) * ATT_BLK, :],
                lambda L: v_ref[0, :L, :],
                s_ref.at[hh], p_ref.at[hh], cbias_ref, i))
        o_ref[0, r0:r0 + ATT_BLK, :] = jnp.where(lane < HEAD_DIM, heads[0], heads[1]).astype(BF16)


def _moba_attention(qa, ka, va):
    B, S, W = qa.shape
    assert S % ATT_BLK == 0 and S // ATT_BLK <= HEAD_DIM // 2 and W % LANES == 0
    spec = pl.BlockSpec((1, S, LANES), lambda b, j: (b, 0, j))
    return pl.pallas_call(
        _moba_kernel,
        grid=(B, W // LANES),
        in_specs=[spec, spec, spec],
        out_specs=spec,
        out_shape=jax.ShapeDtypeStruct((B, S, W), BF16),
        scratch_shapes=[
            pltpu.VMEM((2, S, LANES), BF16),
            pltpu.VMEM((LANES, LANES), F32),
            pltpu.VMEM((2, LANES, LANES), BF16),
            pltpu.VMEM((2, LANES, LANES), BF16),
            pltpu.VMEM((ATT_BLK, ATT_BLK), F32),
            pltpu.VMEM((2, ATT_BLK, S), F32),
            pltpu.VMEM((2, ATT_BLK, S), BF16),
        ],
        compiler_params=_params(("arbitrary", "arbitrary")),
        name="moba_attention",
    )(qa, ka, va)


def _diff_kernel(lq1_ref, lk1_ref, lq2_ref, lk2_ref, g_ref, q_ref, k_ref, v_ref, o_ref, cbias_ref, s_ref, p_ref):
    S = k_ref.shape[1]
    nb = S // ATT_BLK
    lam = (jnp.exp(jnp.sum(lq1_ref[...] * lk1_ref[...], axis=1, keepdims=True))
           - jnp.exp(jnp.sum(lq2_ref[...] * lk2_ref[...], axis=1, keepdims=True)) + LAMBDA_INIT)
    cbias_ref[...] = _causal_bias()
    lane = lax.broadcasted_iota(I32, (ATT_BLK, LANES), 1)

    for i in range(nb):
        r0 = i * ATT_BLK
        q = q_ref[0, r0:r0 + ATT_BLK, :].astype(F32)
        maps = []
        for mp in range(2):
            own = (lane < HEAD_DIM) if mp == 0 else (lane >= HEAD_DIM)
            maps.append(_attend(
                jnp.where(own, q, 0.0).astype(BF16),
                lambda n: k_ref[0, n * ATT_BLK:(n + 1) * ATT_BLK, :],
                lambda L: v_ref[0, :L, :],
                s_ref.at[mp], p_ref.at[mp], cbias_ref, i))
        o = maps[0] - lam * maps[1]
        o_ref[0, r0:r0 + ATT_BLK, :] = (_rms(o, g_ref[...]) * (1.0 - LAMBDA_INIT)).astype(BF16)


def _diff_attention(qd, kd, vd, lq1, lk1, lq2, lk2, subln_g):
    B, S, W = qd.shape
    assert S % ATT_BLK == 0 and W % LANES == 0
    spec = pl.BlockSpec((1, S, LANES), lambda b, j: (b, 0, j))
    small = pl.BlockSpec((1, HEAD_DIM), lambda b, j: (0, 0))
    return pl.pallas_call(
        _diff_kernel,
        grid=(B, W // LANES),
        in_specs=[small, small, small, small, pl.BlockSpec((1, LANES), lambda b, j: (0, 0)), spec, spec, spec],
        out_specs=spec,
        out_shape=jax.ShapeDtypeStruct((B, S, W), BF16),
        scratch_shapes=[pltpu.VMEM((ATT_BLK, ATT_BLK), F32), pltpu.VMEM((2, ATT_BLK, S), F32),
                        pltpu.VMEM((2, ATT_BLK, S), BF16)],
        compiler_params=_params(("arbitrary", "arbitrary")),
        name="diff_attention",
    )(lq1, lk1, lq2, lk2, subln_g, qd, kd, vd)


def _pack_bf16_pairs(x):
    n = x.shape[1] // 2
    lo = lax.bitcast_convert_type(x[:, :n].astype(BF16).astype(F32), U32)
    hi = lax.bitcast_convert_type(x[:, n:].astype(BF16).astype(F32), U32)
    return (lo >> 16) | hi


def _unpack_bf16_pairs(w):
    lo = lax.bitcast_convert_type(w << 16, F32)
    hi = lax.bitcast_convert_type(w & jnp.uint32(0xFFFF0000), F32)
    return jnp.concatenate([lo, hi], axis=1).astype(BF16)


def _merge_kernel(x_ref, oa_ref, od_ref, gmix_ref, wg_ref, wbm_ref, wbd_ref, wout_ref, gffn_ref,
                  wr_ref, br_ref,
                  x1_ref, h2_ref, route_ref, routet_ref, cnt_ref, base_ref, ltri_ref):
    tm, D = x_ref.shape

    @pl.when(pl.program_id(0) == 0)
    def _():
        base_ref[...] = jnp.zeros_like(base_ref)
        row = lax.broadcasted_iota(I32, (tm, tm), 0)
        col = lax.broadcasted_iota(I32, (tm, tm), 1)
        ltri_ref[...] = jnp.where(col < row, 1.0, 0.0).astype(BF16)

    x = x_ref[...]
    h = _rms(x, gmix_ref[...]).astype(BF16)
    sig = jax.nn.sigmoid(_dot(h, wg_ref[...]))
    merged = sig[:, :D] * _dot(oa_ref[...], wbm_ref[...]) + sig[:, D:] * _dot(od_ref[...], wbd_ref[...])
    x1 = x + _dot(merged.astype(BF16), wout_ref[...])
    x1_ref[...] = x1
    h2 = _rms(x1, gffn_ref[...])
    h2_ref[...] = _pack_bf16_pairs(h2)

    hhi = h2.astype(BF16)
    hlo = (h2 - hhi.astype(F32)).astype(BF16)
    lg_hi = _dot(hhi, wr_ref[...])
    lg_lo = _dot(hlo, wr_ref[...])
    lg = (lg_hi[:, :LANES] + lg_hi[:, LANES:]) + (lg_lo[:, :LANES] + lg_lo[:, LANES:]) + br_ref[...]
    lane = lax.broadcasted_iota(I32, (tm, LANES), 1)
    ninf = -jnp.inf

    def first_argmax(vals):
        m = jnp.max(vals, axis=1, keepdims=True)
        return m, jnp.min(jnp.where(vals == m, lane, LANES), axis=1, keepdims=True)

    is_group = lane < N_GROUPS
    gm, gsel = first_argmax(jnp.where(is_group, lg, ninf))
    p_group = 1.0 / jnp.sum(jnp.where(is_group, jnp.exp(lg - gm), 0.0), axis=1, keepdims=True)
    lo = N_GROUPS + gsel * EXPERTS_PER_GROUP
    el = jnp.where((lane >= lo) & (lane < lo + EXPERTS_PER_GROUP), lg, ninf)
    m1, i1 = first_argmax(el)
    m2, i2 = first_argmax(jnp.where(lane == i1, ninf, el))
    e2x = jnp.exp(m2 - m1)
    w1 = p_group / (1.0 + e2x)
    w2 = p_group * e2x / (1.0 + e2x)
    e1 = i1 - N_GROUPS
    e2 = i2 - N_GROUPS

    oh1 = jnp.where(lane == e1, 1.0, 0.0)
    oh2 = jnp.where(lane == e2, 1.0, 0.0)
    c1 = _dot(ltri_ref[...], oh1.astype(BF16))
    c2 = _dot(ltri_ref[...], oh2.astype(BF16))
    base = base_ref[...]
    tot1 = jnp.sum(oh1, axis=0, keepdims=True)
    tot2 = jnp.sum(oh2, axis=0, keepdims=True)
    r1 = jnp.sum(oh1 * (c1 + base), axis=1, keepdims=True)
    r2 = jnp.sum(oh2 * (c2 + base + tot1), axis=1, keepdims=True)
    base = base + tot1 + tot2
    base_ref[...] = base
    cnt_ref[...] = base

    cols = (e1.astype(F32), e2.astype(F32), w1, w2, r1, r2)
    route = jnp.zeros((tm, LANES), F32)
    for c, val in enumerate(cols):
        route = jnp.where(lane == c, val, route)
    route_ref[...] = route[:, :ROUTE_COLS]
    routet_ref[...] = route.T[:ROUTE_COLS, :]


def _merge_route(x2, o_a, o_d, g_mix, w_g, w_bm, w_bd, w_out, g_ffn, w_r2, b_r):
    T, D = x2.shape
    tm = min(MERGE_TILE, T)
    assert T % tm == 0
    W = o_a.shape[1]
    tile = lambda w: pl.BlockSpec((tm, w), lambda i: (i, 0))
    full = lambda a: pl.BlockSpec(a.shape, lambda i: (0, 0))
    return pl.pallas_call(
        _merge_kernel,
        grid=(T // tm,),
        in_specs=[tile(D), tile(W), tile(W), full(g_mix), full(w_g), full(w_bm), full(w_bd), full(w_out),
                  full(g_ffn), full(w_r2), full(b_r)],
        out_specs=[tile(D), tile(D // 2), tile(ROUTE_COLS), pl.BlockSpec((ROUTE_COLS, tm), lambda i: (0, i)),
                   pl.BlockSpec((1, LANES), lambda i: (0, 0))],
        out_shape=[jax.ShapeDtypeStruct((T, D), F32), jax.ShapeDtypeStruct((T, D // 2), U32),
                   jax.ShapeDtypeStruct((T, ROUTE_COLS), F32), jax.ShapeDtypeStruct((ROUTE_COLS, T), F32),
                   jax.ShapeDtypeStruct((1, LANES), F32)],
        scratch_shapes=[pltpu.VMEM((1, LANES), F32), pltpu.VMEM((tm, tm), BF16)],
        compiler_params=_params(("arbitrary",)),
        name="merge_route",
    )(x2, o_a, o_d, g_mix, w_g, w_bm, w_bd, w_out, g_ffn, w_r2, b_r)


def _index_prefetch(dest_hbm, idx_ref, isem):
    i = pl.program_id(0)
    per = dest_hbm.shape[1]
    slot = i % 2

    def copy(step, sl):
        return pltpu.make_async_copy(dest_hbm.at[step], idx_ref.at[pl.ds(sl * per, per)], isem.at[sl])

    @pl.when(i == 0)
    def _():
        copy(0, 0).start()

    @pl.when(i + 1 < pl.num_programs(0))
    def _():
        copy(i + 1, 1 - slot).start()

    copy(i, slot).wait()
    return slot * per


def _dispatch_kernel(zs_ref, nu_ref, dest_hbm, h2_ref, xd_hbm, idx_ref, zero_ref, isem, rsem, zsem):
    tm = h2_ref.shape[0] * 8
    nblk = xd_hbm.shape[0] // MOE_BLK

    @pl.when(pl.program_id(0) == 0)
    def _():
        zero_ref[...] = jnp.zeros_like(zero_ref)

        def zcopy(row):
            return pltpu.make_async_copy(zero_ref, xd_hbm.at[pl.ds(pl.multiple_of(row, MOE_BLK), MOE_BLK), :], zsem)

        def tail_start(blk, carry):
            zcopy(blk * MOE_BLK).start()
            return carry

        def tail_wait(blk, carry):
            zcopy(0).wait()
            return carry

        for e in range(N_EXPERTS):
            @pl.when(zs_ref[e] >= 0)
            def _():
                zcopy(zs_ref[e]).start()
        lax.fori_loop(nu_ref[0], nblk, tail_start, 0)
        for e in range(N_EXPERTS):
            @pl.when(zs_ref[e] >= 0)
            def _():
                zcopy(0).wait()
        lax.fori_loop(nu_ref[0], nblk, tail_wait, 0)

    base = _index_prefetch(dest_hbm, idx_ref, isem)

    def group(j, carry):
        for u in range(8):
            for kk in range(EXPERT_TOPK):
                d = idx_ref[base + kk * tm + 8 * j + u]
                pltpu.make_async_copy(h2_ref.at[j, pl.ds(u, 1), :], xd_hbm.at[pl.ds(d, 1), :],
                                      rsem).start(priority=kk)
        return carry

    lax.fori_loop(0, tm // 8, group, 0)
    for kk in range(EXPERT_TOPK):
        pltpu.make_async_copy(h2_ref, h2_ref, rsem).wait()


def _dispatch(zstart, n_used, dest2, h2p, n_rows):
    T, Wp = h2p.shape
    nt, per = dest2.shape
    tm = per // EXPERT_TOPK
    return pl.pallas_call(
        _dispatch_kernel,
        grid_spec=pltpu.PrefetchScalarGridSpec(
            num_scalar_prefetch=2,
            grid=(nt,),
            in_specs=[pl.BlockSpec(memory_space=pl.ANY),
                      pl.BlockSpec((tm // 8, 8, Wp), lambda i, zs, nu: (i, 0, 0))],
            out_specs=pl.BlockSpec(memory_space=pl.ANY),
            scratch_shapes=[pltpu.SMEM((2 * per,), I32), pltpu.VMEM((MOE_BLK, Wp), U32),
                            pltpu.SemaphoreType.DMA((2,)), pltpu.SemaphoreType.DMA(()),
                            pltpu.SemaphoreType.DMA(())],
        ),
        out_shape=jax.ShapeDtypeStruct((n_rows, Wp), U32),
        compiler_params=_params(("arbitrary",)),
        name="moe_dispatch",
    )(zstart, n_used, dest2, h2p.reshape(T // 8, 8, Wp))


def _expert_kernel(be_ref, nu_ref, *refs):
    G = EXP_GROUP
    xd_ref, w_refs, yb_ref, wb_refs = refs[0], refs[1:1 + 3 * G], refs[1 + 3 * G], refs[2 + 3 * G:]
    i = pl.program_id(0)
    n_used = nu_ref[0]

    def compute(g):
        rows = slice(g * MOE_BLK, (g + 1) * MOE_BLK)
        w1b, w3b, w2b = wb_refs[3 * g:3 * g + 3]
        x = _unpack_bf16_pairs(xd_ref[rows, :])
        act = jax.nn.silu(_dot(x, w1b[...])) * _dot(x, w3b[...])
        yb_ref[rows, :] = _dot(act.astype(BF16), w2b[...])

    for g in range(G):
        blk = i * G + g
        w1_ref, w3_ref, w2_ref = w_refs[3 * g:3 * g + 3]
        w1b, w3b, w2b = wb_refs[3 * g:3 * g + 3]

        @pl.when((blk < n_used) & ((i == 0) | (be_ref[blk] != be_ref[jnp.maximum(blk - G, 0)])))
        def _():
            w1b[...] = w1_ref[0].astype(BF16)
            w3b[...] = w3_ref[0].astype(BF16)
            w2b[...] = w2_ref[0].astype(BF16)

    @pl.when((i + 1) * G <= n_used)
    def _():
        for g in range(G):
            compute(g)

    @pl.when((i + 1) * G > n_used)
    def _():
        for g in range(G):
            @pl.when(i * G + g < n_used)
            def _():
                compute(g)

            @pl.when(i * G + g >= n_used)
            def _():
                yb_ref[g * MOE_BLK:(g + 1) * MOE_BLK, :] = jnp.zeros((MOE_BLK, yb_ref.shape[1]), F32)


def _experts(blk_expert, n_used, xd, w1, w3, w2):
    P, Wp = xd.shape
    E, D, De = w1.shape
    G = EXP_GROUP
    nblk = P // MOE_BLK
    assert nblk % G == 0
    w_specs, scratch = [], []
    for g in range(G):
        for shape in ((1, D, De), (1, D, De), (1, De, D)):
            w_specs.append(pl.BlockSpec(shape, lambda i, be, nu, g=g: (be[i * G + g], 0, 0)))
            scratch.append(pltpu.VMEM(shape[1:], BF16))
    return pl.pallas_call(
        _expert_kernel,
        grid_spec=pltpu.PrefetchScalarGridSpec(
            num_scalar_prefetch=2,
            grid=(nblk // G,),
            in_specs=[pl.BlockSpec((G * MOE_BLK, Wp), lambda i, be, nu: (jnp.minimum(i, (nu[0] - 1) // G), 0))]
                     + w_specs,
            out_specs=pl.BlockSpec((G * MOE_BLK, D), lambda i, be, nu: (i, 0)),
            scratch_shapes=scratch,
        ),
        out_shape=jax.ShapeDtypeStruct((P, D), F32),
        compiler_params=_params(("arbitrary",), EXPERT_VMEM_LIMIT),
        name="moe_experts",
    )(blk_expert, n_used, xd, *([w1, w3, w2] * G))


def _combine_kernel(dest_hbm, route_ref, x1_ref, g_ref, yb_hbm, o_ref, idx_ref, buf_ref, isem, rsem):
    tm, D = x1_ref.shape
    base = _index_prefetch(dest_hbm, idx_ref, isem)

    def group(j, carry):
        for u in range(8):
            for kk in range(EXPERT_TOPK):
                d = idx_ref[base + kk * tm + 8 * j + u]
                pltpu.make_async_copy(yb_hbm.at[pl.ds(d, 1), :], buf_ref.at[kk, j, pl.ds(u, 1), :],
                                      rsem).start(priority=kk)
        return carry

    lax.fori_loop(0, tm // 8, group, 0)
    for kk in range(EXPERT_TOPK):
        pltpu.make_async_copy(buf_ref.at[kk], buf_ref.at[kk], rsem).wait()
    route = route_ref[...]
    y = route[:, 2:3] * buf_ref[0].reshape(tm, D) + route[:, 3:4] * buf_ref[1].reshape(tm, D)
    o_ref[...] = _rms(x1_ref[...] + y, g_ref[...])


def _combine(dest2, route, x1, g_final, yb):
    T, D = x1.shape
    nt, per = dest2.shape
    tm = per // EXPERT_TOPK
    return pl.pallas_call(
        _combine_kernel,
        grid=(nt,),
        in_specs=[pl.BlockSpec(memory_space=pl.ANY), pl.BlockSpec((tm, ROUTE_COLS), lambda i: (i, 0)),
                  pl.BlockSpec((tm, D), lambda i: (i, 0)), pl.BlockSpec((1, D), lambda i: (0, 0)),
                  pl.BlockSpec(memory_space=pl.ANY)],
        out_specs=pl.BlockSpec((tm, D), lambda i: (i, 0)),
        out_shape=jax.ShapeDtypeStruct((T, D), F32),
        scratch_shapes=[pltpu.SMEM((2 * per,), I32), pltpu.VMEM((EXPERT_TOPK, tm // 8, 8, D), F32),
                        pltpu.SemaphoreType.DMA((2,)), pltpu.SemaphoreType.DMA(())],
        compiler_params=_params(("arbitrary",)),
        name="moe_combine",
    )(dest2, route, x1, g_final, yb)


def _moe(h2p, route, route_t, counts, x1, g_final, w1, w3, w2):
    T = h2p.shape[0]
    tm = min(TOK_TILE, T)
    nt = T // tm
    A = T * EXPERT_TOPK
    P = -(-A // MOE_BLK) * MOE_BLK + N_EXPERTS * MOE_BLK
    nblk = P // MOE_BLK
    cnt = counts[0, :N_EXPERTS].astype(I32)
    padded = (cnt + MOE_BLK - 1) // MOE_BLK * MOE_BLK
    pends = jnp.cumsum(padded)
    pstarts = pends - padded
    ids = route_t[0:EXPERT_TOPK].astype(I32)
    ranks = route_t[4:4 + EXPERT_TOPK].astype(I32)
    dest = ranks
    for e in range(N_EXPERTS):
        dest = dest + jnp.where(ids == e, pstarts[e], 0)
    dest2 = dest.reshape(EXPERT_TOPK, nt, tm).transpose(1, 0, 2).reshape(nt, EXPERT_TOPK * tm)
    blk_row = jnp.arange(nblk, dtype=I32) * MOE_BLK
    blk_expert = jnp.minimum(jnp.sum((pends[None, :] <= blk_row[:, None]).astype(I32), axis=1), N_EXPERTS - 1)
    n_used = (pends[-1:] // MOE_BLK).astype(I32)
    zstart = jnp.where(padded > 0, pends - MOE_BLK, -1).astype(I32)
    xd = _dispatch(zstart, n_used, dest2, h2p, P)
    yb = _experts(blk_expert, n_used, xd, w1, w3, w2)
    return _combine(dest2, route, x1, g_final, yb)


def _rope_tables(seq):
    inv = 1.0 / (ROPE_THETA ** (jnp.arange(0, HEAD_DIM, 2, dtype=F32) / HEAD_DIM))
    ang = jnp.arange(seq, dtype=F32)[:, None] * inv[None, :]
    ang = jnp.concatenate([ang, ang], axis=-1)
    sign = jnp.concatenate([-jnp.ones((HEAD_DIM // 2,), F32), jnp.ones((HEAD_DIM // 2,), F32)])
    reps = LANES // HEAD_DIM
    return jnp.tile(jnp.cos(ang), (1, reps)), jnp.tile(jnp.sin(ang) * sign, (1, reps))


def kernel(x, g_mix, w_in, w_branch_moba, w_branch_diff, w_out, diff_lambda_q1, diff_lambda_k1, diff_lambda_q2, diff_lambda_k2, diff_subln_g, g_ffn, w_group, b_group, w_router, b_router, w_expert_gate, w_expert_up, w_expert_down, g_final):
    B, S, D = x.shape
    assert w_in.shape[0] == 1, "one layer"
    T = B * S
    x2 = x.reshape(T, D)
    n_qkv = w_in.shape[2] - 2 * D
    w_qkv = w_in[0, :, :n_qkv].astype(BF16)
    w_g = w_in[0, :, n_qkv:].astype(BF16)
    cos, sin = _rope_tables(S)

    qa, ka, va, qd, kd, vd = _qkv_proj(x2, g_mix, w_qkv, cos, sin, S)
    W = qa.shape[1]
    o_a = _moba_attention(qa.reshape(B, S, W), ka.reshape(B, S, W), va.reshape(B, S, W)).reshape(T, W)
    o_d = _diff_attention(qd.reshape(B, S, W), kd.reshape(B, S, W), vd.reshape(B, S, W),
                          diff_lambda_q1, diff_lambda_k1, diff_lambda_q2, diff_lambda_k2, diff_subln_g).reshape(T, W)

    w_r = jnp.concatenate([w_group[0], w_router[0]], axis=1)
    w_r = jnp.pad(w_r, ((0, 0), (0, LANES - w_r.shape[1])))
    wr_hi = w_r.astype(BF16)
    w_r2 = jnp.concatenate([wr_hi, (w_r - wr_hi.astype(F32)).astype(BF16)], axis=1)
    b_r = jnp.pad(jnp.concatenate([b_group[0], b_router[0]])[None, :], ((0, 0), (0, LANES - N_GROUPS - N_EXPERTS)))

    x1, h2p, route, route_t, counts = _merge_route(
        x2, o_a, o_d, g_mix, w_g, w_branch_moba[0].astype(BF16), w_branch_diff[0].astype(BF16),
        w_out[0].astype(BF16), g_ffn, w_r2, b_r)

    out = _moe(h2p, route, route_t, counts, x1, g_final[None, :],
               w_expert_gate[0], w_expert_up[0], w_expert_down[0])
    return out.reshape(B, S, D)
```

```python
import functools

import jax
import jax.numpy as jnp
from jax import lax
from jax.experimental import pallas as pl
from jax.experimental.pallas import tpu as pltpu

F32 = jnp.float32
BF16 = jnp.bfloat16
I32 = jnp.int32
U32 = jnp.uint32

LANES = 128
HEAD_DIM = 64
ATT_BLK = 256
ATT_BLK_SHIFT = 8
MOBA_TOPK = 3
N_GROUPS = 4
EXPERTS_PER_GROUP = 8
N_EXPERTS = N_GROUPS * EXPERTS_PER_GROUP
EXPERT_TOPK = 2
MOE_BLK = 512
TOK_TILE = 256
MERGE_TILE = 512
QKV_TILE = 512
ROUTE_COLS = 8
EPS = 1e-6
NEG = -1e30
ROPE_THETA = 10000.0
LAMBDA_INIT = 0.8 - 0.6 * 1.0
ATT_SCALE = HEAD_DIM ** -0.5
VMEM_LIMIT = 48 * 1024 * 1024
EXPERT_VMEM_LIMIT = 56 * 1024 * 1024


def _dot(a, b):
    return jnp.dot(a, b, preferred_element_type=F32)


def _dot_nt(a, b):
    return lax.dot_general(a, b, (((1,), (1,)), ((), ())), preferred_element_type=F32)


def _rms(x, g):
    var = jnp.mean(x * x, axis=-1, keepdims=True)
    return (x * lax.rsqrt(var + EPS)) * g


def _params(sem, vmem=VMEM_LIMIT):
    return pltpu.CompilerParams(dimension_semantics=sem, vmem_limit_bytes=vmem)


def _qkv_kernel(x_ref, g_ref, w_ref, cos_ref, sin_ref, qa_ref, ka_ref, va_ref, qd_ref, kd_ref, vd_ref):
    h = _rms(x_ref[...], g_ref[...]).astype(BF16)
    cos = cos_ref[...]
    sin = sin_ref[...]
    lane = lax.broadcasted_iota(I32, cos.shape, 1)
    first = (lane & (HEAD_DIM - 1)) < HEAD_DIM // 2
    width = qa_ref.shape[1]
    outs = ((qa_ref, True, ATT_SCALE), (ka_ref, True, 1.0), (va_ref, False, 1.0),
            (qd_ref, True, ATT_SCALE), (kd_ref, True, 1.0), (vd_ref, False, 1.0))
    for j, (o_ref, rope, scale) in enumerate(outs):
        y = _dot(h, w_ref[:, j * width:(j + 1) * width])
        for c in range(width // LANES):
            yc = y[:, c * LANES:(c + 1) * LANES]
            if rope:
                rot = jnp.where(first, pltpu.roll(yc, LANES - HEAD_DIM // 2, 1), pltpu.roll(yc, HEAD_DIM // 2, 1))
                yc = yc * cos + rot * sin
            if scale != 1.0:
                yc = yc * scale
            o_ref[:, c * LANES:(c + 1) * LANES] = yc.astype(BF16)


def _qkv_proj(x2, g_mix, w_qkv, cos, sin, seq):
    T, D = x2.shape
    width = w_qkv.shape[1] // 6
    tm = min(QKV_TILE, seq)
    assert seq % tm == 0 and T % tm == 0
    spt = seq // tm
    out = jax.ShapeDtypeStruct((T, width), BF16)
    return pl.pallas_call(
        _qkv_kernel,
        grid=(T // tm,),
        in_specs=[
            pl.BlockSpec((tm, D), lambda i: (i, 0)),
            pl.BlockSpec((1, D), lambda i: (0, 0)),
            pl.BlockSpec(w_qkv.shape, lambda i: (0, 0)),
            pl.BlockSpec((tm, LANES), lambda i: (i % spt, 0)),
            pl.BlockSpec((tm, LANES), lambda i: (i % spt, 0)),
        ],
        out_specs=[pl.BlockSpec((tm, width), lambda i: (i, 0))] * 6,
        out_shape=[out] * 6,
        compiler_params=_params(("arbitrary",)),
        name="qkv_proj",
    )(x2, g_mix, w_qkv, cos, sin)


def _attend(qm, k_block, v_rows, s_ref, p_ref, cbias_ref, i):
    mx = None
    for n in range(i + 1):
        sb = _dot_nt(qm, k_block(n))
        if n == i:
            sb = sb + cbias_ref[...]
        s_ref[:, n * ATT_BLK:(n + 1) * ATT_BLK] = sb
        for c in range(ATT_BLK // LANES):
            part = sb[:, c * LANES:(c + 1) * LANES]
            mx = part if mx is None else jnp.maximum(mx, part)
    mb = jnp.broadcast_to(jnp.max(mx, axis=1, keepdims=True), (ATT_BLK, LANES))
    ps = None
    for c in range((i + 1) * ATT_BLK // LANES):
        p = jnp.exp(s_ref[:, c * LANES:(c + 1) * LANES] - mb)
        ps = p if ps is None else ps + p
        p_ref[:, c * LANES:(c + 1) * LANES] = p.astype(BF16)
    l = jnp.sum(ps, axis=1, keepdims=True)
    L = (i + 1) * ATT_BLK
    return _dot(p_ref[:, :L], v_rows(L)) * (1.0 / l)


def _causal_bias():
    row = lax.broadcasted_iota(I32, (ATT_BLK, ATT_BLK), 0)
    col = lax.broadcasted_iota(I32, (ATT_BLK, ATT_BLK), 1)
    return jnp.where(col <= row, 0.0, NEG).astype(F32)


def _moba_kernel(q_ref, k_ref, v_ref, o_ref, kaug_ref, kmf_ref, kmhi_ref, kmlo_ref, cbias_ref, s_ref, p_ref):
    S = k_ref.shape[1]
    nb = S // ATT_BLK
    kf = k_ref[0].astype(F32)
    lane_s = lax.broadcasted_iota(I32, (S, LANES), 1)
    blk_s = lax.broadcasted_iota(I32, (S, LANES), 0) >> ATT_BLK_SHIFT
    kmean = jnp.mean(kf.reshape(nb, ATT_BLK, LANES), axis=1)
    lane_n = lax.broadcasted_iota(I32, (nb, LANES), 1)
    for hh in range(2):
        base = HEAD_DIM * (1 - hh)
        own_s = (lane_s < HEAD_DIM) if hh == 0 else (lane_s >= HEAD_DIM)
        own_n = (lane_n < HEAD_DIM) if hh == 0 else (lane_n >= HEAD_DIM)
        onehot = jnp.where((lane_s - base) == blk_s, 1.0, 0.0)
        kaug_ref[hh] = jnp.where(own_s, kf, onehot).astype(BF16)
        kmf_ref[...] = jnp.zeros((LANES, LANES), F32)
        kmf_ref[base:base + nb, :] = jnp.where(own_n, kmean, 0.0)
        km = kmf_ref[...]
        hi = km.astype(BF16)
        kmhi_ref[hh] = hi
        kmlo_ref[hh] = (km - hi.astype(F32)).astype(BF16)
    cbias_ref[...] = _causal_bias()

    lane = lax.broadcasted_iota(I32, (ATT_BLK, LANES), 1)

    for i in range(nb):
        r0 = i * ATT_BLK
        q = q_ref[0, r0:r0 + ATT_BLK, :].astype(F32)
        heads = []
        for hh in range(2):
            base = HEAD_DIM * (1 - hh)
            own = (lane < HEAD_DIM) if hh == 0 else (lane >= HEAD_DIM)
            qh = jnp.where(own, q, 0.0)
            if i > MOBA_TOPK:
                qhb = qh.astype(BF16)
                gate = _dot_nt(qhb, kmhi_ref[hh]) + _dot_nt(qhb, kmlo_ref[hh])
                cand = (lane >= base) & (lane < base + i)
                g = jnp.where(cand, gate, -jnp.inf)
                rank = jnp.zeros((ATT_BLK, LANES), I32)
                for r in range(1, i):
                    rank = rank + jnp.where(pltpu.roll(g, r, 1) >= g, 1, 0)
                    rank = rank + jnp.where(pltpu.roll(g, LANES - r, 1) > g, 1, 0)
                qh = jnp.where(cand & (rank >= MOBA_TOPK), NEG, qh)
            heads.append(_attend(
                qh.astype(BF16),
                lambda n, hh=hh: kaug_ref[hh, n * ATT_BLK:(n + 1) * ATT_BLK, :],
                lambda L: v_ref[0, :L, :],
                s_ref.at[hh], p_ref.at[hh], cbias_ref, i))
        o_ref[0, r0:r0 + ATT_BLK, :] = jnp.where(lane < HEAD_DIM, heads[0], heads[1]).astype(BF16)


def _moba_attention(qa, ka, va):
    B, S, W = qa.shape
    assert S % ATT_BLK == 0 and S // ATT_BLK <= HEAD_DIM // 2 and W % LANES == 0
    spec = pl.BlockSpec((1, S, LANES), lambda b, j: (b, 0, j))
    return pl.pallas_call(
        _moba_kernel,
        grid=(B, W // LANES),
        in_specs=[spec, spec, spec],
        out_specs=spec,
        out_shape=jax.ShapeDtypeStruct((B, S, W), BF16),
        scratch_shapes=[
            pltpu.VMEM((2, S, LANES), BF16),
            pltpu.VMEM((LANES, LANES), F32),
            pltpu.VMEM((2, LANES, LANES), BF16),
            pltpu.VMEM((2, LANES, LANES), BF16),
            pltpu.VMEM((ATT_BLK, ATT_BLK), F32),
            pltpu.VMEM((2, ATT_BLK, S), F32),
            pltpu.VMEM((2, ATT_BLK, S), BF16),
        ],
        compiler_params=_params(("arbitrary", "arbitrary")),
        name="moba_attention",
    )(qa, ka, va)


def _diff_kernel(lq1_ref, lk1_ref, lq2_ref, lk2_ref, g_ref, q_ref, k_ref, v_ref, o_ref, cbias_ref, s_ref, p_ref):
    S = k_ref.shape[1]
    nb = S // ATT_BLK
    lam = (jnp.exp(jnp.sum(lq1_ref[...] * lk1_ref[...], axis=1, keepdims=True))
           - jnp.exp(jnp.sum(lq2_ref[...] * lk2_ref[...], axis=1, keepdims=True)) + LAMBDA_INIT)
    cbias_ref[...] = _causal_bias()
    lane = lax.broadcasted_iota(I32, (ATT_BLK, LANES), 1)

    for i in range(nb):
        r0 = i * ATT_BLK
        q = q_ref[0, r0:r0 + ATT_BLK, :].astype(F32)
        maps = []
        for mp in range(2):
            own = (lane < HEAD_DIM) if mp == 0 else (lane >= HEAD_DIM)
            maps.append(_attend(
                jnp.where(own, q, 0.0).astype(BF16),
                lambda n: k_ref[0, n * ATT_BLK:(n + 1) * ATT_BLK, :],
                lambda L: v_ref[0, :L, :],
                s_ref.at[mp], p_ref.at[mp], cbias_ref, i))
        o = maps[0] - lam * maps[1]
        o_ref[0, r0:r0 + ATT_BLK, :] = (_rms(o, g_ref[...]) * (1.0 - LAMBDA_INIT)).astype(BF16)


def _diff_attention(qd, kd, vd, lq1, lk1, lq2, lk2, subln_g):
    B, S, W = qd.shape
    assert S % ATT_BLK == 0 and W % LANES == 0
    spec = pl.BlockSpec((1, S, LANES), lambda b, j: (b, 0, j))
    small = pl.BlockSpec((1, HEAD_DIM), lambda b, j: (0, 0))
    return pl.pallas_call(
        _diff_kernel,
        grid=(B, W // LANES),
        in_specs=[small, small, small, small, pl.BlockSpec((1, LANES), lambda b, j: (0, 0)), spec, spec, spec],
        out_specs=spec,
        out_shape=jax.ShapeDtypeStruct((B, S, W), BF16),
        scratch_shapes=[pltpu.VMEM((ATT_BLK, ATT_BLK), F32), pltpu.VMEM((2, ATT_BLK, S), F32),
                        pltpu.VMEM((2, ATT_BLK, S), BF16)],
        compiler_params=_params(("arbitrary", "arbitrary")),
        name="diff_attention",
    )(lq1, lk1, lq2, lk2, subln_g, qd, kd, vd)


def _pack_bf16_pairs(x):
    n = x.shape[1] // 2
    lo = lax.bitcast_convert_type(x[:, :n].astype(BF16).astype(F32), U32)
    hi = lax.bitcast_convert_type(x[:, n:].astype(BF16).astype(F32), U32)
    return (lo >> 16) | hi


def _unpack_bf16_pairs(w, dtype=BF16):
    lo = lax.bitcast_convert_type(w << 16, F32)
    hi = lax.bitcast_convert_type(w & jnp.uint32(0xFFFF0000), F32)
    return jnp.concatenate([lo, hi], axis=1).astype(dtype)


def _merge_kernel(x_ref, oa_ref, od_ref, gmix_ref, wg_ref, wbm_ref, wbd_ref, wout_ref, gffn_ref,
                  wr_ref, br_ref,
                  x1_ref, h2_ref, route_ref, routet_ref, cnt_ref, base_ref, ltri_ref):
    tm, D = x_ref.shape

    @pl.when(pl.program_id(0) == 0)
    def _():
        base_ref[...] = jnp.zeros_like(base_ref)
        row = lax.broadcasted_iota(I32, (tm, tm), 0)
        col = lax.broadcasted_iota(I32, (tm, tm), 1)
        ltri_ref[...] = jnp.where(col < row, 1.0, 0.0).astype(BF16)

    x = x_ref[...]
    h = _rms(x, gmix_ref[...]).astype(BF16)
    sig = jax.nn.sigmoid(_dot(h, wg_ref[...]))
    merged = sig[:, :D] * _dot(oa_ref[...], wbm_ref[...]) + sig[:, D:] * _dot(od_ref[...], wbd_ref[...])
    x1 = x + _dot(merged.astype(BF16), wout_ref[...])
    x1_ref[...] = x1
    h2 = _rms(x1, gffn_ref[...])
    h2_ref[...] = _pack_bf16_pairs(h2)

    hhi = h2.astype(BF16)
    hlo = (h2 - hhi.astype(F32)).astype(BF16)
    lg_hi = _dot(hhi, wr_ref[...])
    lg_lo = _dot(hlo, wr_ref[...])
    lg = (lg_hi[:, :LANES] + lg_hi[:, LANES:]) + (lg_lo[:, :LANES] + lg_lo[:, LANES:]) + br_ref[...]
    lane = lax.broadcasted_iota(I32, (tm, LANES), 1)
    ninf = -jnp.inf

    def first_argmax(vals):
        m = jnp.max(vals, axis=1, keepdims=True)
        return m, jnp.min(jnp.where(vals == m, lane, LANES), axis=1, keepdims=True)

    is_group = lane < N_GROUPS
    gm, gsel = first_argmax(jnp.where(is_group, lg, ninf))
    p_group = 1.0 / jnp.sum(jnp.where(is_group, jnp.exp(lg - gm), 0.0), axis=1, keepdims=True)
    lo = N_GROUPS + gsel * EXPERTS_PER_GROUP
    el = jnp.where((lane >= lo) & (lane < lo + EXPERTS_PER_GROUP), lg, ninf)
    m1, i1 = first_argmax(el)
    m2, i2 = first_argmax(jnp.where(lane == i1, ninf, el))
    e2x = jnp.exp(m2 - m1)
    w1 = p_group / (1.0 + e2x)
    w2 = p_group * e2x / (1.0 + e2x)
    e1 = i1 - N_GROUPS
    e2 = i2 - N_GROUPS

    oh1 = jnp.where(lane == e1, 1.0, 0.0)
    oh2 = jnp.where(lane == e2, 1.0, 0.0)
    c1 = _dot(ltri_ref[...], oh1.astype(BF16))
    c2 = _dot(ltri_ref[...], oh2.astype(BF16))
    base = base_ref[...]
    tot1 = jnp.sum(oh1, axis=0, keepdims=True)
    tot2 = jnp.sum(oh2, axis=0, keepdims=True)
    r1 = jnp.sum(oh1 * (c1 + base), axis=1, keepdims=True)
    r2 = jnp.sum(oh2 * (c2 + base + tot1), axis=1, keepdims=True)
    base = base + tot1 + tot2
    base_ref[...] = base
    cnt_ref[...] = base

    cols = (e1.astype(F32), e2.astype(F32), w1, w2, r1, r2)
    route = jnp.zeros((tm, LANES), F32)
    for c, val in enumerate(cols):
        route = jnp.where(lane == c, val, route)
    route_ref[...] = route[:, :ROUTE_COLS]
    routet_ref[...] = route.T[:ROUTE_COLS, :]


def _merge_route(x2, o_a, o_d, g_mix, w_g, w_bm, w_bd, w_out, g_ffn, w_r2, b_r):
    T, D = x2.shape
    tm = min(MERGE_TILE, T)
    assert T % tm == 0
    W = o_a.shape[1]
    tile = lambda w: pl.BlockSpec((tm, w), lambda i: (i, 0))
    full = lambda a: pl.BlockSpec(a.shape, lambda i: (0, 0))
    return pl.pallas_call(
        _merge_kernel,
        grid=(T // tm,),
        in_specs=[tile(D), tile(W), tile(W), full(g_mix), full(w_g), full(w_bm), full(w_bd), full(w_out),
                  full(g_ffn), full(w_r2), full(b_r)],
        out_specs=[tile(D), tile(D // 2), tile(ROUTE_COLS), pl.BlockSpec((ROUTE_COLS, tm), lambda i: (0, i)),
                   pl.BlockSpec((1, LANES), lambda i: (0, 0))],
        out_shape=[jax.ShapeDtypeStruct((T, D), F32), jax.ShapeDtypeStruct((T, D // 2), U32),
                   jax.ShapeDtypeStruct((T, ROUTE_COLS), F32), jax.ShapeDtypeStruct((ROUTE_COLS, T), F32),
                   jax.ShapeDtypeStruct((1, LANES), F32)],
        scratch_shapes=[pltpu.VMEM((1, LANES), F32), pltpu.VMEM((tm, tm), BF16)],
        compiler_params=_params(("arbitrary",)),
        name="merge_route",
    )(x2, o_a, o_d, g_mix, w_g, w_bm, w_bd, w_out, g_ffn, w_r2, b_r)


def _index_prefetch(dest_hbm, idx_ref, isem):
    i = pl.program_id(0)
    per = dest_hbm.shape[1]
    slot = i % 2

    def copy(step, sl):
        return pltpu.make_async_copy(dest_hbm.at[step], idx_ref.at[pl.ds(sl * per, per)], isem.at[sl])

    @pl.when(i == 0)
    def _():
        copy(0, 0).start()

    @pl.when(i + 1 < pl.num_programs(0))
    def _():
        copy(i + 1, 1 - slot).start()

    copy(i, slot).wait()
    return slot * per


def _dispatch_kernel(zs_ref, nu_ref, dest_hbm, h2_ref, xd_hbm, idx_ref, zero_ref, isem, rsem, zsem):
    tm = h2_ref.shape[0] * 8
    nblk = xd_hbm.shape[0] // MOE_BLK

    @pl.when(pl.program_id(0) == 0)
    def _():
        zero_ref[...] = jnp.zeros_like(zero_ref)

        def zcopy(row):
            return pltpu.make_async_copy(zero_ref, xd_hbm.at[pl.ds(pl.multiple_of(row, MOE_BLK), MOE_BLK), :], zsem)

        def tail_start(blk, carry):
            zcopy(blk * MOE_BLK).start()
            return carry

        def tail_wait(blk, carry):
            zcopy(0).wait()
            return carry

        for e in range(N_EXPERTS):
            @pl.when(zs_ref[e] >= 0)
            def _():
                zcopy(zs_ref[e]).start()
        lax.fori_loop(nu_ref[0], nblk, tail_start, 0)
        for e in range(N_EXPERTS):
            @pl.when(zs_ref[e] >= 0)
            def _():
                zcopy(0).wait()
        lax.fori_loop(nu_ref[0], nblk, tail_wait, 0)

    base = _index_prefetch(dest_hbm, idx_ref, isem)

    def group(j, carry):
        for u in range(8):
            for kk in range(EXPERT_TOPK):
                d = idx_ref[base + kk * tm + 8 * j + u]
                pltpu.make_async_copy(h2_ref.at[j, pl.ds(u, 1), :], xd_hbm.at[pl.ds(d, 1), :],
                                      rsem).start(priority=kk)
        return carry

    lax.fori_loop(0, tm // 8, group, 0)
    for kk in range(EXPERT_TOPK):
        pltpu.make_async_copy(h2_ref, h2_ref, rsem).wait()


def _dispatch(zstart, n_used, dest2, h2p, n_rows):
    T, Wp = h2p.shape
    nt, per = dest2.shape
    tm = per // EXPERT_TOPK
    return pl.pallas_call(
        _dispatch_kernel,
        grid_spec=pltpu.PrefetchScalarGridSpec(
            num_scalar_prefetch=2,
            grid=(nt,),
            in_specs=[pl.BlockSpec(memory_space=pl.ANY),
                      pl.BlockSpec((tm // 8, 8, Wp), lambda i, zs, nu: (i, 0, 0))],
            out_specs=pl.BlockSpec(memory_space=pl.ANY),
            scratch_shapes=[pltpu.SMEM((2 * per,), I32), pltpu.VMEM((MOE_BLK, Wp), U32),
                            pltpu.SemaphoreType.DMA((2,)), pltpu.SemaphoreType.DMA(()),
                            pltpu.SemaphoreType.DMA(())],
        ),
        out_shape=jax.ShapeDtypeStruct((n_rows, Wp), U32),
        compiler_params=_params(("arbitrary",)),
        name="moe_dispatch",
    )(zstart, n_used, dest2, h2p.reshape(T // 8, 8, Wp))


def _expert_kernel(be_ref, nu_ref, xd_ref, w1_ref, w3_ref, w2_ref, yb_ref, w1b, w3b, w2b):
    i = pl.program_id(0)
    n_used = nu_ref[0]

    @pl.when((i < n_used) & ((i == 0) | (be_ref[i] != be_ref[jnp.maximum(i - 1, 0)])))
    def _():
        w1b[...] = w1_ref[0].astype(BF16)
        w3b[...] = w3_ref[0].astype(BF16)
        w2b[...] = w2_ref[0].astype(BF16)

    @pl.when(i < n_used)
    def _():
        x = _unpack_bf16_pairs(xd_ref[...])
        act = jax.nn.silu(_dot(x, w1b[...])) * _dot(x, w3b[...])
        yb_ref[...] = _pack_bf16_pairs(_dot(act.astype(BF16), w2b[...]))

    @pl.when(i >= n_used)
    def _():
        yb_ref[...] = jnp.zeros_like(yb_ref)


def _experts(blk_expert, n_used, xd, w1, w3, w2):
    P, Wp = xd.shape
    E, D, De = w1.shape
    nblk = P // MOE_BLK
    return pl.pallas_call(
        _expert_kernel,
        grid_spec=pltpu.PrefetchScalarGridSpec(
            num_scalar_prefetch=2,
            grid=(nblk,),
            in_specs=[
                pl.BlockSpec((MOE_BLK, Wp), lambda i, be, nu: (jnp.minimum(i, nu[0] - 1), 0)),
                pl.BlockSpec((1, D, De), lambda i, be, nu: (be[i], 0, 0)),
                pl.BlockSpec((1, D, De), lambda i, be, nu: (be[i], 0, 0)),
                pl.BlockSpec((1, De, D), lambda i, be, nu: (be[i], 0, 0)),
            ],
            out_specs=pl.BlockSpec((MOE_BLK, D // 2), lambda i, be, nu: (i, 0)),
            scratch_shapes=[pltpu.VMEM((D, De), BF16), pltpu.VMEM((D, De), BF16), pltpu.VMEM((De, D), BF16)],
        ),
        out_shape=jax.ShapeDtypeStruct((P, D // 2), U32),
        compiler_params=_params(("arbitrary",), EXPERT_VMEM_LIMIT),
        name="moe_experts",
    )(blk_expert, n_used, xd, w1, w3, w2)


def _combine_kernel(dest_hbm, route_ref, x1_ref, g_ref, yb_hbm, o_ref, idx_ref, buf_ref, isem, rsem):
    tm, D = x1_ref.shape
    base = _index_prefetch(dest_hbm, idx_ref, isem)

    def group(j, carry):
        for u in range(8):
            for kk in range(EXPERT_TOPK):
                d = idx_ref[base + kk * tm + 8 * j + u]
                pltpu.make_async_copy(yb_hbm.at[pl.ds(d, 1), :], buf_ref.at[kk, j, pl.ds(u, 1), :],
                                      rsem).start(priority=kk)
        return carry

    lax.fori_loop(0, tm // 8, group, 0)
    for kk in range(EXPERT_TOPK):
        pltpu.make_async_copy(buf_ref.at[kk], buf_ref.at[kk], rsem).wait()
    route = route_ref[...]
    y0 = _unpack_bf16_pairs(buf_ref[0].reshape(tm, D // 2), F32)
    y1 = _unpack_bf16_pairs(buf_ref[1].reshape(tm, D // 2), F32)
    o_ref[...] = _rms(x1_ref[...] + (route[:, 2:3] * y0 + route[:, 3:4] * y1), g_ref[...])


def _combine(dest2, route, x1, g_final, yb):
    T, D = x1.shape
    nt, per = dest2.shape
    tm = per // EXPERT_TOPK
    return pl.pallas_call(
        _combine_kernel,
        grid=(nt,),
        in_specs=[pl.BlockSpec(memory_space=pl.ANY), pl.BlockSpec((tm, ROUTE_COLS), lambda i: (i, 0)),
                  pl.BlockSpec((tm, D), lambda i: (i, 0)), pl.BlockSpec((1, D), lambda i: (0, 0)),
                  pl.BlockSpec(memory_space=pl.ANY)],
        out_specs=pl.BlockSpec((tm, D), lambda i: (i, 0)),
        out_shape=jax.ShapeDtypeStruct((T, D), F32),
        scratch_shapes=[pltpu.SMEM((2 * per,), I32), pltpu.VMEM((EXPERT_TOPK, tm // 8, 8, D // 2), U32),
                        pltpu.SemaphoreType.DMA((2,)), pltpu.SemaphoreType.DMA(())],
        compiler_params=_params(("arbitrary",)),
        name="moe_combine",
    )(dest2, route, x1, g_final, yb)


def _moe(h2p, route, route_t, counts, x1, g_final, w1, w3, w2):
    T = h2p.shape[0]
    tm = min(TOK_TILE, T)
    nt = T // tm
    A = T * EXPERT_TOPK
    P = -(-A // MOE_BLK) * MOE_BLK + N_EXPERTS * MOE_BLK
    nblk = P // MOE_BLK
    cnt = counts[0, :N_EXPERTS].astype(I32)
    padded = (cnt + MOE_BLK - 1) // MOE_BLK * MOE_BLK
    pends = jnp.cumsum(padded)
    pstarts = pends - padded
    ids = route_t[0:EXPERT_TOPK].astype(I32)
    ranks = route_t[4:4 + EXPERT_TOPK].astype(I32)
    dest = ranks
    for e in range(N_EXPERTS):
        dest = dest + jnp.where(ids == e, pstarts[e], 0)
    dest2 = dest.reshape(EXPERT_TOPK, nt, tm).transpose(1, 0, 2).reshape(nt, EXPERT_TOPK * tm)
    blk_row = jnp.arange(nblk, dtype=I32) * MOE_BLK
    blk_expert = jnp.minimum(jnp.sum((pends[None, :] <= blk_row[:, None]).astype(I32), axis=1), N_EXPERTS - 1)
    n_used = (pends[-1:] // MOE_BLK).astype(I32)
    zstart = jnp.where(padded > 0, pends - MOE_BLK, -1).astype(I32)
    xd = _dispatch(zstart, n_used, dest2, h2p, P)
    yb = _experts(blk_expert, n_used, xd, w1, w3, w2)
    return _combine(dest2, route, x1, g_final, yb)


def _rope_tables(seq):
    inv = 1.0 / (ROPE_THETA ** (jnp.arange(0, HEAD_DIM, 2, dtype=F32) / HEAD_DIM))
    ang = jnp.arange(seq, dtype=F32)[:, None] * inv[None, :]
    ang = jnp.concatenate([ang, ang], axis=-1)
    sign = jnp.concatenate([-jnp.ones((HEAD_DIM // 2,), F32), jnp.ones((HEAD_DIM // 2,), F32)])
    reps = LANES // HEAD_DIM
    return jnp.tile(jnp.cos(ang), (1, reps)), jnp.tile(jnp.sin(ang) * sign, (1, reps))


def kernel(x, g_mix, w_in, w_branch_moba, w_branch_diff, w_out, diff_lambda_q1, diff_lambda_k1, diff_lambda_q2, diff_lambda_k2, diff_subln_g, g_ffn, w_group, b_group, w_router, b_router, w_expert_gate, w_expert_up, w_expert_down, g_final):
    B, S, D = x.shape
    assert w_in.shape[0] == 1, "one layer"
    T = B * S
    x2 = x.reshape(T, D)
    n_qkv = w_in.shape[2] - 2 * D
    w_qkv = w_in[0, :, :n_qkv].astype(BF16)
    w_g = w_in[0, :, n_qkv:].astype(BF16)
    cos, sin = _rope_tables(S)

    qa, ka, va, qd, kd, vd = _qkv_proj(x2, g_mix, w_qkv, cos, sin, S)
    W = qa.shape[1]
    o_a = _moba_attention(qa.reshape(B, S, W), ka.reshape(B, S, W), va.reshape(B, S, W)).reshape(T, W)
    o_d = _diff_attention(qd.reshape(B, S, W), kd.reshape(B, S, W), vd.reshape(B, S, W),
                          diff_lambda_q1, diff_lambda_k1, diff_lambda_q2, diff_lambda_k2, diff_subln_g).reshape(T, W)

    w_r = jnp.concatenate([w_group[0], w_router[0]], axis=1)
    w_r = jnp.pad(w_r, ((0, 0), (0, LANES - w_r.shape[1])))
    wr_hi = w_r.astype(BF16)
    w_r2 = jnp.concatenate([wr_hi, (w_r - wr_hi.astype(F32)).astype(BF16)], axis=1)
    b_r = jnp.pad(jnp.concatenate([b_group[0], b_router[0]])[None, :], ((0, 0), (0, LANES - N_GROUPS - N_EXPERTS)))

    x1, h2p, route, route_t, counts = _merge_route(
        x2, o_a, o_d, g_mix, w_g, w_branch_moba[0].astype(BF16), w_branch_diff[0].astype(BF16),
        w_out[0].astype(BF16), g_ffn, w_r2, b_r)

    out = _moe(h2p, route, route_t, counts, x1, g_final[None, :],
               w_expert_gate[0], w_expert_up[0], w_expert_down[0])
    return out.reshape(B, S, D)
```

```python
import functools

import jax
import jax.numpy as jnp
from jax import lax
from jax.experimental import pallas as pl
from jax.experimental.pallas import tpu as pltpu

F32 = jnp.float32
BF16 = jnp.bfloat16
I32 = jnp.int32
U32 = jnp.uint32

LANES = 128
HEAD_DIM = 64
ATT_BLK = 256
ATT_BLK_SHIFT = 8
MOBA_TOPK = 3
N_GROUPS = 4
EXPERTS_PER_GROUP = 8
N_EXPERTS = N_GROUPS * EXPERTS_PER_GROUP
EXPERT_TOPK = 2
MOE_BLK = 512
TOK_TILE = 256
MERGE_TILE = 512
QKV_TILE = 512
ROUTE_COLS = 8
EPS = 1e-6
NEG = -1e30
ROPE_THETA = 10000.0
LAMBDA_INIT = 0.8 - 0.6 * 1.0
ATT_SCALE = HEAD_DIM ** -0.5
VMEM_LIMIT = 48 * 1024 * 1024
EXPERT_VMEM_LIMIT = 56 * 1024 * 1024


def _dot(a, b):
    return jnp.dot(a, b, preferred_element_type=F32)


def _dot_nt(a, b):
    return lax.dot_general(a, b, (((1,), (1,)), ((), ())), preferred_element_type=F32)


def _rms(x, g):
    var = jnp.mean(x * x, axis=-1, keepdims=True)
    return (x * lax.rsqrt(var + EPS)) * g


def _params(sem, vmem=VMEM_LIMIT):
    return pltpu.CompilerParams(dimension_semantics=sem, vmem_limit_bytes=vmem)


def _qkv_kernel(x_ref, g_ref, w_ref, cos_ref, sin_ref, qa_ref, ka_ref, va_ref, qd_ref, kd_ref, vd_ref):
    h = _rms(x_ref[...], g_ref[...]).astype(BF16)
    cos = cos_ref[...]
    sin = sin_ref[...]
    lane = lax.broadcasted_iota(I32, cos.shape, 1)
    first = (lane & (HEAD_DIM - 1)) < HEAD_DIM // 2
    width = qa_ref.shape[1]
    outs = ((qa_ref, True, ATT_SCALE), (ka_ref, True, 1.0), (va_ref, False, 1.0),
            (qd_ref, True, ATT_SCALE), (kd_ref, True, 1.0), (vd_ref, False, 1.0))
    for j, (o_ref, rope, scale) in enumerate(outs):
        y = _dot(h, w_ref[:, j * width:(j + 1) * width])
        for c in range(width // LANES):
            yc = y[:, c * LANES:(c + 1) * LANES]
            if rope:
                rot = jnp.where(first, pltpu.roll(yc, LANES - HEAD_DIM // 2, 1), pltpu.roll(yc, HEAD_DIM // 2, 1))
                yc = yc * cos + rot * sin
            if scale != 1.0:
                yc = yc * scale
            o_ref[:, c * LANES:(c + 1) * LANES] = yc.astype(BF16)


def _qkv_proj(x2, g_mix, w_qkv, cos, sin, seq):
    T, D = x2.shape
    width = w_qkv.shape[1] // 6
    tm = min(QKV_TILE, seq)
    assert seq % tm == 0 and T % tm == 0
    spt = seq // tm
    out = jax.ShapeDtypeStruct((T, width), BF16)
    return pl.pallas_call(
        _qkv_kernel,
        grid=(T // tm,),
        in_specs=[
            pl.BlockSpec((tm, D), lambda i: (i, 0)),
            pl.BlockSpec((1, D), lambda i: (0, 0)),
            pl.BlockSpec(w_qkv.shape, lambda i: (0, 0)),
            pl.BlockSpec((tm, LANES), lambda i: (i % spt, 0)),
            pl.BlockSpec((tm, LANES), lambda i: (i % spt, 0)),
        ],
        out_specs=[pl.BlockSpec((tm, width), lambda i: (i, 0))] * 6,
        out_shape=[out] * 6,
        compiler_params=_params(("arbitrary",)),
        name="qkv_proj",
    )(x2, g_mix, w_qkv, cos, sin)


ATT_ROWS = 2 * ATT_BLK


def _attend(qs, k_block, v_rows, s_ref, p_ref, cbias_ref, i):
    mx = None
    for n in range(i + 1):
        sb = _dot_nt(qs, k_block(n))
        if n == i:
            sb = sb + cbias_ref[...]
        s_ref[:, n * ATT_BLK:(n + 1) * ATT_BLK] = sb
        for c in range(ATT_BLK // LANES):
            part = sb[:, c * LANES:(c + 1) * LANES]
            mx = part if mx is None else jnp.maximum(mx, part)
    mb = jnp.broadcast_to(jnp.max(mx, axis=1, keepdims=True), (ATT_ROWS, LANES))
    ps = None
    for c in range((i + 1) * ATT_BLK // LANES):
        p = jnp.exp(s_ref[:, c * LANES:(c + 1) * LANES] - mb)
        ps = p if ps is None else ps + p
        p_ref[:, c * LANES:(c + 1) * LANES] = p.astype(BF16)
    l = jnp.sum(ps, axis=1, keepdims=True)
    L = (i + 1) * ATT_BLK
    return _dot(p_ref[:, :L], v_rows(L)) * (1.0 / l)


def _causal_bias():
    row = lax.broadcasted_iota(I32, (ATT_ROWS, ATT_BLK), 0) & (ATT_BLK - 1)
    col = lax.broadcasted_iota(I32, (ATT_ROWS, ATT_BLK), 1)
    return jnp.where(col <= row, 0.0, NEG).astype(F32)


def _attention_scratch(seq):
    return [pltpu.VMEM((ATT_ROWS, ATT_BLK), F32), pltpu.VMEM((ATT_ROWS, seq), F32), pltpu.VMEM((ATT_ROWS, seq), BF16)]


def _moba_kernel(q_ref, k_ref, v_ref, o_ref, kaug_ref, kmf_ref, kmhi_ref, kmlo_ref, cbias_ref, s_ref, p_ref):
    S = k_ref.shape[1]
    nb = S // ATT_BLK
    k = k_ref[0]
    lane_s = lax.broadcasted_iota(I32, (S, LANES), 1)
    blk_s = lax.broadcasted_iota(I32, (S, LANES), 0) >> ATT_BLK_SHIFT
    kaug_ref[:, :LANES] = k
    kaug_ref[:, LANES:] = jnp.where(lane_s == blk_s, 1.0, 0.0).astype(BF16)
    kmean = jnp.mean(k.astype(F32).reshape(nb, ATT_BLK, LANES), axis=1)
    lane_n = lax.broadcasted_iota(I32, (nb, LANES), 1)
    for hh in range(2):
        own_n = (lane_n < HEAD_DIM) if hh == 0 else (lane_n >= HEAD_DIM)
        kmf_ref[...] = jnp.zeros((LANES, LANES), F32)
        kmf_ref[0:nb, :] = jnp.where(own_n, kmean, 0.0)
        km = kmf_ref[...]
        hi = km.astype(BF16)
        kmhi_ref[hh] = hi
        kmlo_ref[hh] = (km - hi.astype(F32)).astype(BF16)
    cbias_ref[...] = _causal_bias()

    lane = lax.broadcasted_iota(I32, (ATT_BLK, LANES), 1)

    def stacked_queries(i):
        q = q_ref[0, i * ATT_BLK:(i + 1) * ATT_BLK, :].astype(F32)
        rows = []
        for hh in range(2):
            own = (lane < HEAD_DIM) if hh == 0 else (lane >= HEAD_DIM)
            qh = jnp.where(own, q, 0.0)
            if i <= MOBA_TOPK:
                rows.append(qh)
                continue
            qhb = qh.astype(BF16)
            gate = _dot_nt(qhb, kmhi_ref[hh]) + _dot_nt(qhb, kmlo_ref[hh])
            cand = lane < i
            g = jnp.where(cand, gate, -jnp.inf)
            rank = jnp.zeros((ATT_BLK, LANES), I32)
            for r in range(1, i):
                rank = rank + jnp.where(pltpu.roll(g, r, 1) >= g, 1, 0)
                rank = rank + jnp.where(pltpu.roll(g, LANES - r, 1) > g, 1, 0)
            drop = jnp.where(cand & (rank >= MOBA_TOPK), NEG, 0.0)
            rows.append(jnp.concatenate([qh, drop], axis=1))
        return jnp.concatenate(rows, axis=0).astype(BF16)

    for i in _moba_block_order(nb):
        if i > MOBA_TOPK:
            k_block = lambda n: kaug_ref[n * ATT_BLK:(n + 1) * ATT_BLK, :]
        else:
            k_block = lambda n: k_ref[0, n * ATT_BLK:(n + 1) * ATT_BLK, :]
        out = _attend(stacked_queries(i), k_block, lambda L: v_ref[0, :L, :], s_ref, p_ref, cbias_ref, i)
        o_ref[0, i * ATT_BLK:(i + 1) * ATT_BLK, :] = jnp.where(
            lane < HEAD_DIM, out[:ATT_BLK], out[ATT_BLK:]).astype(BF16)


def _moba_block_order(nb):
    small = list(range(min(nb, MOBA_TOPK + 1)))[::-1]
    large = list(range(MOBA_TOPK + 1, nb))[::-1]
    order = []
    while small or large:
        if small:
            order.append(small.pop(0))
        if large:
            order.append(large.pop(0))
    return order


def _moba_attention(qa, ka, va):
    B, S, W = qa.shape
    assert S % ATT_BLK == 0 and S // ATT_BLK <= LANES and W % LANES == 0
    spec = pl.BlockSpec((1, S, LANES), lambda b, j: (b, 0, j))
    return pl.pallas_call(
        _moba_kernel,
        grid=(B, W // LANES),
        in_specs=[spec, spec, spec],
        out_specs=spec,
        out_shape=jax.ShapeDtypeStruct((B, S, W), BF16),
        scratch_shapes=[
            pltpu.VMEM((S, 2 * LANES), BF16),
            pltpu.VMEM((LANES, LANES), F32),
            pltpu.VMEM((2, LANES, LANES), BF16),
            pltpu.VMEM((2, LANES, LANES), BF16),
        ] + _attention_scratch(S),
        compiler_params=_params(("arbitrary", "arbitrary")),
        name="moba_attention",
    )(qa, ka, va)


def _diff_kernel(lq1_ref, lk1_ref, lq2_ref, lk2_ref, g_ref, q_ref, k_ref, v_ref, o_ref, cbias_ref, s_ref, p_ref):
    S = k_ref.shape[1]
    nb = S // ATT_BLK
    lam = (jnp.exp(jnp.sum(lq1_ref[...] * lk1_ref[...], axis=1, keepdims=True))
           - jnp.exp(jnp.sum(lq2_ref[...] * lk2_ref[...], axis=1, keepdims=True)) + LAMBDA_INIT)
    cbias_ref[...] = _causal_bias()
    lane = lax.broadcasted_iota(I32, (ATT_BLK, LANES), 1)

    for i in reversed(range(nb)):
        r0 = i * ATT_BLK
        q = q_ref[0, r0:r0 + ATT_BLK, :].astype(F32)
        qs = jnp.concatenate([jnp.where(lane < HEAD_DIM, q, 0.0), jnp.where(lane >= HEAD_DIM, q, 0.0)],
                             axis=0).astype(BF16)
        out = _attend(qs, lambda n: k_ref[0, n * ATT_BLK:(n + 1) * ATT_BLK, :], lambda L: v_ref[0, :L, :],
                      s_ref, p_ref, cbias_ref, i)
        o = out[:ATT_BLK] - lam * out[ATT_BLK:]
        o_ref[0, r0:r0 + ATT_BLK, :] = (_rms(o, g_ref[...]) * (1.0 - LAMBDA_INIT)).astype(BF16)


def _diff_attention(qd, kd, vd, lq1, lk1, lq2, lk2, subln_g):
    B, S, W = qd.shape
    assert S % ATT_BLK == 0 and W % LANES == 0
    spec = pl.BlockSpec((1, S, LANES), lambda b, j: (b, 0, j))
    small = pl.BlockSpec((1, HEAD_DIM), lambda b, j: (0, 0))
    return pl.pallas_call(
        _diff_kernel,
        grid=(B, W // LANES),
        in_specs=[small, small, small, small, pl.BlockSpec((1, LANES), lambda b, j: (0, 0)), spec, spec, spec],
        out_specs=spec,
        out_shape=jax.ShapeDtypeStruct((B, S, W), BF16),
        scratch_shapes=_attention_scratch(S),
        compiler_params=_params(("arbitrary", "arbitrary")),
        name="diff_attention",
    )(lq1, lk1, lq2, lk2, subln_g, qd, kd, vd)


def _pack_bf16_pairs(x):
    n = x.shape[1] // 2
    lo = lax.bitcast_convert_type(x[:, :n].astype(BF16).astype(F32), U32)
    hi = lax.bitcast_convert_type(x[:, n:].astype(BF16).astype(F32), U32)
    return (lo >> 16) | hi


def _unpack_bf16_pairs(w, dtype=BF16):
    lo = lax.bitcast_convert_type(w << 16, F32)
    hi = lax.bitcast_convert_type(w & jnp.uint32(0xFFFF0000), F32)
    return jnp.concatenate([lo, hi], axis=1).astype(dtype)


def _merge_kernel(x_ref, oa_ref, od_ref, gmix_ref, wg_ref, wbm_ref, wbd_ref, wout_ref, gffn_ref,
                  wr_ref, br_ref,
                  x1_ref, h2_ref, route_ref, routet_ref, cnt_ref, base_ref, ltri_ref):
    tm, D = x_ref.shape

    @pl.when(pl.program_id(0) == 0)
    def _():
        base_ref[...] = jnp.zeros_like(base_ref)
        row = lax.broadcasted_iota(I32, (tm, tm), 0)
        col = lax.broadcasted_iota(I32, (tm, tm), 1)
        ltri_ref[...] = jnp.where(col < row, 1.0, 0.0).astype(BF16)

    x = x_ref[...]
    h = _rms(x, gmix_ref[...]).astype(BF16)
    sig = jax.nn.sigmoid(_dot(h, wg_ref[...]))
    merged = sig[:, :D] * _dot(oa_ref[...], wbm_ref[...]) + sig[:, D:] * _dot(od_ref[...], wbd_ref[...])
    x1 = x + _dot(merged.astype(BF16), wout_ref[...])
    x1_ref[...] = x1
    h2 = _rms(x1, gffn_ref[...])
    h2_ref[...] = _pack_bf16_pairs(h2)

    hhi = h2.astype(BF16)
    hlo = (h2 - hhi.astype(F32)).astype(BF16)
    lg_hi = _dot(hhi, wr_ref[...])
    lg_lo = _dot(hlo, wr_ref[...])
    lg = (lg_hi[:, :LANES] + lg_hi[:, LANES:]) + (lg_lo[:, :LANES] + lg_lo[:, LANES:]) + br_ref[...]
    lane = lax.broadcasted_iota(I32, (tm, LANES), 1)
    ninf = -jnp.inf

    def first_argmax(vals):
        m = jnp.max(vals, axis=1, keepdims=True)
        return m, jnp.min(jnp.where(vals == m, lane, LANES), axis=1, keepdims=True)

    is_group = lane < N_GROUPS
    gm, gsel = first_argmax(jnp.where(is_group, lg, ninf))
    p_group = 1.0 / jnp.sum(jnp.where(is_group, jnp.exp(lg - gm), 0.0), axis=1, keepdims=True)
    lo = N_GROUPS + gsel * EXPERTS_PER_GROUP
    el = jnp.where((lane >= lo) & (lane < lo + EXPERTS_PER_GROUP), lg, ninf)
    m1, i1 = first_argmax(el)
    m2, i2 = first_argmax(jnp.where(lane == i1, ninf, el))
    e2x = jnp.exp(m2 - m1)
    w1 = p_group / (1.0 + e2x)
    w2 = p_group * e2x / (1.0 + e2x)
    e1 = i1 - N_GROUPS
    e2 = i2 - N_GROUPS

    oh1 = jnp.where(lane == e1, 1.0, 0.0)
    oh2 = jnp.where(lane == e2, 1.0, 0.0)
    c1 = _dot(ltri_ref[...], oh1.astype(BF16))
    c2 = _dot(ltri_ref[...], oh2.astype(BF16))
    base = base_ref[...]
    tot1 = jnp.sum(oh1, axis=0, keepdims=True)
    tot2 = jnp.sum(oh2, axis=0, keepdims=True)
    r1 = jnp.sum(oh1 * (c1 + base), axis=1, keepdims=True)
    r2 = jnp.sum(oh2 * (c2 + base + tot1), axis=1, keepdims=True)
    base = base + tot1 + tot2
    base_ref[...] = base
    cnt_ref[...] = base

    cols = (e1.astype(F32), e2.astype(F32), w1, w2, r1, r2)
    route = jnp.zeros((tm, LANES), F32)
    for c, val in enumerate(cols):
        route = jnp.where(lane == c, val, route)
    route_ref[...] = route[:, :ROUTE_COLS]
    routet_ref[...] = route.T[:ROUTE_COLS, :]


def _merge_route(x2, o_a, o_d, g_mix, w_g, w_bm, w_bd, w_out, g_ffn, w_r2, b_r):
    T, D = x2.shape
    tm = min(MERGE_TILE, T)
    assert T % tm == 0
    W = o_a.shape[1]
    tile = lambda w: pl.BlockSpec((tm, w), lambda i: (i, 0))
    full = lambda a: pl.BlockSpec(a.shape, lambda i: (0, 0))
    return pl.pallas_call(
        _merge_kernel,
        grid=(T // tm,),
        in_specs=[tile(D), tile(W), tile(W), full(g_mix), full(w_g), full(w_bm), full(w_bd), full(w_out),
                  full(g_ffn), full(w_r2), full(b_r)],
        out_specs=[tile(D), tile(D // 2), tile(ROUTE_COLS), pl.BlockSpec((ROUTE_COLS, tm), lambda i: (0, i)),
                   pl.BlockSpec((1, LANES), lambda i: (0, 0))],
        out_shape=[jax.ShapeDtypeStruct((T, D), F32), jax.ShapeDtypeStruct((T, D // 2), U32),
                   jax.ShapeDtypeStruct((T, ROUTE_COLS), F32), jax.ShapeDtypeStruct((ROUTE_COLS, T), F32),
                   jax.ShapeDtypeStruct((1, LANES), F32)],
        scratch_shapes=[pltpu.VMEM((1, LANES), F32), pltpu.VMEM((tm, tm), BF16)],
        compiler_params=_params(("arbitrary",)),
        name="merge_route",
    )(x2, o_a, o_d, g_mix, w_g, w_bm, w_bd, w_out, g_ffn, w_r2, b_r)


def _index_prefetch(dest_hbm, idx_ref, isem):
    i = pl.program_id(0)
    per = dest_hbm.shape[1]
    slot = i % 2

    def copy(step, sl):
        return pltpu.make_async_copy(dest_hbm.at[step], idx_ref.at[pl.ds(sl * per, per)], isem.at[sl])

    @pl.when(i == 0)
    def _():
        copy(0, 0).start()

    @pl.when(i + 1 < pl.num_programs(0))
    def _():
        copy(i + 1, 1 - slot).start()

    copy(i, slot).wait()
    return slot * per


def _dispatch_kernel(zs_ref, nu_ref, dest_hbm, h2_ref, xd_hbm, idx_ref, zero_ref, isem, rsem, zsem):
    tm = h2_ref.shape[0] * 8
    nblk = xd_hbm.shape[0] // MOE_BLK

    @pl.when(pl.program_id(0) == 0)
    def _():
        zero_ref[...] = jnp.zeros_like(zero_ref)

        def zcopy(row):
            return pltpu.make_async_copy(zero_ref, xd_hbm.at[pl.ds(pl.multiple_of(row, MOE_BLK), MOE_BLK), :], zsem)

        def tail_start(blk, carry):
            zcopy(blk * MOE_BLK).start()
            return carry

        def tail_wait(blk, carry):
            zcopy(0).wait()
            return carry

        for e in range(N_EXPERTS):
            @pl.when(zs_ref[e] >= 0)
            def _():
                zcopy(zs_ref[e]).start()
        lax.fori_loop(nu_ref[0], nblk, tail_start, 0)
        for e in range(N_EXPERTS):
            @pl.when(zs_ref[e] >= 0)
            def _():
                zcopy(0).wait()
        lax.fori_loop(nu_ref[0], nblk, tail_wait, 0)

    base = _index_prefetch(dest_hbm, idx_ref, isem)

    def group(j, carry):
        for u in range(8):
            for kk in range(EXPERT_TOPK):
                d = idx_ref[base + kk * tm + 8 * j + u]
                pltpu.make_async_copy(h2_ref.at[j, pl.ds(u, 1), :], xd_hbm.at[pl.ds(d, 1), :],
                                      rsem).start(priority=kk)
        return carry

    lax.fori_loop(0, tm // 8, group, 0)
    for kk in range(EXPERT_TOPK):
        pltpu.make_async_copy(h2_ref, h2_ref, rsem).wait()


def _dispatch(zstart, n_used, dest2, h2p, n_rows):
    T, Wp = h2p.shape
    nt, per = dest2.shape
    tm = per // EXPERT_TOPK
    return pl.pallas_call(
        _dispatch_kernel,
        grid_spec=pltpu.PrefetchScalarGridSpec(
            num_scalar_prefetch=2,
            grid=(nt,),
            in_specs=[pl.BlockSpec(memory_space=pl.ANY),
                      pl.BlockSpec((tm // 8, 8, Wp), lambda i, zs, nu: (i, 0, 0))],
            out_specs=pl.BlockSpec(memory_space=pl.ANY),
            scratch_shapes=[pltpu.SMEM((2 * per,), I32), pltpu.VMEM((MOE_BLK, Wp), U32),
                            pltpu.SemaphoreType.DMA((2,)), pltpu.SemaphoreType.DMA(()),
                            pltpu.SemaphoreType.DMA(())],
        ),
        out_shape=jax.ShapeDtypeStruct((n_rows, Wp), U32),
        compiler_params=_params(("arbitrary",)),
        name="moe_dispatch",
    )(zstart, n_used, dest2, h2p.reshape(T // 8, 8, Wp))


def _expert_kernel(be_ref, nu_ref, xd_ref, w1_ref, w3_ref, w2_ref, yb_ref, w1b, w3b, w2b):
    i = pl.program_id(0)
    n_used = nu_ref[0]

    @pl.when((i < n_used) & ((i == 0) | (be_ref[i] != be_ref[jnp.maximum(i - 1, 0)])))
    def _():
        w1b[...] = w1_ref[0].astype(BF16)
        w3b[...] = w3_ref[0].astype(BF16)
        w2b[...] = w2_ref[0].astype(BF16)

    @pl.when(i < n_used)
    def _():
        x = _unpack_bf16_pairs(xd_ref[...])
        act = jax.nn.silu(_dot(x, w1b[...])) * _dot(x, w3b[...])
        yb_ref[...] = _pack_bf16_pairs(_dot(act.astype(BF16), w2b[...]))

    @pl.when(i >= n_used)
    def _():
        yb_ref[...] = jnp.zeros_like(yb_ref)


def _experts(blk_expert, n_used, xd, w1, w3, w2):
    P, Wp = xd.shape
    E, D, De = w1.shape
    nblk = P // MOE_BLK
    return pl.pallas_call(
        _expert_kernel,
        grid_spec=pltpu.PrefetchScalarGridSpec(
            num_scalar_prefetch=2,
            grid=(nblk,),
            in_specs=[
                pl.BlockSpec((MOE_BLK, Wp), lambda i, be, nu: (jnp.minimum(i, nu[0] - 1), 0)),
                pl.BlockSpec((1, D, De), lambda i, be, nu: (be[i], 0, 0)),
                pl.BlockSpec((1, D, De), lambda i, be, nu: (be[i], 0, 0)),
                pl.BlockSpec((1, De, D), lambda i, be, nu: (be[i], 0, 0)),
            ],
            out_specs=pl.BlockSpec((MOE_BLK, D // 2), lambda i, be, nu: (i, 0)),
            scratch_shapes=[pltpu.VMEM((D, De), BF16), pltpu.VMEM((D, De), BF16), pltpu.VMEM((De, D), BF16)],
        ),
        out_shape=jax.ShapeDtypeStruct((P, D // 2), U32),
        compiler_params=_params(("arbitrary",), EXPERT_VMEM_LIMIT),
        name="moe_experts",
    )(blk_expert, n_used, xd, w1, w3, w2)


def _combine_kernel(dest_hbm, route_ref, x1_ref, g_ref, yb_hbm, o_ref, idx_ref, buf_ref, isem, rsem):
    tm, D = x1_ref.shape
    base = _index_prefetch(dest_hbm, idx_ref, isem)

    def group(j, carry):
        for u in range(8):
            for kk in range(EXPERT_TOPK):
                d = idx_ref[base + kk * tm + 8 * j + u]
                pltpu.make_async_copy(yb_hbm.at[pl.ds(d, 1), :], buf_ref.at[kk, j, pl.ds(u, 1), :],
                                      rsem).start(priority=kk)
        return carry

    lax.fori_loop(0, tm // 8, group, 0)
    for kk in range(EXPERT_TOPK):
        pltpu.make_async_copy(buf_ref.at[kk], buf_ref.at[kk], rsem).wait()
    route = route_ref[...]
    y0 = _unpack_bf16_pairs(buf_ref[0].reshape(tm, D // 2), F32)
    y1 = _unpack_bf16_pairs(buf_ref[1].reshape(tm, D // 2), F32)
    o_ref[...] = _rms(x1_ref[...] + (route[:, 2:3] * y0 + route[:, 3:4] * y1), g_ref[...])


def _combine(dest2, route, x1, g_final, yb):
    T, D = x1.shape
    nt, per = dest2.shape
    tm = per // EXPERT_TOPK
    return pl.pallas_call(
        _combine_kernel,
        grid=(nt,),
        in_specs=[pl.BlockSpec(memory_space=pl.ANY), pl.BlockSpec((tm, ROUTE_COLS), lambda i: (i, 0)),
                  pl.BlockSpec((tm, D), lambda i: (i, 0)), pl.BlockSpec((1, D), lambda i: (0, 0)),
                  pl.BlockSpec(memory_space=pl.ANY)],
        out_specs=pl.BlockSpec((tm, D), lambda i: (i, 0)),
        out_shape=jax.ShapeDtypeStruct((T, D), F32),
        scratch_shapes=[pltpu.SMEM((2 * per,), I32), pltpu.VMEM((EXPERT_TOPK, tm // 8, 8, D // 2), U32),
                        pltpu.SemaphoreType.DMA((2,)), pltpu.SemaphoreType.DMA(())],
        compiler_params=_params(("arbitrary",)),
        name="moe_combine",
    )(dest2, route, x1, g_final, yb)


def _moe(h2p, route, route_t, counts, x1, g_final, w1, w3, w2):
    T = h2p.shape[0]
    tm = min(TOK_TILE, T)
    nt = T // tm
    A = T * EXPERT_TOPK
    P = -(-A // MOE_BLK) * MOE_BLK + N_EXPERTS * MOE_BLK
    nblk = P // MOE_BLK
    cnt = counts[0, :N_EXPERTS].astype(I32)
    padded = (cnt + MOE_BLK - 1) // MOE_BLK * MOE_BLK
    pends = jnp.cumsum(padded)
    pstarts = pends - padded
    ids = route_t[0:EXPERT_TOPK].astype(I32)
    ranks = route_t[4:4 + EXPERT_TOPK].astype(I32)
    dest = ranks
    for e in range(N_EXPERTS):
        dest = dest + jnp.where(ids == e, pstarts[e], 0)
    dest2 = dest.reshape(EXPERT_TOPK, nt, tm).transpose(1, 0, 2).reshape(nt, EXPERT_TOPK * tm)
    blk_row = jnp.arange(nblk, dtype=I32) * MOE_BLK
    blk_expert = jnp.minimum(jnp.sum((pends[None, :] <= blk_row[:, None]).astype(I32), axis=1), N_EXPERTS - 1)
    n_used = (pends[-1:] // MOE_BLK).astype(I32)
    zstart = jnp.where(padded > 0, pends - MOE_BLK, -1).astype(I32)
    xd = _dispatch(zstart, n_used, dest2, h2p, P)
    yb = _experts(blk_expert, n_used, xd, w1, w3, w2)
    return _combine(dest2, route, x1, g_final, yb)


def _rope_tables(seq):
    inv = 1.0 / (ROPE_THETA ** (jnp.arange(0, HEAD_DIM, 2, dtype=F32) / HEAD_DIM))
    ang = jnp.arange(seq, dtype=F32)[:, None] * inv[None, :]
    ang = jnp.concatenate([ang, ang], axis=-1)
    sign = jnp.concatenate([-jnp.ones((HEAD_DIM // 2,), F32), jnp.ones((HEAD_DIM // 2,), F32)])
    reps = LANES // HEAD_DIM
    return jnp.tile(jnp.cos(ang), (1, reps)), jnp.tile(jnp.sin(ang) * sign, (1, reps))


def kernel(x, g_mix, w_in, w_branch_moba, w_branch_diff, w_out, diff_lambda_q1, diff_lambda_k1, diff_lambda_q2, diff_lambda_k2, diff_subln_g, g_ffn, w_group, b_group, w_router, b_router, w_expert_gate, w_expert_up, w_expert_down, g_final):
    B, S, D = x.shape
    assert w_in.shape[0] == 1, "one layer"
    T = B * S
    x2 = x.reshape(T, D)
    n_qkv = w_in.shape[2] - 2 * D
    w_qkv = w_in[0, :, :n_qkv].astype(BF16)
    w_g = w_in[0, :, n_qkv:].astype(BF16)
    cos, sin = _rope_tables(S)

    qa, ka, va, qd, kd, vd = _qkv_proj(x2, g_mix, w_qkv, cos, sin, S)
    W = qa.shape[1]
    o_a = _moba_attention(qa.reshape(B, S, W), ka.reshape(B, S, W), va.reshape(B, S, W)).reshape(T, W)
    o_d = _diff_attention(qd.reshape(B, S, W), kd.reshape(B, S, W), vd.reshape(B, S, W),
                          diff_lambda_q1, diff_lambda_k1, diff_lambda_q2, diff_lambda_k2, diff_subln_g).reshape(T, W)

    w_r = jnp.concatenate([w_group[0], w_router[0]], axis=1)
    w_r = jnp.pad(w_r, ((0, 0), (0, LANES - w_r.shape[1])))
    wr_hi = w_r.astype(BF16)
    w_r2 = jnp.concatenate([wr_hi, (w_r - wr_hi.astype(F32)).astype(BF16)], axis=1)
    b_r = jnp.pad(jnp.concatenate([b_group[0], b_router[0]])[None, :], ((0, 0), (0, LANES - N_GROUPS - N_EXPERTS)))

    x1, h2p, route, route_t, counts = _merge_route(
        x2, o_a, o_d, g_mix, w_g, w_branch_moba[0].astype(BF16), w_branch_diff[0].astype(BF16),
        w_out[0].astype(BF16), g_ffn, w_r2, b_r)

    out = _moe(h2p, route, route_t, counts, x1, g_final[None, :],
               w_expert_gate[0], w_expert_up[0], w_expert_down[0])
    return out.reshape(B, S, D)
```

```python
import functools

import jax
import jax.numpy as jnp
from jax import lax
from jax.experimental import pallas as pl
from jax.experimental.pallas import tpu as pltpu

F32 = jnp.float32
BF16 = jnp.bfloat16
I32 = jnp.int32
U32 = jnp.uint32

LANES = 128
HEAD_DIM = 64
ATT_BLK = 256
ATT_BLK_SHIFT = 8
MOBA_TOPK = 3
N_GROUPS = 4
EXPERTS_PER_GROUP = 8
N_EXPERTS = N_GROUPS * EXPERTS_PER_GROUP
EXPERT_TOPK = 2
MOE_BLK = 512
DISPATCH_TILE = 512
COMBINE_TILE = 256
MERGE_TILE = 512
QKV_TILE = 512
ROUTE_COLS = 8
EPS = 1e-6
NEG = -1e30
ROPE_THETA = 10000.0
LAMBDA_INIT = 0.8 - 0.6 * 1.0
ATT_SCALE = HEAD_DIM ** -0.5
VMEM_LIMIT = 48 * 1024 * 1024
EXPERT_VMEM_LIMIT = 56 * 1024 * 1024


def _dot(a, b):
    return jnp.dot(a, b, preferred_element_type=F32)


def _dot_nt(a, b):
    return lax.dot_general(a, b, (((1,), (1,)), ((), ())), preferred_element_type=F32)


def _rms(x, g):
    var = jnp.mean(x * x, axis=-1, keepdims=True)
    return (x * lax.rsqrt(var + EPS)) * g


def _params(sem, vmem=VMEM_LIMIT):
    return pltpu.CompilerParams(dimension_semantics=sem, vmem_limit_bytes=vmem)


def _qkv_kernel(x_ref, g_ref, w_ref, cos_ref, sin_ref, qa_ref, ka_ref, va_ref, qd_ref, kd_ref, vd_ref):
    h = _rms(x_ref[...], g_ref[...]).astype(BF16)
    cos = cos_ref[...]
    sin = sin_ref[...]
    lane = lax.broadcasted_iota(I32, cos.shape, 1)
    first = (lane & (HEAD_DIM - 1)) < HEAD_DIM // 2
    width = qa_ref.shape[1]
    outs = ((qa_ref, True, ATT_SCALE), (ka_ref, True, 1.0), (va_ref, False, 1.0),
            (qd_ref, True, ATT_SCALE), (kd_ref, True, 1.0), (vd_ref, False, 1.0))
    for j, (o_ref, rope, scale) in enumerate(outs):
        y = _dot(h, w_ref[:, j * width:(j + 1) * width])
        for c in range(width // LANES):
            yc = y[:, c * LANES:(c + 1) * LANES]
            if rope:
                rot = jnp.where(first, pltpu.roll(yc, LANES - HEAD_DIM // 2, 1), pltpu.roll(yc, HEAD_DIM // 2, 1))
                yc = yc * cos + rot * sin
            if scale != 1.0:
                yc = yc * scale
            o_ref[:, c * LANES:(c + 1) * LANES] = yc.astype(BF16)


def _qkv_proj(x2, g_mix, w_qkv, cos, sin, seq):
    T, D = x2.shape
    width = w_qkv.shape[1] // 6
    tm = min(QKV_TILE, seq)
    assert seq % tm == 0 and T % tm == 0
    spt = seq // tm
    out = jax.ShapeDtypeStruct((T, width), BF16)
    return pl.pallas_call(
        _qkv_kernel,
        grid=(T // tm,),
        in_specs=[
            pl.BlockSpec((tm, D), lambda i: (i, 0)),
            pl.BlockSpec((1, D), lambda i: (0, 0)),
            pl.BlockSpec(w_qkv.shape, lambda i: (0, 0)),
            pl.BlockSpec((tm, LANES), lambda i: (i % spt, 0)),
            pl.BlockSpec((tm, LANES), lambda i: (i % spt, 0)),
        ],
        out_specs=[pl.BlockSpec((tm, width), lambda i: (i, 0))] * 6,
        out_shape=[out] * 6,
        compiler_params=_params(("arbitrary",)),
        name="qkv_proj",
    )(x2, g_mix, w_qkv, cos, sin)


ATT_ROWS = 2 * ATT_BLK


def _attend(qs, k_block, v_rows, s_ref, p_ref, cbias_ref, i):
    mx = None
    for n in range(i + 1):
        sb = _dot_nt(qs, k_block(n))
        if n == i:
            sb = sb + cbias_ref[...]
        s_ref[:, n * ATT_BLK:(n + 1) * ATT_BLK] = sb
        for c in range(ATT_BLK // LANES):
            part = sb[:, c * LANES:(c + 1) * LANES]
            mx = part if mx is None else jnp.maximum(mx, part)
    mb = jnp.broadcast_to(jnp.max(mx, axis=1, keepdims=True), (ATT_ROWS, LANES))
    ps = None
    for c in range((i + 1) * ATT_BLK // LANES):
        p = jnp.exp(s_ref[:, c * LANES:(c + 1) * LANES] - mb)
        ps = p if ps is None else ps + p
        p_ref[:, c * LANES:(c + 1) * LANES] = p.astype(BF16)
    l = jnp.sum(ps, axis=1, keepdims=True)
    L = (i + 1) * ATT_BLK
    return _dot(p_ref[:, :L], v_rows(L)) * (1.0 / l)


def _causal_bias():
    row = lax.broadcasted_iota(I32, (ATT_ROWS, ATT_BLK), 0) & (ATT_BLK - 1)
    col = lax.broadcasted_iota(I32, (ATT_ROWS, ATT_BLK), 1)
    return jnp.where(col <= row, 0.0, NEG).astype(F32)


def _attention_scratch(seq):
    return [pltpu.VMEM((ATT_ROWS, ATT_BLK), F32), pltpu.VMEM((ATT_ROWS, seq), F32), pltpu.VMEM((ATT_ROWS, seq), BF16)]


def _moba_kernel(q_ref, k_ref, v_ref, o_ref, kaug_ref, kmf_ref, kmhi_ref, kmlo_ref, cbias_ref, s_ref, p_ref):
    S = k_ref.shape[1]
    nb = S // ATT_BLK
    k = k_ref[0]
    lane_s = lax.broadcasted_iota(I32, (S, LANES), 1)
    blk_s = lax.broadcasted_iota(I32, (S, LANES), 0) >> ATT_BLK_SHIFT
    kaug_ref[:, :LANES] = k
    kaug_ref[:, LANES:] = jnp.where(lane_s == blk_s, 1.0, 0.0).astype(BF16)
    kmean = jnp.mean(k.astype(F32).reshape(nb, ATT_BLK, LANES), axis=1)
    lane_n = lax.broadcasted_iota(I32, (nb, LANES), 1)
    for hh in range(2):
        own_n = (lane_n < HEAD_DIM) if hh == 0 else (lane_n >= HEAD_DIM)
        kmf_ref[...] = jnp.zeros((LANES, LANES), F32)
        kmf_ref[0:nb, :] = jnp.where(own_n, kmean, 0.0)
        km = kmf_ref[...]
        hi = km.astype(BF16)
        kmhi_ref[hh] = hi
        kmlo_ref[hh] = (km - hi.astype(F32)).astype(BF16)
    cbias_ref[...] = _causal_bias()

    lane = lax.broadcasted_iota(I32, (ATT_BLK, LANES), 1)

    def stacked_queries(i):
        q = q_ref[0, i * ATT_BLK:(i + 1) * ATT_BLK, :].astype(F32)
        rows = []
        for hh in range(2):
            own = (lane < HEAD_DIM) if hh == 0 else (lane >= HEAD_DIM)
            qh = jnp.where(own, q, 0.0)
            if i <= MOBA_TOPK:
                rows.append(qh)
                continue
            qhb = qh.astype(BF16)
            gate = _dot_nt(qhb, kmhi_ref[hh]) + _dot_nt(qhb, kmlo_ref[hh])
            cand = lane < i
            g = jnp.where(cand, gate, -jnp.inf)
            rank = jnp.zeros((ATT_BLK, LANES), I32)
            for r in range(1, i):
                rank = rank + jnp.where(pltpu.roll(g, r, 1) >= g, 1, 0)
                rank = rank + jnp.where(pltpu.roll(g, LANES - r, 1) > g, 1, 0)
            drop = jnp.where(cand & (rank >= MOBA_TOPK), NEG, 0.0)
            rows.append(jnp.concatenate([qh, drop], axis=1))
        return jnp.concatenate(rows, axis=0).astype(BF16)

    for i in _moba_block_order(nb):
        if i > MOBA_TOPK:
            k_block = lambda n: kaug_ref[n * ATT_BLK:(n + 1) * ATT_BLK, :]
        else:
            k_block = lambda n: k_ref[0, n * ATT_BLK:(n + 1) * ATT_BLK, :]
        out = _attend(stacked_queries(i), k_block, lambda L: v_ref[0, :L, :], s_ref, p_ref, cbias_ref, i)
        o_ref[0, i * ATT_BLK:(i + 1) * ATT_BLK, :] = jnp.where(
            lane < HEAD_DIM, out[:ATT_BLK], out[ATT_BLK:]).astype(BF16)


def _moba_block_order(nb):
    small = list(range(min(nb, MOBA_TOPK + 1)))[::-1]
    large = list(range(MOBA_TOPK + 1, nb))[::-1]
    order = []
    while small or large:
        if small:
            order.append(small.pop(0))
        if large:
            order.append(large.pop(0))
    return order


def _moba_attention(qa, ka, va):
    B, S, W = qa.shape
    assert S % ATT_BLK == 0 and S // ATT_BLK <= LANES and W % LANES == 0
    spec = pl.BlockSpec((1, S, LANES), lambda b, j: (b, 0, j))
    return pl.pallas_call(
        _moba_kernel,
        grid=(B, W // LANES),
        in_specs=[spec, spec, spec],
        out_specs=spec,
        out_shape=jax.ShapeDtypeStruct((B, S, W), BF16),
        scratch_shapes=[
            pltpu.VMEM((S, 2 * LANES), BF16),
            pltpu.VMEM((LANES, LANES), F32),
            pltpu.VMEM((2, LANES, LANES), BF16),
            pltpu.VMEM((2, LANES, LANES), BF16),
        ] + _attention_scratch(S),
        compiler_params=_params(("arbitrary", "arbitrary")),
        name="moba_attention",
    )(qa, ka, va)


def _diff_kernel(lq1_ref, lk1_ref, lq2_ref, lk2_ref, g_ref, q_ref, k_ref, v_ref, o_ref, cbias_ref, s_ref, p_ref):
    S = k_ref.shape[1]
    nb = S // ATT_BLK
    lam = (jnp.exp(jnp.sum(lq1_ref[...] * lk1_ref[...], axis=1, keepdims=True))
           - jnp.exp(jnp.sum(lq2_ref[...] * lk2_ref[...], axis=1, keepdims=True)) + LAMBDA_INIT)
    cbias_ref[...] = _causal_bias()
    lane = lax.broadcasted_iota(I32, (ATT_BLK, LANES), 1)

    for i in reversed(range(nb)):
        r0 = i * ATT_BLK
        q = q_ref[0, r0:r0 + ATT_BLK, :].astype(F32)
        qs = jnp.concatenate([jnp.where(lane < HEAD_DIM, q, 0.0), jnp.where(lane >= HEAD_DIM, q, 0.0)],
                             axis=0).astype(BF16)
        out = _attend(qs, lambda n: k_ref[0, n * ATT_BLK:(n + 1) * ATT_BLK, :], lambda L: v_ref[0, :L, :],
                      s_ref, p_ref, cbias_ref, i)
        o = out[:ATT_BLK] - lam * out[ATT_BLK:]
        o_ref[0, r0:r0 + ATT_BLK, :] = (_rms(o, g_ref[...]) * (1.0 - LAMBDA_INIT)).astype(BF16)


def _diff_attention(qd, kd, vd, lq1, lk1, lq2, lk2, subln_g):
    B, S, W = qd.shape
    assert S % ATT_BLK == 0 and W % LANES == 0
    spec = pl.BlockSpec((1, S, LANES), lambda b, j: (b, 0, j))
    small = pl.BlockSpec((1, HEAD_DIM), lambda b, j: (0, 0))
    return pl.pallas_call(
        _diff_kernel,
        grid=(B, W // LANES),
        in_specs=[small, small, small, small, pl.BlockSpec((1, LANES), lambda b, j: (0, 0)), spec, spec, spec],
        out_specs=spec,
        out_shape=jax.ShapeDtypeStruct((B, S, W), BF16),
        scratch_shapes=_attention_scratch(S),
        compiler_params=_params(("arbitrary", "arbitrary")),
        name="diff_attention",
    )(lq1, lk1, lq2, lk2, subln_g, qd, kd, vd)


def _pack_bf16_pairs(x):
    n = x.shape[1] // 2
    lo = lax.bitcast_convert_type(x[:, :n].astype(BF16).astype(F32), U32)
    hi = lax.bitcast_convert_type(x[:, n:].astype(BF16).astype(F32), U32)
    return (lo >> 16) | hi


def _unpack_bf16_pairs(w, dtype=BF16):
    lo = lax.bitcast_convert_type(w << 16, F32)
    hi = lax.bitcast_convert_type(w & jnp.uint32(0xFFFF0000), F32)
    return jnp.concatenate([lo, hi], axis=1).astype(dtype)


def _merge_kernel(x_ref, oa_ref, od_ref, gmix_ref, wg_ref, wbm_ref, wbd_ref, wout_ref, gffn_ref,
                  wr_ref, br_ref,
                  x1_ref, h2_ref, route_ref, routet_ref, cnt_ref, base_ref, ltri_ref):
    tm, D = x_ref.shape

    @pl.when(pl.program_id(0) == 0)
    def _():
        base_ref[...] = jnp.zeros_like(base_ref)
        row = lax.broadcasted_iota(I32, (tm, tm), 0)
        col = lax.broadcasted_iota(I32, (tm, tm), 1)
        ltri_ref[...] = jnp.where(col < row, 1.0, 0.0).astype(BF16)

    x = x_ref[...]
    h = _rms(x, gmix_ref[...]).astype(BF16)
    sig = jax.nn.sigmoid(_dot(h, wg_ref[...]))
    merged = sig[:, :D] * _dot(oa_ref[...], wbm_ref[...]) + sig[:, D:] * _dot(od_ref[...], wbd_ref[...])
    x1 = x + _dot(merged.astype(BF16), wout_ref[...])
    x1_ref[...] = x1
    h2 = _rms(x1, gffn_ref[...])
    h2_ref[...] = _pack_bf16_pairs(h2)

    hhi = h2.astype(BF16)
    hlo = (h2 - hhi.astype(F32)).astype(BF16)
    lg_hi = _dot(hhi, wr_ref[...])
    lg_lo = _dot(hlo, wr_ref[...])
    lg = (lg_hi[:, :LANES] + lg_hi[:, LANES:]) + (lg_lo[:, :LANES] + lg_lo[:, LANES:]) + br_ref[...]
    lane = lax.broadcasted_iota(I32, (tm, LANES), 1)
    ninf = -jnp.inf

    def first_argmax(vals):
        m = jnp.max(vals, axis=1, keepdims=True)
        return m, jnp.min(jnp.where(vals == m, lane, LANES), axis=1, keepdims=True)

    is_group = lane < N_GROUPS
    gm, gsel = first_argmax(jnp.where(is_group, lg, ninf))
    p_group = 1.0 / jnp.sum(jnp.where(is_group, jnp.exp(lg - gm), 0.0), axis=1, keepdims=True)
    lo = N_GROUPS + gsel * EXPERTS_PER_GROUP
    el = jnp.where((lane >= lo) & (lane < lo + EXPERTS_PER_GROUP), lg, ninf)
    m1, i1 = first_argmax(el)
    m2, i2 = first_argmax(jnp.where(lane == i1, ninf, el))
    e2x = jnp.exp(m2 - m1)
    w1 = p_group / (1.0 + e2x)
    w2 = p_group * e2x / (1.0 + e2x)
    e1 = i1 - N_GROUPS
    e2 = i2 - N_GROUPS

    oh1 = jnp.where(lane == e1, 1.0, 0.0)
    oh2 = jnp.where(lane == e2, 1.0, 0.0)
    c1 = _dot(ltri_ref[...], oh1.astype(BF16))
    c2 = _dot(ltri_ref[...], oh2.astype(BF16))
    base = base_ref[...]
    tot1 = jnp.sum(oh1, axis=0, keepdims=True)
    tot2 = jnp.sum(oh2, axis=0, keepdims=True)
    r1 = jnp.sum(oh1 * (c1 + base), axis=1, keepdims=True)
    r2 = jnp.sum(oh2 * (c2 + base + tot1), axis=1, keepdims=True)
    base = base + tot1 + tot2
    base_ref[...] = base
    cnt_ref[...] = base

    cols = (e1.astype(F32), e2.astype(F32), w1, w2, r1, r2)
    route = jnp.zeros((tm, LANES), F32)
    for c, val in enumerate(cols):
        route = jnp.where(lane == c, val, route)
    route_ref[...] = route[:, :ROUTE_COLS]
    routet_ref[...] = route.T[:ROUTE_COLS, :]


def _merge_route(x2, o_a, o_d, g_mix, w_g, w_bm, w_bd, w_out, g_ffn, w_r2, b_r):
    T, D = x2.shape
    tm = min(MERGE_TILE, T)
    assert T % tm == 0
    W = o_a.shape[1]
    tile = lambda w: pl.BlockSpec((tm, w), lambda i: (i, 0))
    full = lambda a: pl.BlockSpec(a.shape, lambda i: (0, 0))
    return pl.pallas_call(
        _merge_kernel,
        grid=(T // tm,),
        in_specs=[tile(D), tile(W), tile(W), full(g_mix), full(w_g), full(w_bm), full(w_bd), full(w_out),
                  full(g_ffn), full(w_r2), full(b_r)],
        out_specs=[tile(D), tile(D // 2), tile(ROUTE_COLS), pl.BlockSpec((ROUTE_COLS, tm), lambda i: (0, i)),
                   pl.BlockSpec((1, LANES), lambda i: (0, 0))],
        out_shape=[jax.ShapeDtypeStruct((T, D), F32), jax.ShapeDtypeStruct((T, D // 2), U32),
                   jax.ShapeDtypeStruct((T, ROUTE_COLS), F32), jax.ShapeDtypeStruct((ROUTE_COLS, T), F32),
                   jax.ShapeDtypeStruct((1, LANES), F32)],
        scratch_shapes=[pltpu.VMEM((1, LANES), F32), pltpu.VMEM((tm, tm), BF16)],
        compiler_params=_params(("arbitrary",)),
        name="merge_route",
    )(x2, o_a, o_d, g_mix, w_g, w_bm, w_bd, w_out, g_ffn, w_r2, b_r)


def _index_prefetch(dest_hbm, idx_ref, isem):
    i = pl.program_id(0)
    per = dest_hbm.shape[1]
    slot = i % 2

    def copy(step, sl):
        return pltpu.make_async_copy(dest_hbm.at[step], idx_ref.at[pl.ds(sl * per, per)], isem.at[sl])

    @pl.when(i == 0)
    def _():
        copy(0, 0).start()

    @pl.when(i + 1 < pl.num_programs(0))
    def _():
        copy(i + 1, 1 - slot).start()

    copy(i, slot).wait()
    return slot * per


def _dispatch_kernel(zs_ref, nu_ref, dest_hbm, h2_ref, xd_hbm, idx_ref, zero_ref, isem, rsem, zsem):
    tm = h2_ref.shape[0] * 8
    nblk = xd_hbm.shape[0] // MOE_BLK
    i = pl.program_id(0)
    last = pl.num_programs(0) - 1

    @pl.when(i == 0)
    def _():
        zero_ref[...] = jnp.zeros_like(zero_ref)

        def zcopy(row):
            return pltpu.make_async_copy(zero_ref, xd_hbm.at[pl.ds(pl.multiple_of(row, MOE_BLK), MOE_BLK), :], zsem)

        def tail_start(blk, carry):
            zcopy(blk * MOE_BLK).start()
            return carry

        def tail_wait(blk, carry):
            zcopy(0).wait()
            return carry

        for e in range(N_EXPERTS):
            @pl.when(zs_ref[e] >= 0)
            def _():
                zcopy(zs_ref[e]).start()
        lax.fori_loop(nu_ref[0], nblk, tail_start, 0)
        for e in range(N_EXPERTS):
            @pl.when(zs_ref[e] >= 0)
            def _():
                zcopy(0).wait()
        lax.fori_loop(nu_ref[0], nblk, tail_wait, 0)

    base = _index_prefetch(dest_hbm, idx_ref, isem)
    for r in range(tm):
        for kk in range(EXPERT_TOPK):
            d = idx_ref[base + kk * tm + r]
            pltpu.make_async_copy(h2_ref.at[r // 8, pl.ds(r % 8, 1), :], xd_hbm.at[pl.ds(d, 1), :],
                                  rsem).start(priority=kk)
    for kk in range(EXPERT_TOPK):
        pltpu.make_async_copy(h2_ref, h2_ref, rsem).wait()


def _dispatch(zstart, n_used, dest2, h2p, n_rows):
    T, Wp = h2p.shape
    nt, per = dest2.shape
    tm = per // EXPERT_TOPK
    return pl.pallas_call(
        _dispatch_kernel,
        grid_spec=pltpu.PrefetchScalarGridSpec(
            num_scalar_prefetch=2,
            grid=(nt,),
            in_specs=[pl.BlockSpec(memory_space=pl.ANY),
                      pl.BlockSpec((tm // 8, 8, Wp), lambda i, zs, nu: (i, 0, 0))],
            out_specs=pl.BlockSpec(memory_space=pl.ANY),
            scratch_shapes=[pltpu.SMEM((2 * per,), I32), pltpu.VMEM((MOE_BLK, Wp), U32),
                            pltpu.SemaphoreType.DMA((2,)), pltpu.SemaphoreType.DMA(()),
                            pltpu.SemaphoreType.DMA(())],
        ),
        out_shape=jax.ShapeDtypeStruct((n_rows, Wp), U32),
        compiler_params=_params(("arbitrary",)),
        name="moe_dispatch",
    )(zstart, n_used, dest2, h2p.reshape(T // 8, 8, Wp))


def _expert_kernel(be_ref, nu_ref, xd_ref, w1_ref, w3_ref, w2_ref, yb_ref, w1b, w3b, w2b):
    i = pl.program_id(0)
    n_used = nu_ref[0]

    @pl.when((i < n_used) & ((i == 0) | (be_ref[i] != be_ref[jnp.maximum(i - 1, 0)])))
    def _():
        w1b[...] = w1_ref[0].astype(BF16)
        w3b[...] = w3_ref[0].astype(BF16)
        w2b[...] = w2_ref[0].astype(BF16)

    @pl.when(i < n_used)
    def _():
        x = _unpack_bf16_pairs(xd_ref[...])
        act = jax.nn.silu(_dot(x, w1b[...])) * _dot(x, w3b[...])
        yb_ref[...] = _pack_bf16_pairs(_dot(act.astype(BF16), w2b[...]))

    @pl.when(i >= n_used)
    def _():
        yb_ref[...] = jnp.zeros_like(yb_ref)


def _experts(blk_expert, n_used, xd, w1, w3, w2):
    P, Wp = xd.shape
    E, D, De = w1.shape
    nblk = P // MOE_BLK
    return pl.pallas_call(
        _expert_kernel,
        grid_spec=pltpu.PrefetchScalarGridSpec(
            num_scalar_prefetch=2,
            grid=(nblk,),
            in_specs=[
                pl.BlockSpec((MOE_BLK, Wp), lambda i, be, nu: (jnp.minimum(i, nu[0] - 1), 0)),
                pl.BlockSpec((1, D, De), lambda i, be, nu: (be[i], 0, 0)),
                pl.BlockSpec((1, D, De), lambda i, be, nu: (be[i], 0, 0)),
                pl.BlockSpec((1, De, D), lambda i, be, nu: (be[i], 0, 0)),
            ],
            out_specs=pl.BlockSpec((MOE_BLK, D // 2), lambda i, be, nu: (i, 0)),
            scratch_shapes=[pltpu.VMEM((D, De), BF16), pltpu.VMEM((D, De), BF16), pltpu.VMEM((De, D), BF16)],
        ),
        out_shape=jax.ShapeDtypeStruct((P, D // 2), U32),
        compiler_params=_params(("arbitrary",), EXPERT_VMEM_LIMIT),
        name="moe_experts",
    )(blk_expert, n_used, xd, w1, w3, w2)


def _combine_kernel(dest_hbm, route_ref, x1_ref, g_ref, yb_hbm, o_ref, idx_ref, buf_ref, isem, rsem):
    tm, D = x1_ref.shape
    per = EXPERT_TOPK * tm
    i = pl.program_id(0)
    last = pl.num_programs(0) - 1

    def idx_copy(step):
        sl = step % 3
        return pltpu.make_async_copy(dest_hbm.at[step], idx_ref.at[pl.ds(sl * per, per)], isem.at[sl])

    def issue(step, bs, j):
        ib = (step % 3) * per
        for u in range(8):
            for kk in range(EXPERT_TOPK):
                d = idx_ref[ib + kk * tm + 8 * j + u]
                pltpu.make_async_copy(yb_hbm.at[pl.ds(d, 1), :], buf_ref.at[bs, kk, j, pl.ds(u, 1), :],
                                      rsem.at[bs]).start(priority=kk)

    def drain(bs):
        for kk in range(EXPERT_TOPK):
            pltpu.make_async_copy(buf_ref.at[bs, kk], buf_ref.at[bs, kk], rsem.at[bs]).wait()

    @pl.when(i == 0)
    def _():
        idx_copy(0).start()

        @pl.when(last > 0)
        def _():
            idx_copy(1).start()

        idx_copy(0).wait()
        for j in range(tm // 8):
            issue(0, 0, j)

    @pl.when(i + 2 <= last)
    def _():
        idx_copy(i + 2).start()

    @pl.when(i < last)
    def _():
        idx_copy(i + 1).wait()

    cur = i % 2
    drain(cur)
    nxt = jnp.minimum(i + 1, last)
    for j in range(tm // 8):
        issue(nxt, 1 - cur, j)
        rows = slice(8 * j, 8 * j + 8)
        route = route_ref[rows, :]
        y0 = _unpack_bf16_pairs(buf_ref[cur, 0, j], F32)
        y1 = _unpack_bf16_pairs(buf_ref[cur, 1, j], F32)
        o_ref[rows, :] = _rms(x1_ref[rows, :] + (route[:, 2:3] * y0 + route[:, 3:4] * y1), g_ref[...])

    @pl.when(i == last)
    def _():
        drain(1 - cur)


def _combine(dest2, route, x1, g_final, yb):
    T, D = x1.shape
    nt, per = dest2.shape
    tm = per // EXPERT_TOPK
    return pl.pallas_call(
        _combine_kernel,
        grid=(nt,),
        in_specs=[pl.BlockSpec(memory_space=pl.ANY), pl.BlockSpec((tm, ROUTE_COLS), lambda i: (i, 0)),
                  pl.BlockSpec((tm, D), lambda i: (i, 0)), pl.BlockSpec((1, D), lambda i: (0, 0)),
                  pl.BlockSpec(memory_space=pl.ANY)],
        out_specs=pl.BlockSpec((tm, D), lambda i: (i, 0)),
        out_shape=jax.ShapeDtypeStruct((T, D), F32),
        scratch_shapes=[pltpu.SMEM((3 * per,), I32), pltpu.VMEM((2, EXPERT_TOPK, tm // 8, 8, D // 2), U32),
                        pltpu.SemaphoreType.DMA((3,)), pltpu.SemaphoreType.DMA((2,))],
        compiler_params=_params(("arbitrary",)),
        name="moe_combine",
    )(dest2, route, x1, g_final, yb)


def _moe(h2p, route, route_t, counts, x1, g_final, w1, w3, w2):
    T = h2p.shape[0]
    A = T * EXPERT_TOPK
    P = -(-A // MOE_BLK) * MOE_BLK + N_EXPERTS * MOE_BLK
    nblk = P // MOE_BLK
    cnt = counts[0, :N_EXPERTS].astype(I32)
    padded = (cnt + MOE_BLK - 1) // MOE_BLK * MOE_BLK
    pends = jnp.cumsum(padded)
    pstarts = pends - padded
    ids = route_t[0:EXPERT_TOPK].astype(I32)
    ranks = route_t[4:4 + EXPERT_TOPK].astype(I32)
    dest = ranks
    for e in range(N_EXPERTS):
        dest = dest + jnp.where(ids == e, pstarts[e], 0)

    def tiled(tm):
        tm = min(tm, T)
        return dest.reshape(EXPERT_TOPK, T // tm, tm).transpose(1, 0, 2).reshape(T // tm, EXPERT_TOPK * tm)

    blk_row = jnp.arange(nblk, dtype=I32) * MOE_BLK
    blk_expert = jnp.minimum(jnp.sum((pends[None, :] <= blk_row[:, None]).astype(I32), axis=1), N_EXPERTS - 1)
    n_used = (pends[-1:] // MOE_BLK).astype(I32)
    zstart = jnp.where(padded > 0, pends - MOE_BLK, -1).astype(I32)
    xd = _dispatch(zstart, n_used, tiled(DISPATCH_TILE), h2p, P)
    yb = _experts(blk_expert, n_used, xd, w1, w3, w2)
    return _combine(tiled(COMBINE_TILE), route, x1, g_final, yb)


def _rope_tables(seq):
    inv = 1.0 / (ROPE_THETA ** (jnp.arange(0, HEAD_DIM, 2, dtype=F32) / HEAD_DIM))
    ang = jnp.arange(seq, dtype=F32)[:, None] * inv[None, :]
    ang = jnp.concatenate([ang, ang], axis=-1)
    sign = jnp.concatenate([-jnp.ones((HEAD_DIM // 2,), F32), jnp.ones((HEAD_DIM // 2,), F32)])
    reps = LANES // HEAD_DIM
    return jnp.tile(jnp.cos(ang), (1, reps)), jnp.tile(jnp.sin(ang) * sign, (1, reps))


def kernel(x, g_mix, w_in, w_branch_moba, w_branch_diff, w_out, diff_lambda_q1, diff_lambda_k1, diff_lambda_q2, diff_lambda_k2, diff_subln_g, g_ffn, w_group, b_group, w_router, b_router, w_expert_gate, w_expert_up, w_expert_down, g_final):
    B, S, D = x.shape
    assert w_in.shape[0] == 1, "one layer"
    T = B * S
    x2 = x.reshape(T, D)
    n_qkv = w_in.shape[2] - 2 * D
    w_qkv = w_in[0, :, :n_qkv].astype(BF16)
    w_g = w_in[0, :, n_qkv:].astype(BF16)
    cos, sin = _rope_tables(S)

    qa, ka, va, qd, kd, vd = _qkv_proj(x2, g_mix, w_qkv, cos, sin, S)
    W = qa.shape[1]
    o_a = _moba_attention(qa.reshape(B, S, W), ka.reshape(B, S, W), va.reshape(B, S, W)).reshape(T, W)
    o_d = _diff_attention(qd.reshape(B, S, W), kd.reshape(B, S, W), vd.reshape(B, S, W),
                          diff_lambda_q1, diff_lambda_k1, diff_lambda_q2, diff_lambda_k2, diff_subln_g).reshape(T, W)

    w_r = jnp.concatenate([w_group[0], w_router[0]], axis=1)
    w_r = jnp.pad(w_r, ((0, 0), (0, LANES - w_r.shape[1])))
    wr_hi = w_r.astype(BF16)
    w_r2 = jnp.concatenate([wr_hi, (w_r - wr_hi.astype(F32)).astype(BF16)], axis=1)
    b_r = jnp.pad(jnp.concatenate([b_group[0], b_router[0]])[None, :], ((0, 0), (0, LANES - N_GROUPS - N_EXPERTS)))

    x1, h2p, route, route_t, counts = _merge_route(
        x2, o_a, o_d, g_mix, w_g, w_branch_moba[0].astype(BF16), w_branch_diff[0].astype(BF16),
        w_out[0].astype(BF16), g_ffn, w_r2, b_r)

    out = _moe(h2p, route, route_t, counts, x1, g_final[None, :],
               w_expert_gate[0], w_expert_up[0], w_expert_down[0])
    return out.reshape(B, S, D)
```

```python
import functools

import jax
import jax.numpy as jnp
from jax import lax
from jax.experimental import pallas as pl
from jax.experimental.pallas import tpu as pltpu

F32 = jnp.float32
BF16 = jnp.bfloat16
I32 = jnp.int32
U32 = jnp.uint32

LANES = 128
SUBLANES = 8
HEAD_DIM = 64
ATT_BLK = 256
ATT_BLK_SHIFT = 8
MOBA_TOPK = 3
N_GROUPS = 4
EXPERTS_PER_GROUP = 8
N_EXPERTS = N_GROUPS * EXPERTS_PER_GROUP
EXPERT_TOPK = 2
MOE_BLK = 512
DISPATCH_TILE = 512
COMBINE_TILE = 256
MERGE_TILE = 512
QKV_TILE = 512
ROUTE_COLS = 8
EPS = 1e-6
NEG = -1e30
ROPE_THETA = 10000.0
LAMBDA_INIT = 0.8 - 0.6 * 1.0
ATT_SCALE = HEAD_DIM ** -0.5
VMEM_LIMIT = 48 * 1024 * 1024
EXPERT_VMEM_LIMIT = 56 * 1024 * 1024


def _dot(a, b):
    return jnp.dot(a, b, preferred_element_type=F32)


def _dot_nt(a, b):
    return lax.dot_general(a, b, (((1,), (1,)), ((), ())), preferred_element_type=F32)


def _rms(x, g):
    var = jnp.mean(x * x, axis=-1, keepdims=True)
    return (x * lax.rsqrt(var + EPS)) * g


def _params(sem, vmem=VMEM_LIMIT):
    return pltpu.CompilerParams(dimension_semantics=sem, vmem_limit_bytes=vmem)


def _qkv_kernel(x_ref, g_ref, w_ref, cos_ref, sin_ref, qa_ref, ka_ref, va_ref, qd_ref, kd_ref, vd_ref):
    h = _rms(x_ref[...], g_ref[...]).astype(BF16)
    cos = cos_ref[...]
    sin = sin_ref[...]
    lane = lax.broadcasted_iota(I32, cos.shape, 1)
    first = (lane & (HEAD_DIM - 1)) < HEAD_DIM // 2
    width = qa_ref.shape[1]
    outs = ((qa_ref, True, ATT_SCALE), (ka_ref, True, 1.0), (va_ref, False, 1.0),
            (qd_ref, True, ATT_SCALE), (kd_ref, True, 1.0), (vd_ref, False, 1.0))
    for j, (o_ref, rope, scale) in enumerate(outs):
        y = _dot(h, w_ref[:, j * width:(j + 1) * width])
        for c in range(width // LANES):
            yc = y[:, c * LANES:(c + 1) * LANES]
            if rope:
                rot = jnp.where(first, pltpu.roll(yc, LANES - HEAD_DIM // 2, 1), pltpu.roll(yc, HEAD_DIM // 2, 1))
                yc = yc * cos + rot * sin
            if scale != 1.0:
                yc = yc * scale
            o_ref[:, c * LANES:(c + 1) * LANES] = yc.astype(BF16)


def _qkv_proj(x2, g_mix, w_qkv, cos, sin, seq):
    T, D = x2.shape
    width = w_qkv.shape[1] // 6
    tm = min(QKV_TILE, seq)
    assert seq % tm == 0 and T % tm == 0
    spt = seq // tm
    out = jax.ShapeDtypeStruct((T, width), BF16)
    return pl.pallas_call(
        _qkv_kernel,
        grid=(T // tm,),
        in_specs=[
            pl.BlockSpec((tm, D), lambda i: (i, 0)),
            pl.BlockSpec((1, D), lambda i: (0, 0)),
            pl.BlockSpec(w_qkv.shape, lambda i: (0, 0)),
            pl.BlockSpec((tm, LANES), lambda i: (i % spt, 0)),
            pl.BlockSpec((tm, LANES), lambda i: (i % spt, 0)),
        ],
        out_specs=[pl.BlockSpec((tm, width), lambda i: (i, 0))] * 6,
        out_shape=[out] * 6,
        compiler_params=_params(("arbitrary",)),
        name="qkv_proj",
    )(x2, g_mix, w_qkv, cos, sin)


ATT_ROWS = 2 * ATT_BLK


def _attend(qs, k_block, v_rows, s_ref, p_ref, cbias_ref, i):
    mx = None
    for n in range(i + 1):
        sb = _dot_nt(qs, k_block(n))
        if n == i:
            sb = sb + cbias_ref[...]
        s_ref[:, n * ATT_BLK:(n + 1) * ATT_BLK] = sb
        for c in range(ATT_BLK // LANES):
            part = sb[:, c * LANES:(c + 1) * LANES]
            mx = part if mx is None else jnp.maximum(mx, part)
    mb = jnp.broadcast_to(jnp.max(mx, axis=1, keepdims=True), (ATT_ROWS, LANES))
    ps = None
    for c in range((i + 1) * ATT_BLK // LANES):
        p = jnp.exp(s_ref[:, c * LANES:(c + 1) * LANES] - mb)
        ps = p if ps is None else ps + p
        p_ref[:, c * LANES:(c + 1) * LANES] = p.astype(BF16)
    l = jnp.sum(ps, axis=1, keepdims=True)
    L = (i + 1) * ATT_BLK
    return _dot(p_ref[:, :L], v_rows(L)) * (1.0 / l)


def _causal_bias():
    row = lax.broadcasted_iota(I32, (ATT_ROWS, ATT_BLK), 0) & (ATT_BLK - 1)
    col = lax.broadcasted_iota(I32, (ATT_ROWS, ATT_BLK), 1)
    return jnp.where(col <= row, 0.0, NEG).astype(F32)


def _attention_scratch(seq):
    return [pltpu.VMEM((ATT_ROWS, ATT_BLK), F32), pltpu.VMEM((ATT_ROWS, seq), F32), pltpu.VMEM((ATT_ROWS, seq), BF16)]


def _moba_kernel(q_ref, k_ref, v_ref, o_ref, kaug_ref, kmf_ref, kmhi_ref, kmlo_ref, cbias_ref, s_ref, p_ref):
    S = k_ref.shape[1]
    nb = S // ATT_BLK
    k = k_ref[0]
    lane_s = lax.broadcasted_iota(I32, (S, LANES), 1)
    blk_s = lax.broadcasted_iota(I32, (S, LANES), 0) >> ATT_BLK_SHIFT
    kaug_ref[:, :LANES] = k
    kaug_ref[:, LANES:] = jnp.where(lane_s == blk_s, 1.0, 0.0).astype(BF16)
    kmean = jnp.mean(k.astype(F32).reshape(nb, ATT_BLK, LANES), axis=1)
    lane_n = lax.broadcasted_iota(I32, (nb, LANES), 1)
    for hh in range(2):
        own_n = (lane_n < HEAD_DIM) if hh == 0 else (lane_n >= HEAD_DIM)
        kmf_ref[...] = jnp.zeros((LANES, LANES), F32)
        kmf_ref[0:nb, :] = jnp.where(own_n, kmean, 0.0)
        km = kmf_ref[...]
        hi = km.astype(BF16)
        kmhi_ref[hh] = hi
        kmlo_ref[hh] = (km - hi.astype(F32)).astype(BF16)
    cbias_ref[...] = _causal_bias()

    lane = lax.broadcasted_iota(I32, (ATT_BLK, LANES), 1)

    def stacked_queries(i):
        q = q_ref[0, i * ATT_BLK:(i + 1) * ATT_BLK, :].astype(F32)
        rows = []
        for hh in range(2):
            own = (lane < HEAD_DIM) if hh == 0 else (lane >= HEAD_DIM)
            qh = jnp.where(own, q, 0.0)
            if i <= MOBA_TOPK:
                rows.append(qh)
                continue
            qhb = qh.astype(BF16)
            gate = (_dot_nt(kmhi_ref[hh], qhb) + _dot_nt(kmlo_ref[hh], qhb))[:SUBLANES, :]
            blk = lax.broadcasted_iota(I32, (SUBLANES, ATT_BLK), 0)
            cand = blk < i
            g = jnp.where(cand, gate, -jnp.inf)
            rank = jnp.zeros((SUBLANES, ATT_BLK), I32)
            for r in range(1, i):
                rank = rank + jnp.where((blk >= r) & (pltpu.roll(g, r, 0) >= g), 1, 0)
                rank = rank + jnp.where((blk < SUBLANES - r) & (pltpu.roll(g, SUBLANES - r, 0) > g), 1, 0)
            drop_t = jnp.where(cand & (rank >= MOBA_TOPK), NEG, 0.0)
            drop = jnp.concatenate([drop_t, jnp.zeros((LANES - SUBLANES, ATT_BLK), F32)], axis=0).T
            rows.append(jnp.concatenate([qh, drop], axis=1))
        return jnp.concatenate(rows, axis=0).astype(BF16)

    for i in reversed(range(nb)):
        if i > MOBA_TOPK:
            k_block = lambda n: kaug_ref[n * ATT_BLK:(n + 1) * ATT_BLK, :]
        else:
            k_block = lambda n: k_ref[0, n * ATT_BLK:(n + 1) * ATT_BLK, :]
        out = _attend(stacked_queries(i), k_block, lambda L: v_ref[0, :L, :], s_ref, p_ref, cbias_ref, i)
        o_ref[0, i * ATT_BLK:(i + 1) * ATT_BLK, :] = jnp.where(
            lane < HEAD_DIM, out[:ATT_BLK], out[ATT_BLK:]).astype(BF16)


def _moba_attention(qa, ka, va):
    B, S, W = qa.shape
    assert S % ATT_BLK == 0 and S // ATT_BLK <= SUBLANES and W % LANES == 0
    spec = pl.BlockSpec((1, S, LANES), lambda b, j: (b, 0, j))
    return pl.pallas_call(
        _moba_kernel,
        grid=(B, W // LANES),
        in_specs=[spec, spec, spec],
        out_specs=spec,
        out_shape=jax.ShapeDtypeStruct((B, S, W), BF16),
        scratch_shapes=[
            pltpu.VMEM((S, 2 * LANES), BF16),
            pltpu.VMEM((LANES, LANES), F32),
            pltpu.VMEM((2, LANES, LANES), BF16),
            pltpu.VMEM((2, LANES, LANES), BF16),
        ] + _attention_scratch(S),
        compiler_params=_params(("arbitrary", "arbitrary")),
        name="moba_attention",
    )(qa, ka, va)


def _diff_kernel(lq1_ref, lk1_ref, lq2_ref, lk2_ref, g_ref, q_ref, k_ref, v_ref, o_ref, cbias_ref, s_ref, p_ref):
    S = k_ref.shape[1]
    nb = S // ATT_BLK
    lam = (jnp.exp(jnp.sum(lq1_ref[...] * lk1_ref[...], axis=1, keepdims=True))
           - jnp.exp(jnp.sum(lq2_ref[...] * lk2_ref[...], axis=1, keepdims=True)) + LAMBDA_INIT)
    cbias_ref[...] = _causal_bias()
    lane = lax.broadcasted_iota(I32, (ATT_BLK, LANES), 1)

    for i in reversed(range(nb)):
        r0 = i * ATT_BLK
        q = q_ref[0, r0:r0 + ATT_BLK, :].astype(F32)
        qs = jnp.concatenate([jnp.where(lane < HEAD_DIM, q, 0.0), jnp.where(lane >= HEAD_DIM, q, 0.0)],
                             axis=0).astype(BF16)
        out = _attend(qs, lambda n: k_ref[0, n * ATT_BLK:(n + 1) * ATT_BLK, :], lambda L: v_ref[0, :L, :],
                      s_ref, p_ref, cbias_ref, i)
        o = out[:ATT_BLK] - lam * out[ATT_BLK:]
        o_ref[0, r0:r0 + ATT_BLK, :] = (_rms(o, g_ref[...]) * (1.0 - LAMBDA_INIT)).astype(BF16)


def _diff_attention(qd, kd, vd, lq1, lk1, lq2, lk2, subln_g):
    B, S, W = qd.shape
    assert S % ATT_BLK == 0 and W % LANES == 0
    spec = pl.BlockSpec((1, S, LANES), lambda b, j: (b, 0, j))
    small = pl.BlockSpec((1, HEAD_DIM), lambda b, j: (0, 0))
    return pl.pallas_call(
        _diff_kernel,
        grid=(B, W // LANES),
        in_specs=[small, small, small, small, pl.BlockSpec((1, LANES), lambda b, j: (0, 0)), spec, spec, spec],
        out_specs=spec,
        out_shape=jax.ShapeDtypeStruct((B, S, W), BF16),
        scratch_shapes=_attention_scratch(S),
        compiler_params=_params(("arbitrary", "arbitrary")),
        name="diff_attention",
    )(lq1, lk1, lq2, lk2, subln_g, qd, kd, vd)


def _pack_bf16_pairs(x):
    n = x.shape[1] // 2
    lo = lax.bitcast_convert_type(x[:, :n].astype(BF16).astype(F32), U32)
    hi = lax.bitcast_convert_type(x[:, n:].astype(BF16).astype(F32), U32)
    return (lo >> 16) | hi


def _unpack_bf16_pairs(w, dtype=BF16):
    lo = lax.bitcast_convert_type(w << 16, F32)
    hi = lax.bitcast_convert_type(w & jnp.uint32(0xFFFF0000), F32)
    return jnp.concatenate([lo, hi], axis=1).astype(dtype)


def _merge_kernel(x_ref, oa_ref, od_ref, gmix_ref, wg_ref, wbm_ref, wbd_ref, wout_ref, gffn_ref,
                  wr_ref, br_ref,
                  x1_ref, h2_ref, route_ref, routet_ref, cnt_ref, base_ref, ltri_ref):
    tm, D = x_ref.shape

    @pl.when(pl.program_id(0) == 0)
    def _():
        base_ref[...] = jnp.zeros_like(base_ref)
        row = lax.broadcasted_iota(I32, (tm, tm), 0)
        col = lax.broadcasted_iota(I32, (tm, tm), 1)
        ltri_ref[...] = jnp.where(col < row, 1.0, 0.0).astype(BF16)

    x = x_ref[...]
    h = _rms(x, gmix_ref[...]).astype(BF16)
    sig = jax.nn.sigmoid(_dot(h, wg_ref[...]))
    merged = sig[:, :D] * _dot(oa_ref[...], wbm_ref[...]) + sig[:, D:] * _dot(od_ref[...], wbd_ref[...])
    x1 = x + _dot(merged.astype(BF16), wout_ref[...])
    x1_ref[...] = x1
    h2 = _rms(x1, gffn_ref[...])
    h2_ref[...] = _pack_bf16_pairs(h2)

    hhi = h2.astype(BF16)
    hlo = (h2 - hhi.astype(F32)).astype(BF16)
    lg_hi = _dot(hhi, wr_ref[...])
    lg_lo = _dot(hlo, wr_ref[...])
    lg = (lg_hi[:, :LANES] + lg_hi[:, LANES:]) + (lg_lo[:, :LANES] + lg_lo[:, LANES:]) + br_ref[...]
    lane = lax.broadcasted_iota(I32, (tm, LANES), 1)
    ninf = -jnp.inf

    def first_argmax(vals):
        m = jnp.max(vals, axis=1, keepdims=True)
        return m, jnp.min(jnp.where(vals == m, lane, LANES), axis=1, keepdims=True)

    is_group = lane < N_GROUPS
    gm, gsel = first_argmax(jnp.where(is_group, lg, ninf))
    p_group = 1.0 / jnp.sum(jnp.where(is_group, jnp.exp(lg - gm), 0.0), axis=1, keepdims=True)
    lo = N_GROUPS + gsel * EXPERTS_PER_GROUP
    el = jnp.where((lane >= lo) & (lane < lo + EXPERTS_PER_GROUP), lg, ninf)
    m1, i1 = first_argmax(el)
    m2, i2 = first_argmax(jnp.where(lane == i1, ninf, el))
    e2x = jnp.exp(m2 - m1)
    w1 = p_group / (1.0 + e2x)
    w2 = p_group * e2x / (1.0 + e2x)
    e1 = i1 - N_GROUPS
    e2 = i2 - N_GROUPS

    oh1 = jnp.where(lane == e1, 1.0, 0.0)
    oh2 = jnp.where(lane == e2, 1.0, 0.0)
    cc = _dot(ltri_ref[...], jnp.concatenate([oh1, oh2], axis=1).astype(BF16))
    c1, c2 = cc[:, :LANES], cc[:, LANES:]
    base = base_ref[...]
    tot1 = jnp.sum(oh1, axis=0, keepdims=True)
    tot2 = jnp.sum(oh2, axis=0, keepdims=True)
    r1 = jnp.sum(oh1 * (c1 + base), axis=1, keepdims=True)
    r2 = jnp.sum(oh2 * (c2 + base + tot1), axis=1, keepdims=True)
    base = base + tot1 + tot2
    base_ref[...] = base
    cnt_ref[...] = base

    cols = (e1.astype(F32), e2.astype(F32), w1, w2, r1, r2)
    route = jnp.zeros((tm, LANES), F32)
    for c, val in enumerate(cols):
        route = jnp.where(lane == c, val, route)
    route_ref[...] = route[:, :ROUTE_COLS]
    routet_ref[...] = route.T[:ROUTE_COLS, :]


def _merge_route(x2, o_a, o_d, g_mix, w_g, w_bm, w_bd, w_out, g_ffn, w_r2, b_r):
    T, D = x2.shape
    tm = min(MERGE_TILE, T)
    assert T % tm == 0
    W = o_a.shape[1]
    tile = lambda w: pl.BlockSpec((tm, w), lambda i: (i, 0))
    full = lambda a: pl.BlockSpec(a.shape, lambda i: (0, 0))
    return pl.pallas_call(
        _merge_kernel,
        grid=(T // tm,),
        in_specs=[tile(D), tile(W), tile(W), full(g_mix), full(w_g), full(w_bm), full(w_bd), full(w_out),
                  full(g_ffn), full(w_r2), full(b_r)],
        out_specs=[tile(D), tile(D // 2), tile(ROUTE_COLS), pl.BlockSpec((ROUTE_COLS, tm), lambda i: (0, i)),
                   pl.BlockSpec((1, LANES), lambda i: (0, 0))],
        out_shape=[jax.ShapeDtypeStruct((T, D), F32), jax.ShapeDtypeStruct((T, D // 2), U32),
                   jax.ShapeDtypeStruct((T, ROUTE_COLS), F32), jax.ShapeDtypeStruct((ROUTE_COLS, T), F32),
                   jax.ShapeDtypeStruct((1, LANES), F32)],
        scratch_shapes=[pltpu.VMEM((1, LANES), F32), pltpu.VMEM((tm, tm), BF16)],
        compiler_params=_params(("arbitrary",)),
        name="merge_route",
    )(x2, o_a, o_d, g_mix, w_g, w_bm, w_bd, w_out, g_ffn, w_r2, b_r)


def _index_prefetch(dest_hbm, idx_ref, isem):
    i = pl.program_id(0)
    per = dest_hbm.shape[1]
    slot = i % 2

    def copy(step, sl):
        return pltpu.make_async_copy(dest_hbm.at[step], idx_ref.at[pl.ds(sl * per, per)], isem.at[sl])

    @pl.when(i == 0)
    def _():
        copy(0, 0).start()

    @pl.when(i + 1 < pl.num_programs(0))
    def _():
        copy(i + 1, 1 - slot).start()

    copy(i, slot).wait()
    return slot * per


def _dispatch_kernel(zs_ref, nu_ref, dest_hbm, h2_ref, xd_hbm, idx_ref, zero_ref, isem, rsem, zsem):
    tm = h2_ref.shape[0] * 8
    nblk = xd_hbm.shape[0] // MOE_BLK
    i = pl.program_id(0)
    last = pl.num_programs(0) - 1

    @pl.when(i == 0)
    def _():
        zero_ref[...] = jnp.zeros_like(zero_ref)

        def zcopy(row):
            return pltpu.make_async_copy(zero_ref, xd_hbm.at[pl.ds(pl.multiple_of(row, MOE_BLK), MOE_BLK), :], zsem)

        def tail_start(blk, carry):
            zcopy(blk * MOE_BLK).start()
            return carry

        def tail_wait(blk, carry):
            zcopy(0).wait()
            return carry

        for e in range(N_EXPERTS):
            @pl.when(zs_ref[e] >= 0)
            def _():
                zcopy(zs_ref[e]).start()
        lax.fori_loop(nu_ref[0], nblk, tail_start, 0)
        for e in range(N_EXPERTS):
            @pl.when(zs_ref[e] >= 0)
            def _():
                zcopy(0).wait()
        lax.fori_loop(nu_ref[0], nblk, tail_wait, 0)

    base = _index_prefetch(dest_hbm, idx_ref, isem)
    for r in range(tm):
        for kk in range(EXPERT_TOPK):
            d = idx_ref[base + kk * tm + r]
            pltpu.make_async_copy(h2_ref.at[r // 8, pl.ds(r % 8, 1), :], xd_hbm.at[pl.ds(d, 1), :],
                                  rsem).start(priority=kk)
    for kk in range(EXPERT_TOPK):
        pltpu.make_async_copy(h2_ref, h2_ref, rsem).wait()


def _dispatch(zstart, n_used, dest2, h2p, n_rows):
    T, Wp = h2p.shape
    nt, per = dest2.shape
    tm = per // EXPERT_TOPK
    return pl.pallas_call(
        _dispatch_kernel,
        grid_spec=pltpu.PrefetchScalarGridSpec(
            num_scalar_prefetch=2,
            grid=(nt,),
            in_specs=[pl.BlockSpec(memory_space=pl.ANY),
                      pl.BlockSpec((tm // 8, 8, Wp), lambda i, zs, nu: (i, 0, 0))],
            out_specs=pl.BlockSpec(memory_space=pl.ANY),
            scratch_shapes=[pltpu.SMEM((2 * per,), I32), pltpu.VMEM((MOE_BLK, Wp), U32),
                            pltpu.SemaphoreType.DMA((2,)), pltpu.SemaphoreType.DMA(()),
                            pltpu.SemaphoreType.DMA(())],
        ),
        out_shape=jax.ShapeDtypeStruct((n_rows, Wp), U32),
        compiler_params=_params(("arbitrary",)),
        name="moe_dispatch",
    )(zstart, n_used, dest2, h2p.reshape(T // 8, 8, Wp))


def _expert_kernel(be_ref, nu_ref, xd_ref, w1_ref, w3_ref, w2_ref, yb_ref, w1b, w3b, w2b):
    i = pl.program_id(0)
    n_used = nu_ref[0]

    @pl.when((i < n_used) & ((i == 0) | (be_ref[i] != be_ref[jnp.maximum(i - 1, 0)])))
    def _():
        w1b[...] = w1_ref[0].astype(BF16)
        w3b[...] = w3_ref[0].astype(BF16)
        w2b[...] = w2_ref[0].astype(BF16)

    @pl.when(i < n_used)
    def _():
        x = _unpack_bf16_pairs(xd_ref[...])
        act = jax.nn.silu(_dot(x, w1b[...])) * _dot(x, w3b[...])
        yb_ref[...] = _pack_bf16_pairs(_dot(act.astype(BF16), w2b[...]))

    @pl.when(i >= n_used)
    def _():
        yb_ref[...] = jnp.zeros_like(yb_ref)


def _experts(blk_expert, n_used, xd, w1, w3, w2):
    P, Wp = xd.shape
    E, D, De = w1.shape
    nblk = P // MOE_BLK
    return pl.pallas_call(
        _expert_kernel,
        grid_spec=pltpu.PrefetchScalarGridSpec(
            num_scalar_prefetch=2,
            grid=(nblk,),
            in_specs=[
                pl.BlockSpec((MOE_BLK, Wp), lambda i, be, nu: (jnp.minimum(i, nu[0] - 1), 0)),
                pl.BlockSpec((1, D, De), lambda i, be, nu: (be[i], 0, 0)),
                pl.BlockSpec((1, D, De), lambda i, be, nu: (be[i], 0, 0)),
                pl.BlockSpec((1, De, D), lambda i, be, nu: (be[i], 0, 0)),
            ],
            out_specs=pl.BlockSpec((MOE_BLK, D // 2), lambda i, be, nu: (i, 0)),
            scratch_shapes=[pltpu.VMEM((D, De), BF16), pltpu.VMEM((D, De), BF16), pltpu.VMEM((De, D), BF16)],
        ),
        out_shape=jax.ShapeDtypeStruct((P, D // 2), U32),
        compiler_params=_params(("arbitrary",), EXPERT_VMEM_LIMIT),
        name="moe_experts",
    )(blk_expert, n_used, xd, w1, w3, w2)


def _combine_kernel(dest_hbm, route_ref, x1_ref, g_ref, yb_hbm, o_ref, idx_ref, buf_ref, isem, rsem):
    tm, D = x1_ref.shape
    per = EXPERT_TOPK * tm
    i = pl.program_id(0)
    last = pl.num_programs(0) - 1

    def idx_copy(step):
        sl = step % 3
        return pltpu.make_async_copy(dest_hbm.at[step], idx_ref.at[pl.ds(sl * per, per)], isem.at[sl])

    def issue(step, bs, j):
        ib = (step % 3) * per
        for u in range(8):
            for kk in range(EXPERT_TOPK):
                d = idx_ref[ib + kk * tm + 8 * j + u]
                pltpu.make_async_copy(yb_hbm.at[pl.ds(d, 1), :], buf_ref.at[bs, kk, j, pl.ds(u, 1), :],
                                      rsem.at[bs]).start(priority=kk)

    def drain(bs):
        for kk in range(EXPERT_TOPK):
            pltpu.make_async_copy(buf_ref.at[bs, kk], buf_ref.at[bs, kk], rsem.at[bs]).wait()

    @pl.when(i == 0)
    def _():
        idx_copy(0).start()

        @pl.when(last > 0)
        def _():
            idx_copy(1).start()

        idx_copy(0).wait()
        for j in range(tm // 8):
            issue(0, 0, j)

    @pl.when(i + 2 <= last)
    def _():
        idx_copy(i + 2).start()

    @pl.when(i < last)
    def _():
        idx_copy(i + 1).wait()

    cur = i % 2
    drain(cur)
    nxt = jnp.minimum(i + 1, last)
    for j in range(tm // 8):
        issue(nxt, 1 - cur, j)
        rows = slice(8 * j, 8 * j + 8)
        route = route_ref[rows, :]
        y0 = _unpack_bf16_pairs(buf_ref[cur, 0, j], F32)
        y1 = _unpack_bf16_pairs(buf_ref[cur, 1, j], F32)
        o_ref[rows, :] = _rms(x1_ref[rows, :] + (route[:, 2:3] * y0 + route[:, 3:4] * y1), g_ref[...])

    @pl.when(i == last)
    def _():
        drain(1 - cur)


def _combine(dest2, route, x1, g_final, yb):
    T, D = x1.shape
    nt, per = dest2.shape
    tm = per // EXPERT_TOPK
    return pl.pallas_call(
        _combine_kernel,
        grid=(nt,),
        in_specs=[pl.BlockSpec(memory_space=pl.ANY), pl.BlockSpec((tm, ROUTE_COLS), lambda i: (i, 0)),
                  pl.BlockSpec((tm, D), lambda i: (i, 0)), pl.BlockSpec((1, D), lambda i: (0, 0)),
                  pl.BlockSpec(memory_space=pl.ANY)],
        out_specs=pl.BlockSpec((tm, D), lambda i: (i, 0)),
        out_shape=jax.ShapeDtypeStruct((T, D), F32),
        scratch_shapes=[pltpu.SMEM((3 * per,), I32), pltpu.VMEM((2, EXPERT_TOPK, tm // 8, 8, D // 2), U32),
                        pltpu.SemaphoreType.DMA((3,)), pltpu.SemaphoreType.DMA((2,))],
        compiler_params=_params(("arbitrary",)),
        name="moe_combine",
    )(dest2, route, x1, g_final, yb)


def _moe(h2p, route, route_t, counts, x1, g_final, w1, w3, w2):
    T = h2p.shape[0]
    A = T * EXPERT_TOPK
    P = -(-A // MOE_BLK) * MOE_BLK + N_EXPERTS * MOE_BLK
    nblk = P // MOE_BLK
    cnt = counts[0, :N_EXPERTS].astype(I32)
    padded = (cnt + MOE_BLK - 1) // MOE_BLK * MOE_BLK
    pends = jnp.cumsum(padded)
    pstarts = pends - padded
    ids = route_t[0:EXPERT_TOPK].astype(I32)
    ranks = route_t[4:4 + EXPERT_TOPK].astype(I32)
    dest = ranks
    for e in range(N_EXPERTS):
        dest = dest + jnp.where(ids == e, pstarts[e], 0)

    def tiled(tm):
        tm = min(tm, T)
        return dest.reshape(EXPERT_TOPK, T // tm, tm).transpose(1, 0, 2).reshape(T // tm, EXPERT_TOPK * tm)

    blk_row = jnp.arange(nblk, dtype=I32) * MOE_BLK
    blk_expert = jnp.minimum(jnp.sum((pends[None, :] <= blk_row[:, None]).astype(I32), axis=1), N_EXPERTS - 1)
    n_used = (pends[-1:] // MOE_BLK).astype(I32)
    zstart = jnp.where(padded > 0, pends - MOE_BLK, -1).astype(I32)
    xd = _dispatch(zstart, n_used, tiled(DISPATCH_TILE), h2p, P)
    yb = _experts(blk_expert, n_used, xd, w1, w3, w2)
    return _combine(tiled(COMBINE_TILE), route, x1, g_final, yb)


def _rope_tables(seq):
    inv = 1.0 / (ROPE_THETA ** (jnp.arange(0, HEAD_DIM, 2, dtype=F32) / HEAD_DIM))
    ang = jnp.arange(seq, dtype=F32)[:, None] * inv[None, :]
    ang = jnp.concatenate([ang, ang], axis=-1)
    sign = jnp.concatenate([-jnp.ones((HEAD_DIM // 2,), F32), jnp.ones((HEAD_DIM // 2,), F32)])
    reps = LANES // HEAD_DIM
    return jnp.tile(jnp.cos(ang), (1, reps)), jnp.tile(jnp.sin(ang) * sign, (1, reps))


def kernel(x, g_mix, w_in, w_branch_moba, w_branch_diff, w_out, diff_lambda_q1, diff_lambda_k1, diff_lambda_q2, diff_lambda_k2, diff_subln_g, g_ffn, w_group, b_group, w_router, b_router, w_expert_gate, w_expert_up, w_expert_down, g_final):
    B, S, D = x.shape
    assert w_in.shape[0] == 1, "one layer"
    T = B * S
    x2 = x.reshape(T, D)
    n_qkv = w_in.shape[2] - 2 * D
    w_qkv = w_in[0, :, :n_qkv].astype(BF16)
    w_g = w_in[0, :, n_qkv:].astype(BF16)
    cos, sin = _rope_tables(S)

    qa, ka, va, qd, kd, vd = _qkv_proj(x2, g_mix, w_qkv, cos, sin, S)
    W = qa.shape[1]
    o_a = _moba_attention(qa.reshape(B, S, W), ka.reshape(B, S, W), va.reshape(B, S, W)).reshape(T, W)
    o_d = _diff_attention(qd.reshape(B, S, W), kd.reshape(B, S, W), vd.reshape(B, S, W),
                          diff_lambda_q1, diff_lambda_k1, diff_lambda_q2, diff_lambda_k2, diff_subln_g).reshape(T, W)

    w_r = jnp.concatenate([w_group[0], w_router[0]], axis=1)
    w_r = jnp.pad(w_r, ((0, 0), (0, LANES - w_r.shape[1])))
    wr_hi = w_r.astype(BF16)
    w_r2 = jnp.concatenate([wr_hi, (w_r - wr_hi.astype(F32)).astype(BF16)], axis=1)
    b_r = jnp.pad(jnp.concatenate([b_group[0], b_router[0]])[None, :], ((0, 0), (0, LANES - N_GROUPS - N_EXPERTS)))

    x1, h2p, route, route_t, counts = _merge_route(
        x2, o_a, o_d, g_mix, w_g, w_branch_moba[0].astype(BF16), w_branch_diff[0].astype(BF16),
        w_out[0].astype(BF16), g_ffn, w_r2, b_r)

    out = _moe(h2p, route, route_t, counts, x1, g_final[None, :],
               w_expert_gate[0], w_expert_up[0], w_expert_down[0])
    return out.reshape(B, S, D)
```

```python
import functools

import jax
import jax.numpy as jnp
from jax import lax
from jax.experimental import pallas as pl
from jax.experimental.pallas import tpu as pltpu

F32 = jnp.float32
BF16 = jnp.bfloat16
I32 = jnp.int32
U32 = jnp.uint32

LANES = 128
SUBLANES = 8
HEAD_DIM = 64
ATT_BLK = 256
ATT_BLK_SHIFT = 8
MOBA_TOPK = 3
N_GROUPS = 4
EXPERTS_PER_GROUP = 8
N_EXPERTS = N_GROUPS * EXPERTS_PER_GROUP
EXPERT_TOPK = 2
MOE_BLK = 512
DISPATCH_TILE = 512
COMBINE_TILE = 256
MERGE_TILE = 512
QKV_TILE = 512
ROUTE_COLS = 8
EPS = 1e-6
NEG = -1e30
ROPE_THETA = 10000.0
LAMBDA_INIT = 0.8 - 0.6 * 1.0
ATT_SCALE = HEAD_DIM ** -0.5
VMEM_LIMIT = 48 * 1024 * 1024
EXPERT_VMEM_LIMIT = 56 * 1024 * 1024


def _dot(a, b):
    return jnp.dot(a, b, preferred_element_type=F32)


def _dot_nt(a, b):
    return lax.dot_general(a, b, (((1,), (1,)), ((), ())), preferred_element_type=F32)


def _rms(x, g):
    var = jnp.mean(x * x, axis=-1, keepdims=True)
    return (x * lax.rsqrt(var + EPS)) * g


def _params(sem, vmem=VMEM_LIMIT):
    return pltpu.CompilerParams(dimension_semantics=sem, vmem_limit_bytes=vmem)


def _qkv_kernel(x_ref, g_ref, w_ref, cos_ref, sin_ref, qa_ref, ka_ref, va_ref, qd_ref, kd_ref, vd_ref):
    h = _rms(x_ref[...], g_ref[...]).astype(BF16)
    cos = cos_ref[...]
    sin = sin_ref[...]
    lane = lax.broadcasted_iota(I32, cos.shape, 1)
    first = (lane & (HEAD_DIM - 1)) < HEAD_DIM // 2
    width = qa_ref.shape[1]
    outs = ((qa_ref, True, ATT_SCALE), (ka_ref, True, 1.0), (va_ref, False, 1.0),
            (qd_ref, True, ATT_SCALE), (kd_ref, True, 1.0), (vd_ref, False, 1.0))
    for j, (o_ref, rope, scale) in enumerate(outs):
        y = _dot(h, w_ref[:, j * width:(j + 1) * width])
        for c in range(width // LANES):
            yc = y[:, c * LANES:(c + 1) * LANES]
            if rope:
                rot = jnp.where(first, pltpu.roll(yc, LANES - HEAD_DIM // 2, 1), pltpu.roll(yc, HEAD_DIM // 2, 1))
                yc = yc * cos + rot * sin
            if scale != 1.0:
                yc = yc * scale
            o_ref[:, c * LANES:(c + 1) * LANES] = yc.astype(BF16)


def _qkv_proj(x2, g_mix, w_qkv, cos, sin, seq):
    T, D = x2.shape
    width = w_qkv.shape[1] // 6
    tm = min(QKV_TILE, seq)
    assert seq % tm == 0 and T % tm == 0
    spt = seq // tm
    out = jax.ShapeDtypeStruct((T, width), BF16)
    return pl.pallas_call(
        _qkv_kernel,
        grid=(T // tm,),
        in_specs=[
            pl.BlockSpec((tm, D), lambda i: (i, 0)),
            pl.BlockSpec((1, D), lambda i: (0, 0)),
            pl.BlockSpec(w_qkv.shape, lambda i: (0, 0)),
            pl.BlockSpec((tm, LANES), lambda i: (i % spt, 0)),
            pl.BlockSpec((tm, LANES), lambda i: (i % spt, 0)),
        ],
        out_specs=[pl.BlockSpec((tm, width), lambda i: (i, 0))] * 6,
        out_shape=[out] * 6,
        compiler_params=_params(("arbitrary",)),
        name="qkv_proj",
    )(x2, g_mix, w_qkv, cos, sin)


ATT_ROWS = 2 * ATT_BLK


def _attend(qs, k_block, v_rows, s_ref, p_ref, cbias_ref, i):
    mx = None
    for n in range(i + 1):
        sb = _dot_nt(qs, k_block(n))
        if n == i:
            sb = sb + cbias_ref[...]
        s_ref[:, n * ATT_BLK:(n + 1) * ATT_BLK] = sb
        for c in range(ATT_BLK // LANES):
            part = sb[:, c * LANES:(c + 1) * LANES]
            mx = part if mx is None else jnp.maximum(mx, part)
    mb = jnp.broadcast_to(jnp.max(mx, axis=1, keepdims=True), (ATT_ROWS, LANES))
    ps = None
    for c in range((i + 1) * ATT_BLK // LANES):
        p = jnp.exp(s_ref[:, c * LANES:(c + 1) * LANES] - mb)
        ps = p if ps is None else ps + p
        p_ref[:, c * LANES:(c + 1) * LANES] = p.astype(BF16)
    l = jnp.sum(ps, axis=1, keepdims=True)
    L = (i + 1) * ATT_BLK
    return _dot(p_ref[:, :L], v_rows(L)) * (1.0 / l)


def _causal_bias():
    row = lax.broadcasted_iota(I32, (ATT_ROWS, ATT_BLK), 0) & (ATT_BLK - 1)
    col = lax.broadcasted_iota(I32, (ATT_ROWS, ATT_BLK), 1)
    return jnp.where(col <= row, 0.0, NEG).astype(F32)


def _attention_scratch(seq):
    return [pltpu.VMEM((ATT_ROWS, ATT_BLK), F32), pltpu.VMEM((ATT_ROWS, seq), F32), pltpu.VMEM((ATT_ROWS, seq), BF16)]


def _moba_kernel(q_ref, k_ref, v_ref, o_ref, kaug_ref, kmf_ref, kmhi_ref, kmlo_ref, cbias_ref, s_ref, p_ref):
    S = k_ref.shape[1]
    nb = S // ATT_BLK
    k = k_ref[0]
    lane_s = lax.broadcasted_iota(I32, (S, LANES), 1)
    blk_s = lax.broadcasted_iota(I32, (S, LANES), 0) >> ATT_BLK_SHIFT
    kaug_ref[:, :LANES] = k
    kaug_ref[:, LANES:] = jnp.where(lane_s == blk_s, 1.0, 0.0).astype(BF16)
    kmean = jnp.mean(k.astype(F32).reshape(nb, ATT_BLK, LANES), axis=1)
    lane_n = lax.broadcasted_iota(I32, (nb, LANES), 1)
    for hh in range(2):
        own_n = (lane_n < HEAD_DIM) if hh == 0 else (lane_n >= HEAD_DIM)
        kmf_ref[...] = jnp.zeros((LANES, LANES), F32)
        kmf_ref[0:nb, :] = jnp.where(own_n, kmean, 0.0)
        km = kmf_ref[...]
        hi = km.astype(BF16)
        kmhi_ref[hh] = hi
        kmlo_ref[hh] = (km - hi.astype(F32)).astype(BF16)
    cbias_ref[...] = _causal_bias()

    lane = lax.broadcasted_iota(I32, (ATT_BLK, LANES), 1)

    def stacked_queries(i):
        q = q_ref[0, i * ATT_BLK:(i + 1) * ATT_BLK, :].astype(F32)
        rows = []
        for hh in range(2):
            own = (lane < HEAD_DIM) if hh == 0 else (lane >= HEAD_DIM)
            qh = jnp.where(own, q, 0.0)
            if i <= MOBA_TOPK:
                rows.append(qh)
                continue
            qhb = qh.astype(BF16)
            gate = (_dot_nt(kmhi_ref[hh], qhb) + _dot_nt(kmlo_ref[hh], qhb))[:SUBLANES, :]
            blk = lax.broadcasted_iota(I32, (SUBLANES, ATT_BLK), 0)
            cand = blk < i
            g = jnp.where(cand, gate, -jnp.inf)
            rank = jnp.zeros((SUBLANES, ATT_BLK), I32)
            for r in range(1, i):
                rank = rank + jnp.where((blk >= r) & (pltpu.roll(g, r, 0) >= g), 1, 0)
                rank = rank + jnp.where((blk < SUBLANES - r) & (pltpu.roll(g, SUBLANES - r, 0) > g), 1, 0)
            drop_t = jnp.where(cand & (rank >= MOBA_TOPK), NEG, 0.0)
            drop = jnp.concatenate([drop_t, jnp.zeros((LANES - SUBLANES, ATT_BLK), F32)], axis=0).T
            rows.append(jnp.concatenate([qh, drop], axis=1))
        return jnp.concatenate(rows, axis=0).astype(BF16)

    for i in reversed(range(nb)):
        if i > MOBA_TOPK:
            k_block = lambda n: kaug_ref[n * ATT_BLK:(n + 1) * ATT_BLK, :]
        else:
            k_block = lambda n: k_ref[0, n * ATT_BLK:(n + 1) * ATT_BLK, :]
        out = _attend(stacked_queries(i), k_block, lambda L: v_ref[0, :L, :], s_ref, p_ref, cbias_ref, i)
        o_ref[0, i * ATT_BLK:(i + 1) * ATT_BLK, :] = jnp.where(
            lane < HEAD_DIM, out[:ATT_BLK], out[ATT_BLK:]).astype(BF16)


def _moba_attention(qa, ka, va):
    B, S, W = qa.shape
    assert S % ATT_BLK == 0 and S // ATT_BLK <= SUBLANES and W % LANES == 0
    spec = pl.BlockSpec((1, S, LANES), lambda b, j: (b, 0, j))
    return pl.pallas_call(
        _moba_kernel,
        grid=(B, W // LANES),
        in_specs=[spec, spec, spec],
        out_specs=spec,
        out_shape=jax.ShapeDtypeStruct((B, S, W), BF16),
        scratch_shapes=[
            pltpu.VMEM((S, 2 * LANES), BF16),
            pltpu.VMEM((LANES, LANES), F32),
            pltpu.VMEM((2, LANES, LANES), BF16),
            pltpu.VMEM((2, LANES, LANES), BF16),
        ] + _attention_scratch(S),
        compiler_params=_params(("arbitrary", "arbitrary")),
        name="moba_attention",
    )(qa, ka, va)


def _diff_kernel(lq1_ref, lk1_ref, lq2_ref, lk2_ref, g_ref, q_ref, k_ref, v_ref, o_ref, cbias_ref, s_ref, p_ref):
    S = k_ref.shape[1]
    nb = S // ATT_BLK
    lam = (jnp.exp(jnp.sum(lq1_ref[...] * lk1_ref[...], axis=1, keepdims=True))
           - jnp.exp(jnp.sum(lq2_ref[...] * lk2_ref[...], axis=1, keepdims=True)) + LAMBDA_INIT)
    cbias_ref[...] = _causal_bias()
    lane = lax.broadcasted_iota(I32, (ATT_BLK, LANES), 1)

    for i in reversed(range(nb)):
        r0 = i * ATT_BLK
        q = q_ref[0, r0:r0 + ATT_BLK, :].astype(F32)
        qs = jnp.concatenate([jnp.where(lane < HEAD_DIM, q, 0.0), jnp.where(lane >= HEAD_DIM, q, 0.0)],
                             axis=0).astype(BF16)
        out = _attend(qs, lambda n: k_ref[0, n * ATT_BLK:(n + 1) * ATT_BLK, :], lambda L: v_ref[0, :L, :],
                      s_ref, p_ref, cbias_ref, i)
        o = out[:ATT_BLK] - lam * out[ATT_BLK:]
        o_ref[0, r0:r0 + ATT_BLK, :] = (_rms(o, g_ref[...]) * (1.0 - LAMBDA_INIT)).astype(BF16)


def _diff_attention(qd, kd, vd, lq1, lk1, lq2, lk2, subln_g):
    B, S, W = qd.shape
    assert S % ATT_BLK == 0 and W % LANES == 0
    spec = pl.BlockSpec((1, S, LANES), lambda b, j: (b, 0, j))
    small = pl.BlockSpec((1, HEAD_DIM), lambda b, j: (0, 0))
    return pl.pallas_call(
        _diff_kernel,
        grid=(B, W // LANES),
        in_specs=[small, small, small, small, pl.BlockSpec((1, LANES), lambda b, j: (0, 0)), spec, spec, spec],
        out_specs=spec,
        out_shape=jax.ShapeDtypeStruct((B, S, W), BF16),
        scratch_shapes=_attention_scratch(S),
        compiler_params=_params(("arbitrary", "arbitrary")),
        name="diff_attention",
    )(lq1, lk1, lq2, lk2, subln_g, qd, kd, vd)


def _pack_bf16_pairs(x):
    n = x.shape[1] // 2
    lo = lax.bitcast_convert_type(x[:, :n].astype(BF16).astype(F32), U32)
    hi = lax.bitcast_convert_type(x[:, n:].astype(BF16).astype(F32), U32)
    return (lo >> 16) | hi


def _unpack_bf16_pairs(w, dtype=BF16):
    lo = lax.bitcast_convert_type(w << 16, F32)
    hi = lax.bitcast_convert_type(w & jnp.uint32(0xFFFF0000), F32)
    return jnp.concatenate([lo, hi], axis=1).astype(dtype)


def _merge_kernel(x_ref, oa_ref, od_ref, gmix_ref, wg_ref, wbm_ref, wbd_ref, wout_ref, gffn_ref,
                  wr_ref, br_ref,
                  x1_ref, h2_ref, route_ref, routet_ref, cnt_ref, base_ref, ltri_ref):
    tm, D = x_ref.shape

    @pl.when(pl.program_id(0) == 0)
    def _():
        base_ref[...] = jnp.zeros_like(base_ref)
        row = lax.broadcasted_iota(I32, (tm, tm), 0)
        col = lax.broadcasted_iota(I32, (tm, tm), 1)
        ltri_ref[...] = jnp.where(col < row, 1.0, 0.0).astype(BF16)

    x = x_ref[...]
    h = _rms(x, gmix_ref[...]).astype(BF16)
    sig = jax.nn.sigmoid(_dot(h, wg_ref[...]))
    merged = sig[:, :D] * _dot(oa_ref[...], wbm_ref[...]) + sig[:, D:] * _dot(od_ref[...], wbd_ref[...])
    x1 = x + _dot(merged.astype(BF16), wout_ref[...])
    x1_ref[...] = x1
    h2 = _rms(x1, gffn_ref[...])
    h2_ref[...] = _pack_bf16_pairs(h2)

    hhi = h2.astype(BF16)
    hlo = (h2 - hhi.astype(F32)).astype(BF16)
    lg_hi = _dot(hhi, wr_ref[...])
    lg_lo = _dot(hlo, wr_ref[...])
    lg = (lg_hi[:, :LANES] + lg_hi[:, LANES:]) + (lg_lo[:, :LANES] + lg_lo[:, LANES:]) + br_ref[...]
    lane = lax.broadcasted_iota(I32, (tm, LANES), 1)
    ninf = -jnp.inf

    def first_argmax(vals):
        m = jnp.max(vals, axis=1, keepdims=True)
        return m, jnp.min(jnp.where(vals == m, lane, LANES), axis=1, keepdims=True)

    is_group = lane < N_GROUPS
    gm, gsel = first_argmax(jnp.where(is_group, lg, ninf))
    p_group = 1.0 / jnp.sum(jnp.where(is_group, jnp.exp(lg - gm), 0.0), axis=1, keepdims=True)
    lo = N_GROUPS + gsel * EXPERTS_PER_GROUP
    el = jnp.where((lane >= lo) & (lane < lo + EXPERTS_PER_GROUP), lg, ninf)
    m1, i1 = first_argmax(el)
    m2, i2 = first_argmax(jnp.where(lane == i1, ninf, el))
    e2x = jnp.exp(m2 - m1)
    w1 = p_group / (1.0 + e2x)
    w2 = p_group * e2x / (1.0 + e2x)
    e1 = i1 - N_GROUPS
    e2 = i2 - N_GROUPS

    oh1 = jnp.where(lane == e1, 1.0, 0.0)
    oh2 = jnp.where(lane == e2, 1.0, 0.0)
    cc = _dot(ltri_ref[...], jnp.concatenate([oh1, oh2], axis=1).astype(BF16))
    c1, c2 = cc[:, :LANES], cc[:, LANES:]
    base = base_ref[...]
    tot1 = jnp.sum(oh1, axis=0, keepdims=True)
    tot2 = jnp.sum(oh2, axis=0, keepdims=True)
    r1 = jnp.sum(oh1 * (c1 + base), axis=1, keepdims=True)
    r2 = jnp.sum(oh2 * (c2 + base + tot1), axis=1, keepdims=True)
    base = base + tot1 + tot2
    base_ref[...] = base
    cnt_ref[...] = base

    cols = (e1.astype(F32), e2.astype(F32), w1, w2, r1, r2)
    route = jnp.zeros((tm, LANES), F32)
    for c, val in enumerate(cols):
        route = jnp.where(lane == c, val, route)
    route_ref[...] = route[:, :ROUTE_COLS]
    routet_ref[...] = route.T[:ROUTE_COLS, :]


def _merge_route(x2, o_a, o_d, g_mix, w_g, w_bm, w_bd, w_out, g_ffn, w_r2, b_r):
    T, D = x2.shape
    tm = min(MERGE_TILE, T)
    assert T % tm == 0
    W = o_a.shape[1]
    tile = lambda w: pl.BlockSpec((tm, w), lambda i: (i, 0))
    full = lambda a: pl.BlockSpec(a.shape, lambda i: (0, 0))
    return pl.pallas_call(
        _merge_kernel,
        grid=(T // tm,),
        in_specs=[tile(D), tile(W), tile(W), full(g_mix), full(w_g), full(w_bm), full(w_bd), full(w_out),
                  full(g_ffn), full(w_r2), full(b_r)],
        out_specs=[tile(D), tile(D // 2), tile(ROUTE_COLS), pl.BlockSpec((ROUTE_COLS, tm), lambda i: (0, i)),
                   pl.BlockSpec((1, LANES), lambda i: (0, 0))],
        out_shape=[jax.ShapeDtypeStruct((T, D), F32), jax.ShapeDtypeStruct((T, D // 2), U32),
                   jax.ShapeDtypeStruct((T, ROUTE_COLS), F32), jax.ShapeDtypeStruct((ROUTE_COLS, T), F32),
                   jax.ShapeDtypeStruct((1, LANES), F32)],
        scratch_shapes=[pltpu.VMEM((1, LANES), F32), pltpu.VMEM((tm, tm), BF16)],
        compiler_params=_params(("arbitrary",)),
        name="merge_route",
    )(x2, o_a, o_d, g_mix, w_g, w_bm, w_bd, w_out, g_ffn, w_r2, b_r)


def _index_prefetch(dest_hbm, idx_ref, isem):
    i = pl.program_id(0)
    per = dest_hbm.shape[1]
    slot = i % 2

    def copy(step, sl):
        return pltpu.make_async_copy(dest_hbm.at[step], idx_ref.at[pl.ds(sl * per, per)], isem.at[sl])

    @pl.when(i == 0)
    def _():
        copy(0, 0).start()

    @pl.when(i + 1 < pl.num_programs(0))
    def _():
        copy(i + 1, 1 - slot).start()

    copy(i, slot).wait()
    return slot * per


def _dispatch_kernel(zs_ref, nu_ref, dest_hbm, h2_hbm, xd_hbm, idx_ref, zero_ref, src_ref, isem, ssem, rsem, zsem):
    tm = h2_hbm.shape[1] * 8
    nblk = xd_hbm.shape[0] // MOE_BLK
    i = pl.program_id(0)
    last = pl.num_programs(0) - 1

    @pl.when(i == 0)
    def _():
        zero_ref[...] = jnp.zeros_like(zero_ref)

        def zcopy(row):
            return pltpu.make_async_copy(zero_ref, xd_hbm.at[pl.ds(pl.multiple_of(row, MOE_BLK), MOE_BLK), :], zsem)

        def tail_start(blk, carry):
            zcopy(blk * MOE_BLK).start()
            return carry

        def tail_wait(blk, carry):
            zcopy(0).wait()
            return carry

        for e in range(N_EXPERTS):
            @pl.when(zs_ref[e] >= 0)
            def _():
                zcopy(zs_ref[e]).start()
        lax.fori_loop(nu_ref[0], nblk, tail_start, 0)
        for e in range(N_EXPERTS):
            @pl.when(zs_ref[e] >= 0)
            def _():
                zcopy(0).wait()
        lax.fori_loop(nu_ref[0], nblk, tail_wait, 0)

    def src_copy(step):
        return pltpu.make_async_copy(h2_hbm.at[step], src_ref.at[step % 3], ssem.at[step % 3])

    @pl.when(i == 0)
    def _():
        src_copy(0).start()

    @pl.when(i < last)
    def _():
        src_copy(i + 1).start()

    base = _index_prefetch(dest_hbm, idx_ref, isem)
    src_copy(i).wait()
    for slot in range(3):
        @pl.when(i % 3 == slot)
        def _():
            for r in range(tm):
                for kk in range(EXPERT_TOPK):
                    d = idx_ref[base + kk * tm + r]
                    pltpu.make_async_copy(src_ref.at[slot, r // 8, pl.ds(r % 8, 1), :],
                                          xd_hbm.at[pl.ds(d, 1), :], rsem.at[i % 2]).start(priority=kk)

    def drain(sem_slot):
        for kk in range(EXPERT_TOPK):
            pltpu.make_async_copy(src_ref.at[0], src_ref.at[0], rsem.at[sem_slot]).wait()

    @pl.when(i > 0)
    def _():
        drain((i - 1) % 2)

    @pl.when(i == last)
    def _():
        drain(i % 2)


def _dispatch(zstart, n_used, dest2, h2p, n_rows):
    T, Wp = h2p.shape
    nt, per = dest2.shape
    tm = per // EXPERT_TOPK
    return pl.pallas_call(
        _dispatch_kernel,
        grid_spec=pltpu.PrefetchScalarGridSpec(
            num_scalar_prefetch=2,
            grid=(nt,),
            in_specs=[pl.BlockSpec(memory_space=pl.ANY), pl.BlockSpec(memory_space=pl.ANY)],
            out_specs=pl.BlockSpec(memory_space=pl.ANY),
            scratch_shapes=[pltpu.SMEM((2 * per,), I32), pltpu.VMEM((MOE_BLK, Wp), U32),
                            pltpu.VMEM((3, tm // 8, 8, Wp), U32),
                            pltpu.SemaphoreType.DMA((2,)), pltpu.SemaphoreType.DMA((3,)),
                            pltpu.SemaphoreType.DMA((2,)), pltpu.SemaphoreType.DMA(())],
        ),
        out_shape=jax.ShapeDtypeStruct((n_rows, Wp), U32),
        compiler_params=_params(("arbitrary",)),
        name="moe_dispatch",
    )(zstart, n_used, dest2, h2p.reshape(nt, tm // 8, 8, Wp))


def _expert_kernel(b0_ref, nb_ref, nu_ref, xd_hbm, w1_ref, w3_ref, w2_ref, yb_hbm, w1b, w3b, w2b, xbuf, ybuf, isem, osem):
    e = pl.program_id(0)
    first = b0_ref[e]
    count = nb_ref[e]

    def x_copy(j, slot):
        rows = pl.ds(pl.multiple_of((first + j) * MOE_BLK, MOE_BLK), MOE_BLK)
        return pltpu.make_async_copy(xd_hbm.at[rows, :], xbuf.at[slot], isem.at[slot])

    def y_copy(j, slot):
        rows = pl.ds(pl.multiple_of((first + j) * MOE_BLK, MOE_BLK), MOE_BLK)
        return pltpu.make_async_copy(ybuf.at[slot], yb_hbm.at[rows, :], osem.at[slot])

    @pl.when(count > 0)
    def _():
        x_copy(0, 0).start()
        w1b[...] = w1_ref[0].astype(BF16)
        w3b[...] = w3_ref[0].astype(BF16)
        w2b[...] = w2_ref[0].astype(BF16)

    def block(j, carry):
        slot = j % 2

        @pl.when(j + 1 < count)
        def _():
            x_copy(j + 1, 1 - slot).start()

        x_copy(j, slot).wait()

        @pl.when(j >= 2)
        def _():
            y_copy(j - 2, slot).wait()

        x = _unpack_bf16_pairs(xbuf[slot])
        act = jax.nn.silu(_dot(x, w1b[...])) * _dot(x, w3b[...])
        ybuf[slot] = _pack_bf16_pairs(_dot(act.astype(BF16), w2b[...]))
        y_copy(j, slot).start()
        return carry

    lax.fori_loop(0, count, block, 0)

    @pl.when(count >= 2)
    def _():
        y_copy(count - 2, count % 2).wait()

    @pl.when(count >= 1)
    def _():
        y_copy(count - 1, (count - 1) % 2).wait()

    @pl.when(e == pl.num_programs(0) - 1)
    def _():
        ybuf[0] = jnp.zeros(ybuf.shape[1:], U32)

        def zero_block(blk, carry):
            rows = pl.ds(pl.multiple_of(blk * MOE_BLK, MOE_BLK), MOE_BLK)
            tail = pltpu.make_async_copy(ybuf.at[0], yb_hbm.at[rows, :], osem.at[0])
            tail.start()
            tail.wait()
            return carry

        lax.fori_loop(nu_ref[0], yb_hbm.shape[0] // MOE_BLK, zero_block, 0)


def _experts(blk_first, blk_count, n_used, xd, w1, w3, w2):
    P, Wp = xd.shape
    E, D, De = w1.shape
    weight = lambda shape: pl.BlockSpec(shape, lambda e, b0, nb, nu: (e, 0, 0))
    return pl.pallas_call(
        _expert_kernel,
        grid_spec=pltpu.PrefetchScalarGridSpec(
            num_scalar_prefetch=3,
            grid=(E,),
            in_specs=[pl.BlockSpec(memory_space=pl.ANY), weight((1, D, De)), weight((1, D, De)), weight((1, De, D))],
            out_specs=pl.BlockSpec(memory_space=pl.ANY),
            scratch_shapes=[pltpu.VMEM((D, De), BF16), pltpu.VMEM((D, De), BF16), pltpu.VMEM((De, D), BF16),
                            pltpu.VMEM((2, MOE_BLK, Wp), U32), pltpu.VMEM((2, MOE_BLK, D // 2), U32),
                            pltpu.SemaphoreType.DMA((2,)), pltpu.SemaphoreType.DMA((2,))],
        ),
        out_shape=jax.ShapeDtypeStruct((P, D // 2), U32),
        compiler_params=_params(("arbitrary",), EXPERT_VMEM_LIMIT),
        name="moe_experts",
    )(blk_first, blk_count, n_used, xd, w1, w3, w2)


def _combine_kernel(dest_hbm, route_ref, x1_ref, g_ref, yb_hbm, o_ref, idx_ref, buf_ref, isem, rsem):
    tm, D = x1_ref.shape
    per = EXPERT_TOPK * tm
    i = pl.program_id(0)
    last = pl.num_programs(0) - 1

    def idx_copy(step):
        sl = step % 3
        return pltpu.make_async_copy(dest_hbm.at[step], idx_ref.at[pl.ds(sl * per, per)], isem.at[sl])

    def issue(step, bs, j):
        ib = (step % 3) * per
        for u in range(8):
            for kk in range(EXPERT_TOPK):
                d = idx_ref[ib + kk * tm + 8 * j + u]
                pltpu.make_async_copy(yb_hbm.at[pl.ds(d, 1), :], buf_ref.at[bs, kk, j, pl.ds(u, 1), :],
                                      rsem.at[bs]).start(priority=kk)

    def drain(bs):
        for kk in range(EXPERT_TOPK):
            pltpu.make_async_copy(buf_ref.at[bs, kk], buf_ref.at[bs, kk], rsem.at[bs]).wait()

    @pl.when(i == 0)
    def _():
        idx_copy(0).start()

        @pl.when(last > 0)
        def _():
            idx_copy(1).start()

        idx_copy(0).wait()
        for j in range(tm // 8):
            issue(0, 0, j)

    @pl.when(i + 2 <= last)
    def _():
        idx_copy(i + 2).start()

    @pl.when(i < last)
    def _():
        idx_copy(i + 1).wait()

    cur = i % 2
    drain(cur)
    nxt = jnp.minimum(i + 1, last)
    for j in range(tm // 8):
        issue(nxt, 1 - cur, j)
        rows = slice(8 * j, 8 * j + 8)
        route = route_ref[rows, :]
        y0 = _unpack_bf16_pairs(buf_ref[cur, 0, j], F32)
        y1 = _unpack_bf16_pairs(buf_ref[cur, 1, j], F32)
        o_ref[rows, :] = _rms(x1_ref[rows, :] + (route[:, 2:3] * y0 + route[:, 3:4] * y1), g_ref[...])

    @pl.when(i == last)
    def _():
        drain(1 - cur)


def _combine(dest2, route, x1, g_final, yb):
    T, D = x1.shape
    nt, per = dest2.shape
    tm = per // EXPERT_TOPK
    return pl.pallas_call(
        _combine_kernel,
        grid=(nt,),
        in_specs=[pl.BlockSpec(memory_space=pl.ANY), pl.BlockSpec((tm, ROUTE_COLS), lambda i: (i, 0)),
                  pl.BlockSpec((tm, D), lambda i: (i, 0)), pl.BlockSpec((1, D), lambda i: (0, 0)),
                  pl.BlockSpec(memory_space=pl.ANY)],
        out_specs=pl.BlockSpec((tm, D), lambda i: (i, 0)),
        out_shape=jax.ShapeDtypeStruct((T, D), F32),
        scratch_shapes=[pltpu.SMEM((3 * per,), I32), pltpu.VMEM((2, EXPERT_TOPK, tm // 8, 8, D // 2), U32),
                        pltpu.SemaphoreType.DMA((3,)), pltpu.SemaphoreType.DMA((2,))],
        compiler_params=_params(("arbitrary",)),
        name="moe_combine",
    )(dest2, route, x1, g_final, yb)


def _moe(h2p, route, route_t, counts, x1, g_final, w1, w3, w2):
    T = h2p.shape[0]
    A = T * EXPERT_TOPK
    P = -(-A // MOE_BLK) * MOE_BLK + N_EXPERTS * MOE_BLK
    cnt = counts[0, :N_EXPERTS].astype(I32)
    padded = (cnt + MOE_BLK - 1) // MOE_BLK * MOE_BLK
    pends = jnp.cumsum(padded)
    pstarts = pends - padded
    ids = route_t[0:EXPERT_TOPK].astype(I32)
    ranks = route_t[4:4 + EXPERT_TOPK].astype(I32)
    dest = ranks
    for e in range(N_EXPERTS):
        dest = dest + jnp.where(ids == e, pstarts[e], 0)

    def tiled(tm):
        tm = min(tm, T)
        return dest.reshape(EXPERT_TOPK, T // tm, tm).transpose(1, 0, 2).reshape(T // tm, EXPERT_TOPK * tm)

    n_used = (pends[-1:] // MOE_BLK).astype(I32)
    zstart = jnp.where(padded > 0, pends - MOE_BLK, -1).astype(I32)
    xd = _dispatch(zstart, n_used, tiled(DISPATCH_TILE), h2p, P)
    yb = _experts((pstarts // MOE_BLK).astype(I32), (padded // MOE_BLK).astype(I32), n_used, xd, w1, w3, w2)
    return _combine(tiled(COMBINE_TILE), route, x1, g_final, yb)


def _rope_tables(seq):
    inv = 1.0 / (ROPE_THETA ** (jnp.arange(0, HEAD_DIM, 2, dtype=F32) / HEAD_DIM))
    ang = jnp.arange(seq, dtype=F32)[:, None] * inv[None, :]
    ang = jnp.concatenate([ang, ang], axis=-1)
    sign = jnp.concatenate([-jnp.ones((HEAD_DIM // 2,), F32), jnp.ones((HEAD_DIM // 2,), F32)])
    reps = LANES // HEAD_DIM
    return jnp.tile(jnp.cos(ang), (1, reps)), jnp.tile(jnp.sin(ang) * sign, (1, reps))


def kernel(x, g_mix, w_in, w_branch_moba, w_branch_diff, w_out, diff_lambda_q1, diff_lambda_k1, diff_lambda_q2, diff_lambda_k2, diff_subln_g, g_ffn, w_group, b_group, w_router, b_router, w_expert_gate, w_expert_up, w_expert_down, g_final):
    B, S, D = x.shape
    assert w_in.shape[0] == 1, "one layer"
    T = B * S
    x2 = x.reshape(T, D)
    n_qkv = w_in.shape[2] - 2 * D
    w_qkv = w_in[0, :, :n_qkv].astype(BF16)
    w_g = w_in[0, :, n_qkv:].astype(BF16)
    cos, sin = _rope_tables(S)

    qa, ka, va, qd, kd, vd = _qkv_proj(x2, g_mix, w_qkv, cos, sin, S)
    W = qa.shape[1]
    o_a = _moba_attention(qa.reshape(B, S, W), ka.reshape(B, S, W), va.reshape(B, S, W)).reshape(T, W)
    o_d = _diff_attention(qd.reshape(B, S, W), kd.reshape(B, S, W), vd.reshape(B, S, W),
                          diff_lambda_q1, diff_lambda_k1, diff_lambda_q2, diff_lambda_k2, diff_subln_g).reshape(T, W)

    w_r = jnp.concatenate([w_group[0], w_router[0]], axis=1)
    w_r = jnp.pad(w_r, ((0, 0), (0, LANES - w_r.shape[1])))
    wr_hi = w_r.astype(BF16)
    w_r2 = jnp.concatenate([wr_hi, (w_r - wr_hi.astype(F32)).astype(BF16)], axis=1)
    b_r = jnp.pad(jnp.concatenate([b_group[0], b_router[0]])[None, :], ((0, 0), (0, LANES - N_GROUPS - N_EXPERTS)))

    x1, h2p, route, route_t, counts = _merge_route(
        x2, o_a, o_d, g_mix, w_g, w_branch_moba[0].astype(BF16), w_branch_diff[0].astype(BF16),
        w_out[0].astype(BF16), g_ffn, w_r2, b_r)

    out = _moe(h2p, route, route_t, counts, x1, g_final[None, :],
               w_expert_gate[0], w_expert_up[0], w_expert_down[0])
    return out.reshape(B, S, D)
```

```python
import functools

import jax
import jax.numpy as jnp
from jax import lax
from jax.experimental import pallas as pl
from jax.experimental.pallas import tpu as pltpu

F32 = jnp.float32
BF16 = jnp.bfloat16
I32 = jnp.int32
U32 = jnp.uint32

LANES = 128
SUBLANES = 8
HEAD_DIM = 64
ATT_BLK = 256
ATT_BLK_SHIFT = 8
MOBA_TOPK = 3
N_GROUPS = 4
EXPERTS_PER_GROUP = 8
N_EXPERTS = N_GROUPS * EXPERTS_PER_GROUP
EXPERT_TOPK = 2
MOE_BLK = 512
DISPATCH_TILE = 512
COMBINE_TILE = 512
MERGE_TILE = 512
QKV_TILE = 512
ROUTE_COLS = 8
EPS = 1e-6
NEG = -1e30
ROPE_THETA = 10000.0
LAMBDA_INIT = 0.8 - 0.6 * 1.0
ATT_SCALE = HEAD_DIM ** -0.5
VMEM_LIMIT = 48 * 1024 * 1024
EXPERT_VMEM_LIMIT = 56 * 1024 * 1024


def _dot(a, b):
    return jnp.dot(a, b, preferred_element_type=F32)


def _dot_nt(a, b):
    return lax.dot_general(a, b, (((1,), (1,)), ((), ())), preferred_element_type=F32)


def _rms(x, g):
    var = jnp.mean(x * x, axis=-1, keepdims=True)
    return (x * lax.rsqrt(var + EPS)) * g


def _params(sem, vmem=VMEM_LIMIT):
    return pltpu.CompilerParams(dimension_semantics=sem, vmem_limit_bytes=vmem)


def _qkv_kernel(x_ref, g_ref, w_ref, cos_ref, sin_ref, qa_ref, ka_ref, va_ref, qd_ref, kd_ref, vd_ref):
    h = _rms(x_ref[...], g_ref[...]).astype(BF16)
    cos = cos_ref[...]
    sin = sin_ref[...]
    lane = lax.broadcasted_iota(I32, cos.shape, 1)
    first = (lane & (HEAD_DIM - 1)) < HEAD_DIM // 2
    width = qa_ref.shape[1]
    outs = ((qa_ref, True, ATT_SCALE), (ka_ref, True, 1.0), (va_ref, False, 1.0),
            (qd_ref, True, ATT_SCALE), (kd_ref, True, 1.0), (vd_ref, False, 1.0))
    for j, (o_ref, rope, scale) in enumerate(outs):
        y = _dot(h, w_ref[:, j * width:(j + 1) * width])
        for c in range(width // LANES):
            yc = y[:, c * LANES:(c + 1) * LANES]
            if rope:
                rot = jnp.where(first, pltpu.roll(yc, LANES - HEAD_DIM // 2, 1), pltpu.roll(yc, HEAD_DIM // 2, 1))
                yc = yc * cos + rot * sin
            if scale != 1.0:
                yc = yc * scale
            o_ref[:, c * LANES:(c + 1) * LANES] = yc.astype(BF16)


def _qkv_proj(x2, g_mix, w_qkv, cos, sin, seq):
    T, D = x2.shape
    width = w_qkv.shape[1] // 6
    tm = min(QKV_TILE, seq)
    assert seq % tm == 0 and T % tm == 0
    spt = seq // tm
    out = jax.ShapeDtypeStruct((T, width), BF16)
    return pl.pallas_call(
        _qkv_kernel,
        grid=(T // tm,),
        in_specs=[
            pl.BlockSpec((tm, D), lambda i: (i, 0)),
            pl.BlockSpec((1, D), lambda i: (0, 0)),
            pl.BlockSpec(w_qkv.shape, lambda i: (0, 0)),
            pl.BlockSpec((tm, LANES), lambda i: (i % spt, 0)),
            pl.BlockSpec((tm, LANES), lambda i: (i % spt, 0)),
        ],
        out_specs=[pl.BlockSpec((tm, width), lambda i: (i, 0))] * 6,
        out_shape=[out] * 6,
        compiler_params=_params(("arbitrary",)),
        name="qkv_proj",
    )(x2, g_mix, w_qkv, cos, sin)


ATT_ROWS = 2 * ATT_BLK


def _attend(qs, k_block, v_rows, s_ref, p_ref, cbias_ref, i):
    mx = None
    for n in range(i + 1):
        sb = _dot_nt(qs, k_block(n))
        if n == i:
            sb = sb + cbias_ref[...]
        s_ref[:, n * ATT_BLK:(n + 1) * ATT_BLK] = sb
        for c in range(ATT_BLK // LANES):
            part = sb[:, c * LANES:(c + 1) * LANES]
            mx = part if mx is None else jnp.maximum(mx, part)
    mb = jnp.broadcast_to(jnp.max(mx, axis=1, keepdims=True), (ATT_ROWS, LANES))
    ps = None
    for c in range((i + 1) * ATT_BLK // LANES):
        p = jnp.exp(s_ref[:, c * LANES:(c + 1) * LANES] - mb)
        ps = p if ps is None else ps + p
        p_ref[:, c * LANES:(c + 1) * LANES] = p.astype(BF16)
    l = jnp.sum(ps, axis=1, keepdims=True)
    L = (i + 1) * ATT_BLK
    return _dot(p_ref[:, :L], v_rows(L)) * (1.0 / l)


def _causal_bias():
    row = lax.broadcasted_iota(I32, (ATT_ROWS, ATT_BLK), 0) & (ATT_BLK - 1)
    col = lax.broadcasted_iota(I32, (ATT_ROWS, ATT_BLK), 1)
    return jnp.where(col <= row, 0.0, NEG).astype(F32)


def _attention_scratch(seq):
    return [pltpu.VMEM((ATT_ROWS, ATT_BLK), F32), pltpu.VMEM((ATT_ROWS, seq), F32), pltpu.VMEM((ATT_ROWS, seq), BF16)]


def _moba_kernel(q_ref, k_ref, v_ref, o_ref, kaug_ref, kmf_ref, kmhi_ref, kmlo_ref, cbias_ref, s_ref, p_ref):
    S = k_ref.shape[1]
    nb = S // ATT_BLK
    k = k_ref[0]
    lane_s = lax.broadcasted_iota(I32, (S, LANES), 1)
    blk_s = lax.broadcasted_iota(I32, (S, LANES), 0) >> ATT_BLK_SHIFT
    kaug_ref[:, :LANES] = k
    kaug_ref[:, LANES:] = jnp.where(lane_s == blk_s, 1.0, 0.0).astype(BF16)
    kmean = jnp.mean(k.astype(F32).reshape(nb, ATT_BLK, LANES), axis=1)
    lane_n = lax.broadcasted_iota(I32, (nb, LANES), 1)
    for hh in range(2):
        own_n = (lane_n < HEAD_DIM) if hh == 0 else (lane_n >= HEAD_DIM)
        kmf_ref[...] = jnp.zeros((LANES, LANES), F32)
        kmf_ref[0:nb, :] = jnp.where(own_n, kmean, 0.0)
        km = kmf_ref[...]
        hi = km.astype(BF16)
        kmhi_ref[hh] = hi
        kmlo_ref[hh] = (km - hi.astype(F32)).astype(BF16)
    cbias_ref[...] = _causal_bias()

    lane = lax.broadcasted_iota(I32, (ATT_BLK, LANES), 1)

    def stacked_queries(i):
        q = q_ref[0, i * ATT_BLK:(i + 1) * ATT_BLK, :].astype(F32)
        rows = []
        for hh in range(2):
            own = (lane < HEAD_DIM) if hh == 0 else (lane >= HEAD_DIM)
            qh = jnp.where(own, q, 0.0)
            if i <= MOBA_TOPK:
                rows.append(qh)
                continue
            qhb = qh.astype(BF16)
            gate = (_dot_nt(kmhi_ref[hh], qhb) + _dot_nt(kmlo_ref[hh], qhb))[:SUBLANES, :]
            blk = lax.broadcasted_iota(I32, (SUBLANES, ATT_BLK), 0)
            cand = blk < i
            g = jnp.where(cand, gate, -jnp.inf)
            rank = jnp.zeros((SUBLANES, ATT_BLK), I32)
            for r in range(1, i):
                rank = rank + jnp.where((blk >= r) & (pltpu.roll(g, r, 0) >= g), 1, 0)
                rank = rank + jnp.where((blk < SUBLANES - r) & (pltpu.roll(g, SUBLANES - r, 0) > g), 1, 0)
            drop_t = jnp.where(cand & (rank >= MOBA_TOPK), NEG, 0.0)
            drop = jnp.concatenate([drop_t, jnp.zeros((LANES - SUBLANES, ATT_BLK), F32)], axis=0).T
            rows.append(jnp.concatenate([qh, drop], axis=1))
        return jnp.concatenate(rows, axis=0).astype(BF16)

    for i in reversed(range(nb)):
        if i > MOBA_TOPK:
            k_block = lambda n: kaug_ref[n * ATT_BLK:(n + 1) * ATT_BLK, :]
        else:
            k_block = lambda n: k_ref[0, n * ATT_BLK:(n + 1) * ATT_BLK, :]
        out = _attend(stacked_queries(i), k_block, lambda L: v_ref[0, :L, :], s_ref, p_ref, cbias_ref, i)
        o_ref[0, i * ATT_BLK:(i + 1) * ATT_BLK, :] = jnp.where(
            lane < HEAD_DIM, out[:ATT_BLK], out[ATT_BLK:]).astype(BF16)


def _moba_attention(qa, ka, va):
    B, S, W = qa.shape
    assert S % ATT_BLK == 0 and S // ATT_BLK <= SUBLANES and W % LANES == 0
    spec = pl.BlockSpec((1, S, LANES), lambda b, j: (b, 0, j))
    return pl.pallas_call(
        _moba_kernel,
        grid=(B, W // LANES),
        in_specs=[spec, spec, spec],
        out_specs=spec,
        out_shape=jax.ShapeDtypeStruct((B, S, W), BF16),
        scratch_shapes=[
            pltpu.VMEM((S, 2 * LANES), BF16),
            pltpu.VMEM((LANES, LANES), F32),
            pltpu.VMEM((2, LANES, LANES), BF16),
            pltpu.VMEM((2, LANES, LANES), BF16),
        ] + _attention_scratch(S),
        compiler_params=_params(("arbitrary", "arbitrary")),
        name="moba_attention",
    )(qa, ka, va)


def _diff_kernel(lq1_ref, lk1_ref, lq2_ref, lk2_ref, g_ref, q_ref, k_ref, v_ref, o_ref, cbias_ref, s_ref, p_ref):
    S = k_ref.shape[1]
    nb = S // ATT_BLK
    lam = (jnp.exp(jnp.sum(lq1_ref[...] * lk1_ref[...], axis=1, keepdims=True))
           - jnp.exp(jnp.sum(lq2_ref[...] * lk2_ref[...], axis=1, keepdims=True)) + LAMBDA_INIT)
    cbias_ref[...] = _causal_bias()
    lane = lax.broadcasted_iota(I32, (ATT_BLK, LANES), 1)

    for i in reversed(range(nb)):
        r0 = i * ATT_BLK
        q = q_ref[0, r0:r0 + ATT_BLK, :].astype(F32)
        qs = jnp.concatenate([jnp.where(lane < HEAD_DIM, q, 0.0), jnp.where(lane >= HEAD_DIM, q, 0.0)],
                             axis=0).astype(BF16)
        out = _attend(qs, lambda n: k_ref[0, n * ATT_BLK:(n + 1) * ATT_BLK, :], lambda L: v_ref[0, :L, :],
                      s_ref, p_ref, cbias_ref, i)
        o = out[:ATT_BLK] - lam * out[ATT_BLK:]
        o_ref[0, r0:r0 + ATT_BLK, :] = (_rms(o, g_ref[...]) * (1.0 - LAMBDA_INIT)).astype(BF16)


def _diff_attention(qd, kd, vd, lq1, lk1, lq2, lk2, subln_g):
    B, S, W = qd.shape
    assert S % ATT_BLK == 0 and W % LANES == 0
    spec = pl.BlockSpec((1, S, LANES), lambda b, j: (b, 0, j))
    small = pl.BlockSpec((1, HEAD_DIM), lambda b, j: (0, 0))
    return pl.pallas_call(
        _diff_kernel,
        grid=(B, W // LANES),
        in_specs=[small, small, small, small, pl.BlockSpec((1, LANES), lambda b, j: (0, 0)), spec, spec, spec],
        out_specs=spec,
        out_shape=jax.ShapeDtypeStruct((B, S, W), BF16),
        scratch_shapes=_attention_scratch(S),
        compiler_params=_params(("arbitrary", "arbitrary")),
        name="diff_attention",
    )(lq1, lk1, lq2, lk2, subln_g, qd, kd, vd)


def _pack_bf16_pairs(x):
    n = x.shape[1] // 2
    lo = lax.bitcast_convert_type(x[:, :n].astype(BF16).astype(F32), U32)
    hi = lax.bitcast_convert_type(x[:, n:].astype(BF16).astype(F32), U32)
    return (lo >> 16) | hi


def _unpack_bf16_pairs(w, dtype=BF16):
    lo = lax.bitcast_convert_type(w << 16, F32)
    hi = lax.bitcast_convert_type(w & jnp.uint32(0xFFFF0000), F32)
    return jnp.concatenate([lo, hi], axis=1).astype(dtype)


def _merge_kernel(x_ref, oa_ref, od_ref, gmix_ref, wg_ref, wbm_ref, wbd_ref, wout_ref, gffn_ref,
                  wr_ref, br_ref,
                  x1_ref, h2_ref, route_ref, routet_ref, cnt_ref, base_ref, ltri_ref):
    tm, D = x_ref.shape

    @pl.when(pl.program_id(0) == 0)
    def _():
        base_ref[...] = jnp.zeros_like(base_ref)
        row = lax.broadcasted_iota(I32, (tm, tm), 0)
        col = lax.broadcasted_iota(I32, (tm, tm), 1)
        ltri_ref[...] = jnp.where(col < row, 1.0, 0.0).astype(BF16)

    x = x_ref[...]
    h = _rms(x, gmix_ref[...]).astype(BF16)
    sig = jax.nn.sigmoid(_dot(h, wg_ref[...]))
    merged = sig[:, :D] * _dot(oa_ref[...], wbm_ref[...]) + sig[:, D:] * _dot(od_ref[...], wbd_ref[...])
    x1 = x + _dot(merged.astype(BF16), wout_ref[...])
    x1_ref[...] = x1
    h2 = _rms(x1, gffn_ref[...])
    h2_ref[...] = _pack_bf16_pairs(h2)

    hhi = h2.astype(BF16)
    hlo = (h2 - hhi.astype(F32)).astype(BF16)
    lg_hi = _dot(hhi, wr_ref[...])
    lg_lo = _dot(hlo, wr_ref[...])
    lg = (lg_hi[:, :LANES] + lg_hi[:, LANES:]) + (lg_lo[:, :LANES] + lg_lo[:, LANES:]) + br_ref[...]
    lane = lax.broadcasted_iota(I32, (tm, LANES), 1)
    ninf = -jnp.inf

    def first_argmax(vals):
        m = jnp.max(vals, axis=1, keepdims=True)
        return m, jnp.min(jnp.where(vals == m, lane, LANES), axis=1, keepdims=True)

    is_group = lane < N_GROUPS
    gm, gsel = first_argmax(jnp.where(is_group, lg, ninf))
    p_group = 1.0 / jnp.sum(jnp.where(is_group, jnp.exp(lg - gm), 0.0), axis=1, keepdims=True)
    lo = N_GROUPS + gsel * EXPERTS_PER_GROUP
    el = jnp.where((lane >= lo) & (lane < lo + EXPERTS_PER_GROUP), lg, ninf)
    m1, i1 = first_argmax(el)
    m2, i2 = first_argmax(jnp.where(lane == i1, ninf, el))
    e2x = jnp.exp(m2 - m1)
    w1 = p_group / (1.0 + e2x)
    w2 = p_group * e2x / (1.0 + e2x)
    e1 = i1 - N_GROUPS
    e2 = i2 - N_GROUPS

    oh1 = jnp.where(lane == e1, 1.0, 0.0)
    oh2 = jnp.where(lane == e2, 1.0, 0.0)
    cc = _dot(ltri_ref[...], jnp.concatenate([oh1, oh2], axis=1).astype(BF16))
    c1, c2 = cc[:, :LANES], cc[:, LANES:]
    base = base_ref[...]
    tot1 = jnp.sum(oh1, axis=0, keepdims=True)
    tot2 = jnp.sum(oh2, axis=0, keepdims=True)
    r1 = jnp.sum(oh1 * (c1 + base), axis=1, keepdims=True)
    r2 = jnp.sum(oh2 * (c2 + base + tot1), axis=1, keepdims=True)
    base = base + tot1 + tot2
    base_ref[...] = base
    cnt_ref[...] = base

    cols = (e1.astype(F32), e2.astype(F32), w1, w2, r1, r2)
    route = jnp.zeros((tm, LANES), F32)
    for c, val in enumerate(cols):
        route = jnp.where(lane == c, val, route)
    route_ref[...] = route[:, :ROUTE_COLS]
    routet_ref[...] = route.T[:ROUTE_COLS, :]


def _merge_route(x2, o_a, o_d, g_mix, w_g, w_bm, w_bd, w_out, g_ffn, w_r2, b_r):
    T, D = x2.shape
    tm = min(MERGE_TILE, T)
    assert T % tm == 0
    W = o_a.shape[1]
    tile = lambda w: pl.BlockSpec((tm, w), lambda i: (i, 0))
    full = lambda a: pl.BlockSpec(a.shape, lambda i: (0, 0))
    return pl.pallas_call(
        _merge_kernel,
        grid=(T // tm,),
        in_specs=[tile(D), tile(W), tile(W), full(g_mix), full(w_g), full(w_bm), full(w_bd), full(w_out),
                  full(g_ffn), full(w_r2), full(b_r)],
        out_specs=[tile(D), tile(D // 2), tile(ROUTE_COLS), pl.BlockSpec((ROUTE_COLS, tm), lambda i: (0, i)),
                   pl.BlockSpec((1, LANES), lambda i: (0, 0))],
        out_shape=[jax.ShapeDtypeStruct((T, D), F32), jax.ShapeDtypeStruct((T, D // 2), U32),
                   jax.ShapeDtypeStruct((T, ROUTE_COLS), F32), jax.ShapeDtypeStruct((ROUTE_COLS, T), F32),
                   jax.ShapeDtypeStruct((1, LANES), F32)],
        scratch_shapes=[pltpu.VMEM((1, LANES), F32), pltpu.VMEM((tm, tm), BF16)],
        compiler_params=_params(("arbitrary",)),
        name="merge_route",
    )(x2, o_a, o_d, g_mix, w_g, w_bm, w_bd, w_out, g_ffn, w_r2, b_r)


def _index_prefetch(dest_hbm, idx_ref, isem):
    i = pl.program_id(0)
    per = dest_hbm.shape[1]
    slot = i % 2

    def copy(step, sl):
        return pltpu.make_async_copy(dest_hbm.at[step], idx_ref.at[pl.ds(sl * per, per)], isem.at[sl])

    @pl.when(i == 0)
    def _():
        copy(0, 0).start()

    @pl.when(i + 1 < pl.num_programs(0))
    def _():
        copy(i + 1, 1 - slot).start()

    copy(i, slot).wait()
    return slot * per


def _dispatch_kernel(zs_ref, nu_ref, dest_hbm, h2_hbm, xd_hbm, idx_ref, zero_ref, src_ref, isem, ssem, rsem, zsem):
    tm = h2_hbm.shape[1] * 8
    nblk = xd_hbm.shape[0] // MOE_BLK
    i = pl.program_id(0)
    last = pl.num_programs(0) - 1

    @pl.when(i == 0)
    def _():
        zero_ref[...] = jnp.zeros_like(zero_ref)

        def zcopy(row):
            return pltpu.make_async_copy(zero_ref, xd_hbm.at[pl.ds(pl.multiple_of(row, MOE_BLK), MOE_BLK), :], zsem)

        def tail_start(blk, carry):
            zcopy(blk * MOE_BLK).start()
            return carry

        def tail_wait(blk, carry):
            zcopy(0).wait()
            return carry

        for e in range(N_EXPERTS):
            @pl.when(zs_ref[e] >= 0)
            def _():
                zcopy(zs_ref[e]).start()
        lax.fori_loop(nu_ref[0], nblk, tail_start, 0)
        for e in range(N_EXPERTS):
            @pl.when(zs_ref[e] >= 0)
            def _():
                zcopy(0).wait()
        lax.fori_loop(nu_ref[0], nblk, tail_wait, 0)

    def src_copy(step):
        return pltpu.make_async_copy(h2_hbm.at[step], src_ref.at[step % 3], ssem.at[step % 3])

    @pl.when(i == 0)
    def _():
        src_copy(0).start()

    @pl.when(i < last)
    def _():
        src_copy(i + 1).start()

    base = _index_prefetch(dest_hbm, idx_ref, isem)
    src_copy(i).wait()
    for slot in range(3):
        @pl.when(i % 3 == slot)
        def _():
            for r in range(tm):
                for kk in range(EXPERT_TOPK):
                    d = idx_ref[base + kk * tm + r]
                    pltpu.make_async_copy(src_ref.at[slot, r // 8, pl.ds(r % 8, 1), :],
                                          xd_hbm.at[pl.ds(d, 1), :], rsem.at[i % 2]).start(priority=kk)

    def drain(sem_slot):
        for kk in range(EXPERT_TOPK):
            pltpu.make_async_copy(src_ref.at[0], src_ref.at[0], rsem.at[sem_slot]).wait()

    @pl.when(i > 0)
    def _():
        drain((i - 1) % 2)

    @pl.when(i == last)
    def _():
        drain(i % 2)


def _dispatch(zstart, n_used, dest2, h2p, n_rows):
    T, Wp = h2p.shape
    nt, per = dest2.shape
    tm = per // EXPERT_TOPK
    return pl.pallas_call(
        _dispatch_kernel,
        grid_spec=pltpu.PrefetchScalarGridSpec(
            num_scalar_prefetch=2,
            grid=(nt,),
            in_specs=[pl.BlockSpec(memory_space=pl.ANY), pl.BlockSpec(memory_space=pl.ANY)],
            out_specs=pl.BlockSpec(memory_space=pl.ANY),
            scratch_shapes=[pltpu.SMEM((2 * per,), I32), pltpu.VMEM((MOE_BLK, Wp), U32),
                            pltpu.VMEM((3, tm // 8, 8, Wp), U32),
                            pltpu.SemaphoreType.DMA((2,)), pltpu.SemaphoreType.DMA((3,)),
                            pltpu.SemaphoreType.DMA((2,)), pltpu.SemaphoreType.DMA(())],
        ),
        out_shape=jax.ShapeDtypeStruct((n_rows, Wp), U32),
        compiler_params=_params(("arbitrary",)),
        name="moe_dispatch",
    )(zstart, n_used, dest2, h2p.reshape(nt, tm // 8, 8, Wp))


def _expert_kernel(be_ref, nu_ref, xd_ref, w1_ref, w3_ref, w2_ref, yb_ref, w1b, w3b, w2b):
    i = pl.program_id(0)
    n_used = nu_ref[0]

    @pl.when((i < n_used) & ((i == 0) | (be_ref[i] != be_ref[jnp.maximum(i - 1, 0)])))
    def _():
        w1b[...] = w1_ref[0].astype(BF16)
        w3b[...] = w3_ref[0].astype(BF16)
        w2b[...] = w2_ref[0].astype(BF16)

    @pl.when(i < n_used)
    def _():
        x = _unpack_bf16_pairs(xd_ref[...])
        act = jax.nn.silu(_dot(x, w1b[...])) * _dot(x, w3b[...])
        yb_ref[...] = _pack_bf16_pairs(_dot(act.astype(BF16), w2b[...]))

    @pl.when(i >= n_used)
    def _():
        yb_ref[...] = jnp.zeros_like(yb_ref)


def _experts(blk_expert, n_used, xd, w1, w3, w2):
    P, Wp = xd.shape
    E, D, De = w1.shape
    nblk = P // MOE_BLK
    return pl.pallas_call(
        _expert_kernel,
        grid_spec=pltpu.PrefetchScalarGridSpec(
            num_scalar_prefetch=2,
            grid=(nblk,),
            in_specs=[
                pl.BlockSpec((MOE_BLK, Wp), lambda i, be, nu: (jnp.minimum(i, nu[0] - 1), 0)),
                pl.BlockSpec((1, D, De), lambda i, be, nu: (be[i], 0, 0)),
                pl.BlockSpec((1, D, De), lambda i, be, nu: (be[i], 0, 0)),
                pl.BlockSpec((1, De, D), lambda i, be, nu: (be[i], 0, 0)),
            ],
            out_specs=pl.BlockSpec((MOE_BLK, D // 2), lambda i, be, nu: (i, 0)),
            scratch_shapes=[pltpu.VMEM((D, De), BF16), pltpu.VMEM((D, De), BF16), pltpu.VMEM((De, D), BF16)],
        ),
        out_shape=jax.ShapeDtypeStruct((P, D // 2), U32),
        compiler_params=_params(("arbitrary",), EXPERT_VMEM_LIMIT),
        name="moe_experts",
    )(blk_expert, n_used, xd, w1, w3, w2)


def _combine_kernel(dest_hbm, route_ref, x1_ref, g_ref, yb_hbm, o_ref, idx_ref, buf_ref, isem, rsem):
    tm, D = x1_ref.shape
    per = EXPERT_TOPK * tm
    i = pl.program_id(0)
    last = pl.num_programs(0) - 1

    def idx_copy(step):
        sl = step % 3
        return pltpu.make_async_copy(dest_hbm.at[step], idx_ref.at[pl.ds(sl * per, per)], isem.at[sl])

    def issue(step, bs, j):
        ib = (step % 3) * per
        for u in range(8):
            for kk in range(EXPERT_TOPK):
                d = idx_ref[ib + kk * tm + 8 * j + u]
                pltpu.make_async_copy(yb_hbm.at[pl.ds(d, 1), :], buf_ref.at[bs, kk, j, pl.ds(u, 1), :],
                                      rsem.at[bs]).start(priority=kk)

    def drain(bs):
        for kk in range(EXPERT_TOPK):
            pltpu.make_async_copy(buf_ref.at[bs, kk], buf_ref.at[bs, kk], rsem.at[bs]).wait()

    @pl.when(i == 0)
    def _():
        idx_copy(0).start()

        @pl.when(last > 0)
        def _():
            idx_copy(1).start()

        idx_copy(0).wait()
        for j in range(tm // 8):
            issue(0, 0, j)

    @pl.when(i + 2 <= last)
    def _():
        idx_copy(i + 2).start()

    @pl.when(i < last)
    def _():
        idx_copy(i + 1).wait()

    cur = i % 2
    drain(cur)
    nxt = jnp.minimum(i + 1, last)
    for j in range(tm // 8):
        issue(nxt, 1 - cur, j)
        rows = slice(8 * j, 8 * j + 8)
        route = route_ref[rows, :]
        y0 = _unpack_bf16_pairs(buf_ref[cur, 0, j], F32)
        y1 = _unpack_bf16_pairs(buf_ref[cur, 1, j], F32)
        o_ref[rows, :] = _rms(x1_ref[rows, :] + (route[:, 2:3] * y0 + route[:, 3:4] * y1), g_ref[...])

    @pl.when(i == last)
    def _():
        drain(1 - cur)


def _combine(dest2, route, x1, g_final, yb):
    T, D = x1.shape
    nt, per = dest2.shape
    tm = per // EXPERT_TOPK
    return pl.pallas_call(
        _combine_kernel,
        grid=(nt,),
        in_specs=[pl.BlockSpec(memory_space=pl.ANY), pl.BlockSpec((tm, ROUTE_COLS), lambda i: (i, 0)),
                  pl.BlockSpec((tm, D), lambda i: (i, 0)), pl.BlockSpec((1, D), lambda i: (0, 0)),
                  pl.BlockSpec(memory_space=pl.ANY)],
        out_specs=pl.BlockSpec((tm, D), lambda i: (i, 0)),
        out_shape=jax.ShapeDtypeStruct((T, D), F32),
        scratch_shapes=[pltpu.SMEM((3 * per,), I32), pltpu.VMEM((2, EXPERT_TOPK, tm // 8, 8, D // 2), U32),
                        pltpu.SemaphoreType.DMA((3,)), pltpu.SemaphoreType.DMA((2,))],
        compiler_params=_params(("arbitrary",)),
        name="moe_combine",
    )(dest2, route, x1, g_final, yb)


def _moe(h2p, route, route_t, counts, x1, g_final, w1, w3, w2):
    T = h2p.shape[0]
    A = T * EXPERT_TOPK
    P = -(-A // MOE_BLK) * MOE_BLK + N_EXPERTS * MOE_BLK
    nblk = P // MOE_BLK
    cnt = counts[0, :N_EXPERTS].astype(I32)
    padded = (cnt + MOE_BLK - 1) // MOE_BLK * MOE_BLK
    pends = jnp.cumsum(padded)
    pstarts = pends - padded
    ids = route_t[0:EXPERT_TOPK].astype(I32)
    ranks = route_t[4:4 + EXPERT_TOPK].astype(I32)
    dest = ranks
    for e in range(N_EXPERTS):
        dest = dest + jnp.where(ids == e, pstarts[e], 0)

    def tiled(tm):
        tm = min(tm, T)
        return dest.reshape(EXPERT_TOPK, T // tm, tm).transpose(1, 0, 2).reshape(T // tm, EXPERT_TOPK * tm)

    blk_row = jnp.arange(nblk, dtype=I32) * MOE_BLK
    blk_expert = jnp.minimum(jnp.sum((pends[None, :] <= blk_row[:, None]).astype(I32), axis=1), N_EXPERTS - 1)
    n_used = (pends[-1:] // MOE_BLK).astype(I32)
    zstart = jnp.where(padded > 0, pends - MOE_BLK, -1).astype(I32)
    xd = _dispatch(zstart, n_used, tiled(DISPATCH_TILE), h2p, P)
    yb = _experts(blk_expert, n_used, xd, w1, w3, w2)
    return _combine(tiled(COMBINE_TILE), route, x1, g_final, yb)


def _rope_tables(seq):
    inv = 1.0 / (ROPE_THETA ** (jnp.arange(0, HEAD_DIM, 2, dtype=F32) / HEAD_DIM))
    ang = jnp.arange(seq, dtype=F32)[:, None] * inv[None, :]
    ang = jnp.concatenate([ang, ang], axis=-1)
    sign = jnp.concatenate([-jnp.ones((HEAD_DIM // 2,), F32), jnp.ones((HEAD_DIM // 2,), F32)])
    reps = LANES // HEAD_DIM
    return jnp.tile(jnp.cos(ang), (1, reps)), jnp.tile(jnp.sin(ang) * sign, (1, reps))


def kernel(x, g_mix, w_in, w_branch_moba, w_branch_diff, w_out, diff_lambda_q1, diff_lambda_k1, diff_lambda_q2, diff_lambda_k2, diff_subln_g, g_ffn, w_group, b_group, w_router, b_router, w_expert_gate, w_expert_up, w_expert_down, g_final):
    B, S, D = x.shape
    assert w_in.shape[0] == 1, "one layer"
    T = B * S
    x2 = x.reshape(T, D)
    n_qkv = w_in.shape[2] - 2 * D
    w_qkv = w_in[0, :, :n_qkv].astype(BF16)
    w_g = w_in[0, :, n_qkv:].astype(BF16)
    cos, sin = _rope_tables(S)

    qa, ka, va, qd, kd, vd = _qkv_proj(x2, g_mix, w_qkv, cos, sin, S)
    W = qa.shape[1]
    o_a = _moba_attention(qa.reshape(B, S, W), ka.reshape(B, S, W), va.reshape(B, S, W)).reshape(T, W)
    o_d = _diff_attention(qd.reshape(B, S, W), kd.reshape(B, S, W), vd.reshape(B, S, W),
                          diff_lambda_q1, diff_lambda_k1, diff_lambda_q2, diff_lambda_k2, diff_subln_g).reshape(T, W)

    w_r = jnp.concatenate([w_group[0], w_router[0]], axis=1)
    w_r = jnp.pad(w_r, ((0, 0), (0, LANES - w_r.shape[1])))
    wr_hi = w_r.astype(BF16)
    w_r2 = jnp.concatenate([wr_hi, (w_r - wr_hi.astype(F32)).astype(BF16)], axis=1)
    b_r = jnp.pad(jnp.concatenate([b_group[0], b_router[0]])[None, :], ((0, 0), (0, LANES - N_GROUPS - N_EXPERTS)))

    x1, h2p, route, route_t, counts = _merge_route(
        x2, o_a, o_d, g_mix, w_g, w_branch_moba[0].astype(BF16), w_branch_diff[0].astype(BF16),
        w_out[0].astype(BF16), g_ffn, w_r2, b_r)

    out = _moe(h2p, route, route_t, counts, x1, g_final[None, :],
               w_expert_gate[0], w_expert_up[0], w_expert_down[0])
    return out.reshape(B, S, D)
```

```python
import functools

import jax
import jax.numpy as jnp
from jax import lax
from jax.experimental import pallas as pl
from jax.experimental.pallas import tpu as pltpu

F32 = jnp.float32
BF16 = jnp.bfloat16
I32 = jnp.int32
U32 = jnp.uint32

LANES = 128
SUBLANES = 8
HEAD_DIM = 64
ATT_BLK = 256
ATT_BLK_SHIFT = 8
MOBA_TOPK = 3
N_GROUPS = 4
EXPERTS_PER_GROUP = 8
N_EXPERTS = N_GROUPS * EXPERTS_PER_GROUP
EXPERT_TOPK = 2
MOE_BLK = 512
DISPATCH_TILE = 512
COMBINE_TILE = 1024
MERGE_TILE = 512
QKV_TILE = 1024
ROUTE_COLS = 8
EPS = 1e-6
NEG = -1e30
ROPE_THETA = 10000.0
LAMBDA_INIT = 0.8 - 0.6 * 1.0
ATT_SCALE = HEAD_DIM ** -0.5
VMEM_LIMIT = 48 * 1024 * 1024
EXPERT_VMEM_LIMIT = 56 * 1024 * 1024


def _dot(a, b):
    return jnp.dot(a, b, preferred_element_type=F32)


def _dot_nt(a, b):
    return lax.dot_general(a, b, (((1,), (1,)), ((), ())), preferred_element_type=F32)


def _rms(x, g):
    var = jnp.mean(x * x, axis=-1, keepdims=True)
    return (x * lax.rsqrt(var + EPS)) * g


def _params(sem, vmem=VMEM_LIMIT):
    return pltpu.CompilerParams(dimension_semantics=sem, vmem_limit_bytes=vmem)


def _qkv_kernel(x_ref, g_ref, w_ref, cos_ref, sin_ref, qa_ref, ka_ref, va_ref, qd_ref, kd_ref, vd_ref):
    h = _rms(x_ref[...], g_ref[...]).astype(BF16)
    cos = cos_ref[...]
    sin = sin_ref[...]
    lane = lax.broadcasted_iota(I32, cos.shape, 1)
    first = (lane & (HEAD_DIM - 1)) < HEAD_DIM // 2
    width = qa_ref.shape[1]
    outs = ((qa_ref, True, ATT_SCALE), (ka_ref, True, 1.0), (va_ref, False, 1.0),
            (qd_ref, True, ATT_SCALE), (kd_ref, True, 1.0), (vd_ref, False, 1.0))
    for j, (o_ref, rope, scale) in enumerate(outs):
        y = _dot(h, w_ref[:, j * width:(j + 1) * width])
        for c in range(width // LANES):
            yc = y[:, c * LANES:(c + 1) * LANES]
            if rope:
                rot = jnp.where(first, pltpu.roll(yc, LANES - HEAD_DIM // 2, 1), pltpu.roll(yc, HEAD_DIM // 2, 1))
                yc = yc * cos + rot * sin
            if scale != 1.0:
                yc = yc * scale
            o_ref[:, c * LANES:(c + 1) * LANES] = yc.astype(BF16)


def _qkv_proj(x2, g_mix, w_qkv, cos, sin, seq):
    T, D = x2.shape
    width = w_qkv.shape[1] // 6
    tm = min(QKV_TILE, seq)
    assert seq % tm == 0 and T % tm == 0
    spt = seq // tm
    out = jax.ShapeDtypeStruct((T, width), BF16)
    return pl.pallas_call(
        _qkv_kernel,
        grid=(T // tm,),
        in_specs=[
            pl.BlockSpec((tm, D), lambda i: (i, 0)),
            pl.BlockSpec((1, D), lambda i: (0, 0)),
            pl.BlockSpec(w_qkv.shape, lambda i: (0, 0)),
            pl.BlockSpec((tm, LANES), lambda i: (i % spt, 0)),
            pl.BlockSpec((tm, LANES), lambda i: (i % spt, 0)),
        ],
        out_specs=[pl.BlockSpec((tm, width), lambda i: (i, 0))] * 6,
        out_shape=[out] * 6,
        compiler_params=_params(("arbitrary",)),
        name="qkv_proj",
    )(x2, g_mix, w_qkv, cos, sin)


ATT_ROWS = 2 * ATT_BLK


def _attend(qs, k_block, v_rows, s_ref, p_ref, cbias_ref, i):
    mx = None
    for n in range(i + 1):
        sb = _dot_nt(qs, k_block(n))
        if n == i:
            sb = sb + cbias_ref[...]
        s_ref[:, n * ATT_BLK:(n + 1) * ATT_BLK] = sb
        for c in range(ATT_BLK // LANES):
            part = sb[:, c * LANES:(c + 1) * LANES]
            mx = part if mx is None else jnp.maximum(mx, part)
    mb = jnp.broadcast_to(jnp.max(mx, axis=1, keepdims=True), (ATT_ROWS, LANES))
    ps = None
    for c in range((i + 1) * ATT_BLK // LANES):
        p = jnp.exp(s_ref[:, c * LANES:(c + 1) * LANES] - mb)
        ps = p if ps is None else ps + p
        p_ref[:, c * LANES:(c + 1) * LANES] = p.astype(BF16)
    l = jnp.sum(ps, axis=1, keepdims=True)
    L = (i + 1) * ATT_BLK
    return _dot(p_ref[:, :L], v_rows(L)) * (1.0 / l)


def _causal_bias():
    row = lax.broadcasted_iota(I32, (ATT_ROWS, ATT_BLK), 0) & (ATT_BLK - 1)
    col = lax.broadcasted_iota(I32, (ATT_ROWS, ATT_BLK), 1)
    return jnp.where(col <= row, 0.0, NEG).astype(F32)


def _attention_scratch(seq):
    return [pltpu.VMEM((ATT_ROWS, ATT_BLK), F32), pltpu.VMEM((ATT_ROWS, seq), F32), pltpu.VMEM((ATT_ROWS, seq), BF16)]


def _moba_kernel(q_ref, k_ref, v_ref, o_ref, kaug_ref, kmf_ref, kmhi_ref, kmlo_ref, cbias_ref, s_ref, p_ref):
    S = k_ref.shape[1]
    nb = S // ATT_BLK
    k = k_ref[0]
    lane_s = lax.broadcasted_iota(I32, (S, LANES), 1)
    blk_s = lax.broadcasted_iota(I32, (S, LANES), 0) >> ATT_BLK_SHIFT
    kaug_ref[:, :LANES] = k
    kaug_ref[:, LANES:] = jnp.where(lane_s == blk_s, 1.0, 0.0).astype(BF16)
    kmean = jnp.mean(k.astype(F32).reshape(nb, ATT_BLK, LANES), axis=1)
    lane_n = lax.broadcasted_iota(I32, (nb, LANES), 1)
    for hh in range(2):
        own_n = (lane_n < HEAD_DIM) if hh == 0 else (lane_n >= HEAD_DIM)
        kmf_ref[...] = jnp.zeros((LANES, LANES), F32)
        kmf_ref[0:nb, :] = jnp.where(own_n, kmean, 0.0)
        km = kmf_ref[...]
        hi = km.astype(BF16)
        kmhi_ref[hh] = hi
        kmlo_ref[hh] = (km - hi.astype(F32)).astype(BF16)
    cbias_ref[...] = _causal_bias()

    lane = lax.broadcasted_iota(I32, (ATT_BLK, LANES), 1)

    def stacked_queries(i):
        q = q_ref[0, i * ATT_BLK:(i + 1) * ATT_BLK, :].astype(F32)
        rows = []
        for hh in range(2):
            own = (lane < HEAD_DIM) if hh == 0 else (lane >= HEAD_DIM)
            qh = jnp.where(own, q, 0.0)
            if i <= MOBA_TOPK:
                rows.append(qh)
                continue
            qhb = qh.astype(BF16)
            gate = (_dot_nt(kmhi_ref[hh], qhb) + _dot_nt(kmlo_ref[hh], qhb))[:SUBLANES, :]
            blk = lax.broadcasted_iota(I32, (SUBLANES, ATT_BLK), 0)
            cand = blk < i
            g = jnp.where(cand, gate, -jnp.inf)
            rank = jnp.zeros((SUBLANES, ATT_BLK), I32)
            for r in range(1, i):
                rank = rank + jnp.where((blk >= r) & (pltpu.roll(g, r, 0) >= g), 1, 0)
                rank = rank + jnp.where((blk < SUBLANES - r) & (pltpu.roll(g, SUBLANES - r, 0) > g), 1, 0)
            drop_t = jnp.where(cand & (rank >= MOBA_TOPK), NEG, 0.0)
            drop = jnp.concatenate([drop_t, jnp.zeros((LANES - SUBLANES, ATT_BLK), F32)], axis=0).T
            rows.append(jnp.concatenate([qh, drop], axis=1))
        return jnp.concatenate(rows, axis=0).astype(BF16)

    for i in reversed(range(nb)):
        if i > MOBA_TOPK:
            k_block = lambda n: kaug_ref[n * ATT_BLK:(n + 1) * ATT_BLK, :]
        else:
            k_block = lambda n: k_ref[0, n * ATT_BLK:(n + 1) * ATT_BLK, :]
        out = _attend(stacked_queries(i), k_block, lambda L: v_ref[0, :L, :], s_ref, p_ref, cbias_ref, i)
        o_ref[0, i * ATT_BLK:(i + 1) * ATT_BLK, :] = jnp.where(
            lane < HEAD_DIM, out[:ATT_BLK], out[ATT_BLK:]).astype(BF16)


def _moba_attention(qa, ka, va):
    B, S, W = qa.shape
    assert S % ATT_BLK == 0 and S // ATT_BLK <= SUBLANES and W % LANES == 0
    spec = pl.BlockSpec((1, S, LANES), lambda b, j: (b, 0, j))
    return pl.pallas_call(
        _moba_kernel,
        grid=(B, W // LANES),
        in_specs=[spec, spec, spec],
        out_specs=spec,
        out_shape=jax.ShapeDtypeStruct((B, S, W), BF16),
        scratch_shapes=[
            pltpu.VMEM((S, 2 * LANES), BF16),
            pltpu.VMEM((LANES, LANES), F32),
            pltpu.VMEM((2, LANES, LANES), BF16),
            pltpu.VMEM((2, LANES, LANES), BF16),
        ] + _attention_scratch(S),
        compiler_params=_params(("arbitrary", "arbitrary")),
        name="moba_attention",
    )(qa, ka, va)


def _diff_kernel(lq1_ref, lk1_ref, lq2_ref, lk2_ref, g_ref, q_ref, k_ref, v_ref, o_ref, cbias_ref, s_ref, p_ref):
    S = k_ref.shape[1]
    nb = S // ATT_BLK
    lam = (jnp.exp(jnp.sum(lq1_ref[...] * lk1_ref[...], axis=1, keepdims=True))
           - jnp.exp(jnp.sum(lq2_ref[...] * lk2_ref[...], axis=1, keepdims=True)) + LAMBDA_INIT)
    cbias_ref[...] = _causal_bias()
    lane = lax.broadcasted_iota(I32, (ATT_BLK, LANES), 1)

    for i in reversed(range(nb)):
        r0 = i * ATT_BLK
        q = q_ref[0, r0:r0 + ATT_BLK, :].astype(F32)
        qs = jnp.concatenate([jnp.where(lane < HEAD_DIM, q, 0.0), jnp.where(lane >= HEAD_DIM, q, 0.0)],
                             axis=0).astype(BF16)
        out = _attend(qs, lambda n: k_ref[0, n * ATT_BLK:(n + 1) * ATT_BLK, :], lambda L: v_ref[0, :L, :],
                      s_ref, p_ref, cbias_ref, i)
        o = out[:ATT_BLK] - lam * out[ATT_BLK:]
        o_ref[0, r0:r0 + ATT_BLK, :] = (_rms(o, g_ref[...]) * (1.0 - LAMBDA_INIT)).astype(BF16)


def _diff_attention(qd, kd, vd, lq1, lk1, lq2, lk2, subln_g):
    B, S, W = qd.shape
    assert S % ATT_BLK == 0 and W % LANES == 0
    spec = pl.BlockSpec((1, S, LANES), lambda b, j: (b, 0, j))
    small = pl.BlockSpec((1, HEAD_DIM), lambda b, j: (0, 0))
    return pl.pallas_call(
        _diff_kernel,
        grid=(B, W // LANES),
        in_specs=[small, small, small, small, pl.BlockSpec((1, LANES), lambda b, j: (0, 0)), spec, spec, spec],
        out_specs=spec,
        out_shape=jax.ShapeDtypeStruct((B, S, W), BF16),
        scratch_shapes=_attention_scratch(S),
        compiler_params=_params(("arbitrary", "arbitrary")),
        name="diff_attention",
    )(lq1, lk1, lq2, lk2, subln_g, qd, kd, vd)


def _pack_bf16_pairs(x):
    n = x.shape[1] // 2
    lo = lax.bitcast_convert_type(x[:, :n].astype(BF16).astype(F32), U32)
    hi = lax.bitcast_convert_type(x[:, n:].astype(BF16).astype(F32), U32)
    return (lo >> 16) | hi


def _unpack_bf16_pairs(w, dtype=BF16):
    lo = lax.bitcast_convert_type(w << 16, F32)
    hi = lax.bitcast_convert_type(w & jnp.uint32(0xFFFF0000), F32)
    return jnp.concatenate([lo, hi], axis=1).astype(dtype)


def _merge_kernel(x_ref, oa_ref, od_ref, gmix_ref, wg_ref, wbm_ref, wbd_ref, wout_ref, gffn_ref,
                  wr_ref, br_ref,
                  x1_ref, h2_ref, route_ref, routet_ref, cnt_ref, base_ref, ltri_ref, lg_ref):
    tm, D = x_ref.shape
    step = pl.program_id(0)

    @pl.when(step == 0)
    def _():
        base_ref[...] = jnp.zeros_like(base_ref)
        lg_ref[...] = jnp.zeros_like(lg_ref)
        row = lax.broadcasted_iota(I32, (tm, tm), 0)
        col = lax.broadcasted_iota(I32, (tm, tm), 1)
        ltri_ref[...] = jnp.where(col < row, 1.0, 0.0).astype(BF16)

    lg = lg_ref[...]
    live = step > 0
    lane = lax.broadcasted_iota(I32, (tm, LANES), 1)
    ninf = -jnp.inf

    def first_argmax(vals):
        m = jnp.max(vals, axis=1, keepdims=True)
        return m, jnp.min(jnp.where(vals == m, lane, LANES), axis=1, keepdims=True)

    is_group = lane < N_GROUPS
    gm, gsel = first_argmax(jnp.where(is_group, lg, ninf))
    p_group = 1.0 / jnp.sum(jnp.where(is_group, jnp.exp(lg - gm), 0.0), axis=1, keepdims=True)
    lo = N_GROUPS + gsel * EXPERTS_PER_GROUP
    el = jnp.where((lane >= lo) & (lane < lo + EXPERTS_PER_GROUP), lg, ninf)
    m1, i1 = first_argmax(el)
    m2, i2 = first_argmax(jnp.where(lane == i1, ninf, el))
    e2x = jnp.exp(m2 - m1)
    w1 = p_group / (1.0 + e2x)
    w2 = p_group * e2x / (1.0 + e2x)
    e1 = i1 - N_GROUPS
    e2 = i2 - N_GROUPS

    oh1 = jnp.where((lane == e1) & live, 1.0, 0.0)
    oh2 = jnp.where((lane == e2) & live, 1.0, 0.0)
    cc = _dot(ltri_ref[...], jnp.concatenate([oh1, oh2], axis=1).astype(BF16))
    c1, c2 = cc[:, :LANES], cc[:, LANES:]
    base = base_ref[...]
    tot1 = jnp.sum(oh1, axis=0, keepdims=True)
    tot2 = jnp.sum(oh2, axis=0, keepdims=True)
    r1 = jnp.sum(oh1 * (c1 + base), axis=1, keepdims=True)
    r2 = jnp.sum(oh2 * (c2 + base + tot1), axis=1, keepdims=True)
    base = base + tot1 + tot2
    base_ref[...] = base
    cnt_ref[...] = base

    cols = (e1.astype(F32), e2.astype(F32), w1, w2, r1, r2)
    route = jnp.zeros((tm, LANES), F32)
    for c, val in enumerate(cols):
        route = jnp.where(lane == c, val, route)
    route_ref[...] = route[:, :ROUTE_COLS]
    routet_ref[...] = route.T[:ROUTE_COLS, :]

    x = x_ref[...]
    h = _rms(x, gmix_ref[...]).astype(BF16)
    sig = jax.nn.sigmoid(_dot(h, wg_ref[...]))
    merged = sig[:, :D] * _dot(oa_ref[...], wbm_ref[...]) + sig[:, D:] * _dot(od_ref[...], wbd_ref[...])
    x1 = x + _dot(merged.astype(BF16), wout_ref[...])
    x1_ref[...] = x1
    h2 = _rms(x1, gffn_ref[...])
    h2_ref[...] = _pack_bf16_pairs(h2)
    hhi = h2.astype(BF16)
    hlo = (h2 - hhi.astype(F32)).astype(BF16)
    lg_hi = _dot(hhi, wr_ref[...])
    lg_lo = _dot(hlo, wr_ref[...])
    lg_ref[...] = (lg_hi[:, :LANES] + lg_hi[:, LANES:]) + (lg_lo[:, :LANES] + lg_lo[:, LANES:]) + br_ref[...]


def _merge_route(x2, o_a, o_d, g_mix, w_g, w_bm, w_bd, w_out, g_ffn, w_r2, b_r):
    T, D = x2.shape
    tm = min(MERGE_TILE, T)
    assert T % tm == 0
    nt = T // tm
    W = o_a.shape[1]
    chain = lambda w: pl.BlockSpec((tm, w), lambda i: (jnp.minimum(i, nt - 1), 0))
    routed = lambda i: jnp.maximum(i - 1, 0)
    full = lambda a: pl.BlockSpec(a.shape, lambda i: (0, 0))
    return pl.pallas_call(
        _merge_kernel,
        grid=(nt + 1,),
        in_specs=[chain(D), chain(W), chain(W), full(g_mix), full(w_g), full(w_bm), full(w_bd), full(w_out),
                  full(g_ffn), full(w_r2), full(b_r)],
        out_specs=[chain(D), chain(D // 2), pl.BlockSpec((tm, ROUTE_COLS), lambda i: (routed(i), 0)),
                   pl.BlockSpec((ROUTE_COLS, tm), lambda i: (0, routed(i))),
                   pl.BlockSpec((1, LANES), lambda i: (0, 0))],
        out_shape=[jax.ShapeDtypeStruct((T, D), F32), jax.ShapeDtypeStruct((T, D // 2), U32),
                   jax.ShapeDtypeStruct((T, ROUTE_COLS), F32), jax.ShapeDtypeStruct((ROUTE_COLS, T), F32),
                   jax.ShapeDtypeStruct((1, LANES), F32)],
        scratch_shapes=[pltpu.VMEM((1, LANES), F32), pltpu.VMEM((tm, tm), BF16), pltpu.VMEM((tm, LANES), F32)],
        compiler_params=_params(("arbitrary",)),
        name="merge_route",
    )(x2, o_a, o_d, g_mix, w_g, w_bm, w_bd, w_out, g_ffn, w_r2, b_r)


def _index_prefetch(dest_hbm, idx_ref, isem):
    i = pl.program_id(0)
    per = dest_hbm.shape[1]
    slot = i % 2

    def copy(step, sl):
        return pltpu.make_async_copy(dest_hbm.at[step], idx_ref.at[pl.ds(sl * per, per)], isem.at[sl])

    @pl.when(i == 0)
    def _():
        copy(0, 0).start()

    @pl.when(i + 1 < pl.num_programs(0))
    def _():
        copy(i + 1, 1 - slot).start()

    copy(i, slot).wait()
    return slot * per


def _dispatch_kernel(zs_ref, nu_ref, dest_hbm, h2_hbm, xd_hbm, idx_ref, zero_ref, src_ref, isem, ssem, rsem, zsem):
    tm = h2_hbm.shape[1] * 8
    nblk = xd_hbm.shape[0] // MOE_BLK
    i = pl.program_id(0)
    last = pl.num_programs(0) - 1

    @pl.when(i == 0)
    def _():
        zero_ref[...] = jnp.zeros_like(zero_ref)

        def zcopy(row):
            return pltpu.make_async_copy(zero_ref, xd_hbm.at[pl.ds(pl.multiple_of(row, MOE_BLK), MOE_BLK), :], zsem)

        def tail_start(blk, carry):
            zcopy(blk * MOE_BLK).start()
            return carry

        def tail_wait(blk, carry):
            zcopy(0).wait()
            return carry

        for e in range(N_EXPERTS):
            @pl.when(zs_ref[e] >= 0)
            def _():
                zcopy(zs_ref[e]).start()
        lax.fori_loop(nu_ref[0], nblk, tail_start, 0)
        for e in range(N_EXPERTS):
            @pl.when(zs_ref[e] >= 0)
            def _():
                zcopy(0).wait()
        lax.fori_loop(nu_ref[0], nblk, tail_wait, 0)

    def src_copy(step):
        return pltpu.make_async_copy(h2_hbm.at[step], src_ref.at[step % 3], ssem.at[step % 3])

    @pl.when(i == 0)
    def _():
        src_copy(0).start()

    @pl.when(i < last)
    def _():
        src_copy(i + 1).start()

    base = _index_prefetch(dest_hbm, idx_ref, isem)
    src_copy(i).wait()
    for slot in range(3):
        @pl.when(i % 3 == slot)
        def _():
            for r in range(tm):
                for kk in range(EXPERT_TOPK):
                    d = idx_ref[base + kk * tm + r]
                    pltpu.make_async_copy(src_ref.at[slot, r // 8, pl.ds(r % 8, 1), :],
                                          xd_hbm.at[pl.ds(d, 1), :], rsem.at[i % 2]).start(priority=kk)

    def drain(sem_slot):
        for kk in range(EXPERT_TOPK):
            pltpu.make_async_copy(src_ref.at[0], src_ref.at[0], rsem.at[sem_slot]).wait()

    @pl.when(i > 0)
    def _():
        drain((i - 1) % 2)

    @pl.when(i == last)
    def _():
        drain(i % 2)


def _dispatch(zstart, n_used, dest2, h2p, n_rows):
    T, Wp = h2p.shape
    nt, per = dest2.shape
    tm = per // EXPERT_TOPK
    return pl.pallas_call(
        _dispatch_kernel,
        grid_spec=pltpu.PrefetchScalarGridSpec(
            num_scalar_prefetch=2,
            grid=(nt,),
            in_specs=[pl.BlockSpec(memory_space=pl.ANY), pl.BlockSpec(memory_space=pl.ANY)],
            out_specs=pl.BlockSpec(memory_space=pl.ANY),
            scratch_shapes=[pltpu.SMEM((2 * per,), I32), pltpu.VMEM((MOE_BLK, Wp), U32),
                            pltpu.VMEM((3, tm // 8, 8, Wp), U32),
                            pltpu.SemaphoreType.DMA((2,)), pltpu.SemaphoreType.DMA((3,)),
                            pltpu.SemaphoreType.DMA((2,)), pltpu.SemaphoreType.DMA(())],
        ),
        out_shape=jax.ShapeDtypeStruct((n_rows, Wp), U32),
        compiler_params=_params(("arbitrary",)),
        name="moe_dispatch",
    )(zstart, n_used, dest2, h2p.reshape(nt, tm // 8, 8, Wp))


def _expert_kernel(be_ref, nu_ref, xd_ref, w1_ref, w3_ref, w2_ref, yb_ref, w1b, w3b, w2b):
    i = pl.program_id(0)
    n_used = nu_ref[0]

    @pl.when((i < n_used) & ((i == 0) | (be_ref[i] != be_ref[jnp.maximum(i - 1, 0)])))
    def _():
        w1b[...] = w1_ref[0].astype(BF16)
        w3b[...] = w3_ref[0].astype(BF16)
        w2b[...] = w2_ref[0].astype(BF16)

    @pl.when(i < n_used)
    def _():
        x = _unpack_bf16_pairs(xd_ref[...])
        act = jax.nn.silu(_dot(x, w1b[...])) * _dot(x, w3b[...])
        yb_ref[...] = _pack_bf16_pairs(_dot(act.astype(BF16), w2b[...]))

    @pl.when(i >= n_used)
    def _():
        yb_ref[...] = jnp.zeros_like(yb_ref)


def _experts(blk_expert, n_used, xd, w1, w3, w2):
    P, Wp = xd.shape
    E, D, De = w1.shape
    nblk = P // MOE_BLK
    return pl.pallas_call(
        _expert_kernel,
        grid_spec=pltpu.PrefetchScalarGridSpec(
            num_scalar_prefetch=2,
            grid=(nblk,),
            in_specs=[
                pl.BlockSpec((MOE_BLK, Wp), lambda i, be, nu: (jnp.minimum(i, nu[0] - 1), 0)),
                pl.BlockSpec((1, D, De), lambda i, be, nu: (be[i], 0, 0)),
                pl.BlockSpec((1, D, De), lambda i, be, nu: (be[i], 0, 0)),
                pl.BlockSpec((1, De, D), lambda i, be, nu: (be[i], 0, 0)),
            ],
            out_specs=pl.BlockSpec((MOE_BLK, D // 2), lambda i, be, nu: (i, 0)),
            scratch_shapes=[pltpu.VMEM((D, De), BF16), pltpu.VMEM((D, De), BF16), pltpu.VMEM((De, D), BF16)],
        ),
        out_shape=jax.ShapeDtypeStruct((P, D // 2), U32),
        compiler_params=_params(("arbitrary",), EXPERT_VMEM_LIMIT),
        name="moe_experts",
    )(blk_expert, n_used, xd, w1, w3, w2)


def _combine_kernel(dest_hbm, route_ref, x1_ref, g_ref, yb_hbm, o_ref, idx_ref, buf_ref, isem, rsem):
    tm, D = x1_ref.shape
    per = EXPERT_TOPK * tm
    i = pl.program_id(0)
    last = pl.num_programs(0) - 1

    def idx_copy(step):
        sl = step % 3
        return pltpu.make_async_copy(dest_hbm.at[step], idx_ref.at[pl.ds(sl * per, per)], isem.at[sl])

    def issue(step, bs, j):
        ib = (step % 3) * per
        for u in range(8):
            for kk in range(EXPERT_TOPK):
                d = idx_ref[ib + kk * tm + 8 * j + u]
                pltpu.make_async_copy(yb_hbm.at[pl.ds(d, 1), :], buf_ref.at[bs, kk, j, pl.ds(u, 1), :],
                                      rsem.at[bs]).start(priority=kk)

    def drain(bs):
        for kk in range(EXPERT_TOPK):
            pltpu.make_async_copy(buf_ref.at[bs, kk], buf_ref.at[bs, kk], rsem.at[bs]).wait()

    @pl.when(i == 0)
    def _():
        idx_copy(0).start()

        @pl.when(last > 0)
        def _():
            idx_copy(1).start()

        idx_copy(0).wait()
        for j in range(tm // 8):
            issue(0, 0, j)

    @pl.when(i + 2 <= last)
    def _():
        idx_copy(i + 2).start()

    @pl.when(i < last)
    def _():
        idx_copy(i + 1).wait()

    cur = i % 2
    drain(cur)
    nxt = jnp.minimum(i + 1, last)
    for j in range(tm // 8):
        issue(nxt, 1 - cur, j)
        rows = slice(8 * j, 8 * j + 8)
        route = route_ref[rows, :]
        y0 = _unpack_bf16_pairs(buf_ref[cur, 0, j], F32)
        y1 = _unpack_bf16_pairs(buf_ref[cur, 1, j], F32)
        o_ref[rows, :] = _rms(x1_ref[rows, :] + (route[:, 2:3] * y0 + route[:, 3:4] * y1), g_ref[...])

    @pl.when(i == last)
    def _():
        drain(1 - cur)


def _combine(dest2, route, x1, g_final, yb):
    T, D = x1.shape
    nt, per = dest2.shape
    tm = per // EXPERT_TOPK
    return pl.pallas_call(
        _combine_kernel,
        grid=(nt,),
        in_specs=[pl.BlockSpec(memory_space=pl.ANY), pl.BlockSpec((tm, ROUTE_COLS), lambda i: (i, 0)),
                  pl.BlockSpec((tm, D), lambda i: (i, 0)), pl.BlockSpec((1, D), lambda i: (0, 0)),
                  pl.BlockSpec(memory_space=pl.ANY)],
        out_specs=pl.BlockSpec((tm, D), lambda i: (i, 0)),
        out_shape=jax.ShapeDtypeStruct((T, D), F32),
        scratch_shapes=[pltpu.SMEM((3 * per,), I32), pltpu.VMEM((2, EXPERT_TOPK, tm // 8, 8, D // 2), U32),
                        pltpu.SemaphoreType.DMA((3,)), pltpu.SemaphoreType.DMA((2,))],
        compiler_params=_params(("arbitrary",)),
        name="moe_combine",
    )(dest2, route, x1, g_final, yb)


def _moe(h2p, route, route_t, counts, x1, g_final, w1, w3, w2):
    T = h2p.shape[0]
    A = T * EXPERT_TOPK
    P = -(-A // MOE_BLK) * MOE_BLK + N_EXPERTS * MOE_BLK
    nblk = P // MOE_BLK
    cnt = counts[0, :N_EXPERTS].astype(I32)
    padded = (cnt + MOE_BLK - 1) // MOE_BLK * MOE_BLK
    pends = jnp.cumsum(padded)
    pstarts = pends - padded
    ids = route_t[0:EXPERT_TOPK].astype(I32)
    ranks = route_t[4:4 + EXPERT_TOPK].astype(I32)
    dest = ranks
    for e in range(N_EXPERTS):
        dest = dest + jnp.where(ids == e, pstarts[e], 0)

    def tiled(tm):
        tm = min(tm, T)
        return dest.reshape(EXPERT_TOPK, T // tm, tm).transpose(1, 0, 2).reshape(T // tm, EXPERT_TOPK * tm)

    blk_row = jnp.arange(nblk, dtype=I32) * MOE_BLK
    blk_expert = jnp.minimum(jnp.sum((pends[None, :] <= blk_row[:, None]).astype(I32), axis=1), N_EXPERTS - 1)
    n_used = (pends[-1:] // MOE_BLK).astype(I32)
    zstart = jnp.where(padded > 0, pends - MOE_BLK, -1).astype(I32)
    xd = _dispatch(zstart, n_used, tiled(DISPATCH_TILE), h2p, P)
    yb = _experts(blk_expert, n_used, xd, w1, w3, w2)
    return _combine(tiled(COMBINE_TILE), route, x1, g_final, yb)


def _rope_tables(seq):
    inv = 1.0 / (ROPE_THETA ** (jnp.arange(0, HEAD_DIM, 2, dtype=F32) / HEAD_DIM))
    ang = jnp.arange(seq, dtype=F32)[:, None] * inv[None, :]
    ang = jnp.concatenate([ang, ang], axis=-1)
    sign = jnp.concatenate([-jnp.ones((HEAD_DIM // 2,), F32), jnp.ones((HEAD_DIM // 2,), F32)])
    reps = LANES // HEAD_DIM
    return jnp.tile(jnp.cos(ang), (1, reps)), jnp.tile(jnp.sin(ang) * sign, (1, reps))


def kernel(x, g_mix, w_in, w_branch_moba, w_branch_diff, w_out, diff_lambda_q1, diff_lambda_k1, diff_lambda_q2, diff_lambda_k2, diff_subln_g, g_ffn, w_group, b_group, w_router, b_router, w_expert_gate, w_expert_up, w_expert_down, g_final):
    B, S, D = x.shape
    assert w_in.shape[0] == 1, "one layer"
    T = B * S
    x2 = x.reshape(T, D)
    n_qkv = w_in.shape[2] - 2 * D
    w_qkv = w_in[0, :, :n_qkv].astype(BF16)
    w_g = w_in[0, :, n_qkv:].astype(BF16)
    cos, sin = _rope_tables(S)

    qa, ka, va, qd, kd, vd = _qkv_proj(x2, g_mix, w_qkv, cos, sin, S)
    W = qa.shape[1]
    o_a = _moba_attention(qa.reshape(B, S, W), ka.reshape(B, S, W), va.reshape(B, S, W)).reshape(T, W)
    o_d = _diff_attention(qd.reshape(B, S, W), kd.reshape(B, S, W), vd.reshape(B, S, W),
                          diff_lambda_q1, diff_lambda_k1, diff_lambda_q2, diff_lambda_k2, diff_subln_g).reshape(T, W)

    w_r = jnp.concatenate([w_group[0], w_router[0]], axis=1)
    w_r = jnp.pad(w_r, ((0, 0), (0, LANES - w_r.shape[1])))
    wr_hi = w_r.astype(BF16)
    w_r2 = jnp.concatenate([wr_hi, (w_r - wr_hi.astype(F32)).astype(BF16)], axis=1)
    b_r = jnp.pad(jnp.concatenate([b_group[0], b_router[0]])[None, :], ((0, 0), (0, LANES - N_GROUPS - N_EXPERTS)))

    x1, h2p, route, route_t, counts = _merge_route(
        x2, o_a, o_d, g_mix, w_g, w_branch_moba[0].astype(BF16), w_branch_diff[0].astype(BF16),
        w_out[0].astype(BF16), g_ffn, w_r2, b_r)

    out = _moe(h2p, route, route_t, counts, x1, g_final[None, :],
               w_expert_gate[0], w_expert_up[0], w_expert_down[0])
    return out.reshape(B, S, D)
```

```python
import jax
import jax.numpy as jnp
from jax import lax
from jax.experimental import pallas as pl
from jax.experimental.pallas import tpu as pltpu

F32 = jnp.float32
BF16 = jnp.bfloat16
I32 = jnp.int32
U32 = jnp.uint32

LANES = 128
SUBLANES = 8
HEAD_DIM = 64
ATT_BLK = 256
ATT_BLK_SHIFT = 8
MOBA_TOPK = 3
N_GROUPS = 4
EXPERTS_PER_GROUP = 8
N_EXPERTS = N_GROUPS * EXPERTS_PER_GROUP
EXPERT_TOPK = 2
MOE_BLK = 512
DISPATCH_TILE = 512
COMBINE_TILE = 1024
MERGE_TILE = 1024
MERGE_SPLIT = 2
QKV_TILE = 1024
ROUTE_COLS = 8
EPS = 1e-6
NEG = -1e30
ROPE_THETA = 10000.0
LAMBDA_INIT = 0.8 - 0.6 * 1.0
ATT_SCALE = HEAD_DIM ** -0.5
VMEM_LIMIT = 48 * 1024 * 1024
EXPERT_VMEM_LIMIT = 56 * 1024 * 1024
MERGE_VMEM_LIMIT = 56 * 1024 * 1024


def _dot(a, b):
    return jnp.dot(a, b, preferred_element_type=F32)


def _dot_nt(a, b):
    return lax.dot_general(a, b, (((1,), (1,)), ((), ())), preferred_element_type=F32)


def _rms(x, g):
    var = jnp.mean(x * x, axis=-1, keepdims=True)
    return (x * lax.rsqrt(var + EPS)) * g


def _params(sem, vmem=VMEM_LIMIT):
    return pltpu.CompilerParams(dimension_semantics=sem, vmem_limit_bytes=vmem)


def _qkv_kernel(x_ref, g_ref, w_ref, cos_ref, sin_ref, qa_ref, ka_ref, va_ref, qd_ref, kd_ref, vd_ref):
    h = _rms(x_ref[...], g_ref[...]).astype(BF16)
    cos = cos_ref[...]
    sin = sin_ref[...]
    lane = lax.broadcasted_iota(I32, cos.shape, 1)
    first = (lane & (HEAD_DIM - 1)) < HEAD_DIM // 2
    width = qa_ref.shape[1]
    outs = ((qa_ref, True, ATT_SCALE), (ka_ref, True, 1.0), (va_ref, False, 1.0),
            (qd_ref, True, ATT_SCALE), (kd_ref, True, 1.0), (vd_ref, False, 1.0))
    for j, (o_ref, rope, scale) in enumerate(outs):
        y = _dot(h, w_ref[:, j * width:(j + 1) * width])
        for c in range(width // LANES):
            yc = y[:, c * LANES:(c + 1) * LANES]
            if rope:
                rot = jnp.where(first, pltpu.roll(yc, LANES - HEAD_DIM // 2, 1), pltpu.roll(yc, HEAD_DIM // 2, 1))
                yc = yc * cos + rot * sin
            if scale != 1.0:
                yc = yc * scale
            o_ref[:, c * LANES:(c + 1) * LANES] = yc.astype(BF16)


def _qkv_proj(x2, g_mix, w_qkv, cos, sin, seq):
    T, D = x2.shape
    width = w_qkv.shape[1] // 6
    tm = min(QKV_TILE, seq)
    assert seq % tm == 0 and T % tm == 0
    spt = seq // tm
    out = jax.ShapeDtypeStruct((T, width), BF16)
    return pl.pallas_call(
        _qkv_kernel,
        grid=(T // tm,),
        in_specs=[
            pl.BlockSpec((tm, D), lambda i: (i, 0)),
            pl.BlockSpec((1, D), lambda i: (0, 0)),
            pl.BlockSpec(w_qkv.shape, lambda i: (0, 0)),
            pl.BlockSpec((tm, LANES), lambda i: (i % spt, 0)),
            pl.BlockSpec((tm, LANES), lambda i: (i % spt, 0)),
        ],
        out_specs=[pl.BlockSpec((tm, width), lambda i: (i, 0))] * 6,
        out_shape=[out] * 6,
        compiler_params=_params(("arbitrary",)),
        name="qkv_proj",
    )(x2, g_mix, w_qkv, cos, sin)


ATT_ROWS = 2 * ATT_BLK


def _attend(qs, k_block, v_rows, s_ref, p_ref, cbias_ref, i):
    mx = None
    for n in range(i + 1):
        sb = _dot_nt(qs, k_block(n))
        if n == i:
            sb = sb + cbias_ref[...]
        s_ref[:, n * ATT_BLK:(n + 1) * ATT_BLK] = sb
        for c in range(ATT_BLK // LANES):
            part = sb[:, c * LANES:(c + 1) * LANES]
            mx = part if mx is None else jnp.maximum(mx, part)
    mb = jnp.broadcast_to(jnp.max(mx, axis=1, keepdims=True), (ATT_ROWS, LANES))
    ps = None
    for c in range((i + 1) * ATT_BLK // LANES):
        p = jnp.exp(s_ref[:, c * LANES:(c + 1) * LANES] - mb)
        ps = p if ps is None else ps + p
        p_ref[:, c * LANES:(c + 1) * LANES] = p.astype(BF16)
    l = jnp.sum(ps, axis=1, keepdims=True)
    L = (i + 1) * ATT_BLK
    return _dot(p_ref[:, :L], v_rows(L)) * (1.0 / l)


def _causal_bias():
    row = lax.broadcasted_iota(I32, (ATT_ROWS, ATT_BLK), 0) & (ATT_BLK - 1)
    col = lax.broadcasted_iota(I32, (ATT_ROWS, ATT_BLK), 1)
    return jnp.where(col <= row, 0.0, NEG).astype(F32)


def _attention_scratch(seq):
    return [pltpu.VMEM((ATT_ROWS, ATT_BLK), F32), pltpu.VMEM((ATT_ROWS, seq), F32), pltpu.VMEM((ATT_ROWS, seq), BF16)]


def _moba_kernel(q_ref, k_ref, v_ref, o_ref, kaug_ref, kmf_ref, kmhi_ref, kmlo_ref, cbias_ref, s_ref, p_ref):
    S = k_ref.shape[1]
    nb = S // ATT_BLK
    k = k_ref[0]
    lane_s = lax.broadcasted_iota(I32, (S, LANES), 1)
    blk_s = lax.broadcasted_iota(I32, (S, LANES), 0) >> ATT_BLK_SHIFT
    kaug_ref[:, :LANES] = k
    kaug_ref[:, LANES:] = jnp.where(lane_s == blk_s, 1.0, 0.0).astype(BF16)
    kmean = jnp.mean(k.astype(F32).reshape(nb, ATT_BLK, LANES), axis=1)
    lane_n = lax.broadcasted_iota(I32, (nb, LANES), 1)
    for hh in range(2):
        own_n = (lane_n < HEAD_DIM) if hh == 0 else (lane_n >= HEAD_DIM)
        kmf_ref[...] = jnp.zeros((LANES, LANES), F32)
        kmf_ref[0:nb, :] = jnp.where(own_n, kmean, 0.0)
        km = kmf_ref[...]
        hi = km.astype(BF16)
        kmhi_ref[hh] = hi
        kmlo_ref[hh] = (km - hi.astype(F32)).astype(BF16)
    cbias_ref[...] = _causal_bias()

    lane = lax.broadcasted_iota(I32, (ATT_BLK, LANES), 1)

    def stacked_queries(i):
        q = q_ref[0, i * ATT_BLK:(i + 1) * ATT_BLK, :].astype(F32)
        rows = []
        for hh in range(2):
            own = (lane < HEAD_DIM) if hh == 0 else (lane >= HEAD_DIM)
            qh = jnp.where(own, q, 0.0)
            if i <= MOBA_TOPK:
                rows.append(qh)
                continue
            qhb = qh.astype(BF16)
            gate = (_dot_nt(kmhi_ref[hh], qhb) + _dot_nt(kmlo_ref[hh], qhb))[:SUBLANES, :]
            blk = lax.broadcasted_iota(I32, (SUBLANES, ATT_BLK), 0)
            cand = blk < i
            g = jnp.where(cand, gate, -jnp.inf)
            rank = jnp.zeros((SUBLANES, ATT_BLK), I32)
            for r in range(1, i):
                rank = rank + jnp.where((blk >= r) & (pltpu.roll(g, r, 0) >= g), 1, 0)
                rank = rank + jnp.where((blk < SUBLANES - r) & (pltpu.roll(g, SUBLANES - r, 0) > g), 1, 0)
            drop_t = jnp.where(cand & (rank >= MOBA_TOPK), NEG, 0.0)
            drop = jnp.concatenate([drop_t, jnp.zeros((LANES - SUBLANES, ATT_BLK), F32)], axis=0).T
            rows.append(jnp.concatenate([qh, drop], axis=1))
        return jnp.concatenate(rows, axis=0).astype(BF16)

    for i in reversed(range(nb)):
        if i > MOBA_TOPK:
            k_block = lambda n: kaug_ref[n * ATT_BLK:(n + 1) * ATT_BLK, :]
        else:
            k_block = lambda n: k_ref[0, n * ATT_BLK:(n + 1) * ATT_BLK, :]
        out = _attend(stacked_queries(i), k_block, lambda L: v_ref[0, :L, :], s_ref, p_ref, cbias_ref, i)
        o_ref[0, i * ATT_BLK:(i + 1) * ATT_BLK, :] = jnp.where(
            lane < HEAD_DIM, out[:ATT_BLK], out[ATT_BLK:]).astype(BF16)


def _moba_attention(qa, ka, va):
    B, S, W = qa.shape
    assert S % ATT_BLK == 0 and S // ATT_BLK <= SUBLANES and W % LANES == 0
    spec = pl.BlockSpec((1, S, LANES), lambda b, j: (b, 0, j))
    return pl.pallas_call(
        _moba_kernel,
        grid=(B, W // LANES),
        in_specs=[spec, spec, spec],
        out_specs=spec,
        out_shape=jax.ShapeDtypeStruct((B, S, W), BF16),
        scratch_shapes=[
            pltpu.VMEM((S, 2 * LANES), BF16),
            pltpu.VMEM((LANES, LANES), F32),
            pltpu.VMEM((2, LANES, LANES), BF16),
            pltpu.VMEM((2, LANES, LANES), BF16),
        ] + _attention_scratch(S),
        compiler_params=_params(("arbitrary", "arbitrary")),
        name="moba_attention",
    )(qa, ka, va)


def _diff_kernel(lq1_ref, lk1_ref, lq2_ref, lk2_ref, g_ref, q_ref, k_ref, v_ref, o_ref, cbias_ref, s_ref, p_ref):
    S = k_ref.shape[1]
    nb = S // ATT_BLK
    lam = (jnp.exp(jnp.sum(lq1_ref[...] * lk1_ref[...], axis=1, keepdims=True))
           - jnp.exp(jnp.sum(lq2_ref[...] * lk2_ref[...], axis=1, keepdims=True)) + LAMBDA_INIT)
    cbias_ref[...] = _causal_bias()
    lane = lax.broadcasted_iota(I32, (ATT_BLK, LANES), 1)

    for i in reversed(range(nb)):
        r0 = i * ATT_BLK
        q = q_ref[0, r0:r0 + ATT_BLK, :].astype(F32)
        qs = jnp.concatenate([jnp.where(lane < HEAD_DIM, q, 0.0), jnp.where(lane >= HEAD_DIM, q, 0.0)],
                             axis=0).astype(BF16)
        out = _attend(qs, lambda n: k_ref[0, n * ATT_BLK:(n + 1) * ATT_BLK, :], lambda L: v_ref[0, :L, :],
                      s_ref, p_ref, cbias_ref, i)
        o = out[:ATT_BLK] - lam * out[ATT_BLK:]
        o_ref[0, r0:r0 + ATT_BLK, :] = (_rms(o, g_ref[...]) * (1.0 - LAMBDA_INIT)).astype(BF16)


def _diff_attention(qd, kd, vd, lq1, lk1, lq2, lk2, subln_g):
    B, S, W = qd.shape
    assert S % ATT_BLK == 0 and W % LANES == 0
    spec = pl.BlockSpec((1, S, LANES), lambda b, j: (b, 0, j))
    small = pl.BlockSpec((1, HEAD_DIM), lambda b, j: (0, 0))
    return pl.pallas_call(
        _diff_kernel,
        grid=(B, W // LANES),
        in_specs=[small, small, small, small, pl.BlockSpec((1, LANES), lambda b, j: (0, 0)), spec, spec, spec],
        out_specs=spec,
        out_shape=jax.ShapeDtypeStruct((B, S, W), BF16),
        scratch_shapes=_attention_scratch(S),
        compiler_params=_params(("arbitrary", "arbitrary")),
        name="diff_attention",
    )(lq1, lk1, lq2, lk2, subln_g, qd, kd, vd)


def _pack_bf16_pairs(x):
    n = x.shape[1] // 2
    lo = lax.bitcast_convert_type(x[:, :n].astype(BF16).astype(F32), U32)
    hi = lax.bitcast_convert_type(x[:, n:].astype(BF16).astype(F32), U32)
    return (lo >> 16) | hi


def _unpack_bf16_pairs(w, dtype=BF16):
    lo = lax.bitcast_convert_type(w << 16, F32)
    hi = lax.bitcast_convert_type(w & jnp.uint32(0xFFFF0000), F32)
    return jnp.concatenate([lo, hi], axis=1).astype(dtype)


def _merge_rows(rows, first, x_ref, oa_ref, od_ref, gmix_ref, wg_ref, wbm_ref, wbd_ref, wout_ref, gffn_ref,
                wr_ref, br_ref,
                x1_ref, h2_ref, route_ref, routet_ref, cnt_ref, base_ref, ltri_ref, lg_ref):
    tm, D = rows.stop - rows.start, x_ref.shape[1]
    step = pl.program_id(0)

    @pl.when((step == 0) & first)
    def _():
        base_ref[...] = jnp.zeros_like(base_ref)
        lg_ref[...] = jnp.zeros_like(lg_ref)
        row = lax.broadcasted_iota(I32, (tm, tm), 0)
        col = lax.broadcasted_iota(I32, (tm, tm), 1)
        ltri_ref[...] = jnp.where(col < row, 1.0, 0.0).astype(BF16)

    lg = lg_ref[rows, :]
    live = step > 0
    lane = lax.broadcasted_iota(I32, (tm, LANES), 1)
    ninf = -jnp.inf

    def first_argmax(vals):
        m = jnp.max(vals, axis=1, keepdims=True)
        return m, jnp.min(jnp.where(vals == m, lane, LANES), axis=1, keepdims=True)

    is_group = lane < N_GROUPS
    gm, gsel = first_argmax(jnp.where(is_group, lg, ninf))
    p_group = 1.0 / jnp.sum(jnp.where(is_group, jnp.exp(lg - gm), 0.0), axis=1, keepdims=True)
    lo = N_GROUPS + gsel * EXPERTS_PER_GROUP
    el = jnp.where((lane >= lo) & (lane < lo + EXPERTS_PER_GROUP), lg, ninf)
    m1, i1 = first_argmax(el)
    m2, i2 = first_argmax(jnp.where(lane == i1, ninf, el))
    e2x = jnp.exp(m2 - m1)
    w1 = p_group / (1.0 + e2x)
    w2 = p_group * e2x / (1.0 + e2x)
    e1 = i1 - N_GROUPS
    e2 = i2 - N_GROUPS

    oh1 = jnp.where((lane == e1) & live, 1.0, 0.0)
    oh2 = jnp.where((lane == e2) & live, 1.0, 0.0)
    cc = _dot(ltri_ref[...], jnp.concatenate([oh1, oh2], axis=1).astype(BF16))
    c1, c2 = cc[:, :LANES], cc[:, LANES:]
    base = base_ref[...]
    tot1 = jnp.sum(oh1, axis=0, keepdims=True)
    tot2 = jnp.sum(oh2, axis=0, keepdims=True)
    r1 = jnp.sum(oh1 * (c1 + base), axis=1, keepdims=True)
    r2 = jnp.sum(oh2 * (c2 + base + tot1), axis=1, keepdims=True)
    base = base + tot1 + tot2
    base_ref[...] = base
    cnt_ref[...] = base

    cols = (e1.astype(F32), e2.astype(F32), w1, w2, r1, r2)
    route = jnp.zeros((tm, LANES), F32)
    for c, val in enumerate(cols):
        route = jnp.where(lane == c, val, route)
    route_ref[rows, :] = route[:, :ROUTE_COLS]
    routet_ref[:, rows] = route.T[:ROUTE_COLS, :]

    x = x_ref[rows, :]
    h = _rms(x, gmix_ref[...]).astype(BF16)
    sig = jax.nn.sigmoid(_dot(h, wg_ref[...]))
    merged = sig[:, :D] * _dot(oa_ref[rows, :], wbm_ref[...]) + sig[:, D:] * _dot(od_ref[rows, :], wbd_ref[...])
    x1 = x + _dot(merged.astype(BF16), wout_ref[...])
    x1_ref[rows, :] = x1
    h2 = _rms(x1, gffn_ref[...])
    h2_ref[rows, :] = _pack_bf16_pairs(h2)
    hhi = h2.astype(BF16)
    hlo = (h2 - hhi.astype(F32)).astype(BF16)
    lg_hi = _dot(hhi, wr_ref[...])
    lg_lo = _dot(hlo, wr_ref[...])
    lg_ref[rows, :] = (lg_hi[:, :LANES] + lg_hi[:, LANES:]) + (lg_lo[:, :LANES] + lg_lo[:, LANES:]) + br_ref[...]


def _merge_kernel(*refs):
    part = refs[0].shape[0] // MERGE_SPLIT
    for g in range(MERGE_SPLIT):
        _merge_rows(slice(g * part, (g + 1) * part), g == 0, *refs)


def _merge_route(x2, o_a, o_d, g_mix, w_g, w_bm, w_bd, w_out, g_ffn, w_r2, b_r):
    T, D = x2.shape
    tm = min(MERGE_TILE, T)
    assert T % tm == 0 and tm % (MERGE_SPLIT * SUBLANES) == 0
    nt = T // tm
    W = o_a.shape[1]
    chain = lambda w: pl.BlockSpec((tm, w), lambda i: (jnp.minimum(i, nt - 1), 0))
    routed = lambda i: jnp.maximum(i - 1, 0)
    full = lambda a: pl.BlockSpec(a.shape, lambda i: (0, 0))
    return pl.pallas_call(
        _merge_kernel,
        grid=(nt + 1,),
        in_specs=[chain(D), chain(W), chain(W), full(g_mix), full(w_g), full(w_bm), full(w_bd), full(w_out),
                  full(g_ffn), full(w_r2), full(b_r)],
        out_specs=[chain(D), chain(D // 2), pl.BlockSpec((tm, ROUTE_COLS), lambda i: (routed(i), 0)),
                   pl.BlockSpec((ROUTE_COLS, tm), lambda i: (0, routed(i))),
                   pl.BlockSpec((1, LANES), lambda i: (0, 0))],
        out_shape=[jax.ShapeDtypeStruct((T, D), F32), jax.ShapeDtypeStruct((T, D // 2), U32),
                   jax.ShapeDtypeStruct((T, ROUTE_COLS), F32), jax.ShapeDtypeStruct((ROUTE_COLS, T), F32),
                   jax.ShapeDtypeStruct((1, LANES), F32)],
        scratch_shapes=[pltpu.VMEM((1, LANES), F32), pltpu.VMEM((tm // MERGE_SPLIT, tm // MERGE_SPLIT), BF16),
                        pltpu.VMEM((tm, LANES), F32)],
        compiler_params=_params(("arbitrary",), MERGE_VMEM_LIMIT),
        name="merge_route",
    )(x2, o_a, o_d, g_mix, w_g, w_bm, w_bd, w_out, g_ffn, w_r2, b_r)


def _index_prefetch(dest_hbm, idx_ref, isem):
    i = pl.program_id(0)
    per = dest_hbm.shape[1]
    slot = i % 2

    def copy(step, sl):
        return pltpu.make_async_copy(dest_hbm.at[step], idx_ref.at[pl.ds(sl * per, per)], isem.at[sl])

    @pl.when(i == 0)
    def _():
        copy(0, 0).start()

    @pl.when(i + 1 < pl.num_programs(0))
    def _():
        copy(i + 1, 1 - slot).start()

    copy(i, slot).wait()
    return slot * per


def _dispatch_kernel(zs_ref, nu_ref, dest_hbm, h2_hbm, xd_hbm, idx_ref, zero_ref, src_ref, isem, ssem, rsem, zsem):
    tm = h2_hbm.shape[1] * 8
    nblk = xd_hbm.shape[0] // MOE_BLK
    i = pl.program_id(0)
    last = pl.num_programs(0) - 1

    @pl.when(i == 0)
    def _():
        zero_ref[...] = jnp.zeros_like(zero_ref)

        def zcopy(row):
            return pltpu.make_async_copy(zero_ref, xd_hbm.at[pl.ds(pl.multiple_of(row, MOE_BLK), MOE_BLK), :], zsem)

        def tail_start(blk, carry):
            zcopy(blk * MOE_BLK).start()
            return carry

        def tail_wait(blk, carry):
            zcopy(0).wait()
            return carry

        for e in range(N_EXPERTS):
            @pl.when(zs_ref[e] >= 0)
            def _():
                zcopy(zs_ref[e]).start()
        lax.fori_loop(nu_ref[0], nblk, tail_start, 0)
        for e in range(N_EXPERTS):
            @pl.when(zs_ref[e] >= 0)
            def _():
                zcopy(0).wait()
        lax.fori_loop(nu_ref[0], nblk, tail_wait, 0)

    def src_copy(step):
        return pltpu.make_async_copy(h2_hbm.at[step], src_ref.at[step % 3], ssem.at[step % 3])

    @pl.when(i == 0)
    def _():
        src_copy(0).start()

    @pl.when(i < last)
    def _():
        src_copy(i + 1).start()

    base = _index_prefetch(dest_hbm, idx_ref, isem)
    src_copy(i).wait()
    for slot in range(3):
        @pl.when(i % 3 == slot)
        def _():
            for r in range(tm):
                for kk in range(EXPERT_TOPK):
                    d = idx_ref[base + kk * tm + r]
                    pltpu.make_async_copy(src_ref.at[slot, r // 8, pl.ds(r % 8, 1), :],
                                          xd_hbm.at[pl.ds(d, 1), :], rsem.at[i % 2]).start(priority=kk)

    def drain(sem_slot):
        for kk in range(EXPERT_TOPK):
            pltpu.make_async_copy(src_ref.at[0], src_ref.at[0], rsem.at[sem_slot]).wait()

    @pl.when(i > 0)
    def _():
        drain((i - 1) % 2)

    @pl.when(i == last)
    def _():
        drain(i % 2)


def _dispatch(zstart, n_used, dest2, h2p, n_rows):
    T, Wp = h2p.shape
    nt, per = dest2.shape
    tm = per // EXPERT_TOPK
    return pl.pallas_call(
        _dispatch_kernel,
        grid_spec=pltpu.PrefetchScalarGridSpec(
            num_scalar_prefetch=2,
            grid=(nt,),
            in_specs=[pl.BlockSpec(memory_space=pl.ANY), pl.BlockSpec(memory_space=pl.ANY)],
            out_specs=pl.BlockSpec(memory_space=pl.ANY),
            scratch_shapes=[pltpu.SMEM((2 * per,), I32), pltpu.VMEM((MOE_BLK, Wp), U32),
                            pltpu.VMEM((3, tm // 8, 8, Wp), U32),
                            pltpu.SemaphoreType.DMA((2,)), pltpu.SemaphoreType.DMA((3,)),
                            pltpu.SemaphoreType.DMA((2,)), pltpu.SemaphoreType.DMA(())],
        ),
        out_shape=jax.ShapeDtypeStruct((n_rows, Wp), U32),
        compiler_params=_params(("arbitrary",)),
        name="moe_dispatch",
    )(zstart, n_used, dest2, h2p.reshape(nt, tm // 8, 8, Wp))


def _expert_kernel(be_ref, nu_ref, xd_ref, w1_ref, w3_ref, w2_ref, yb_ref, w1b, w3b, w2b):
    i = pl.program_id(0)
    n_used = nu_ref[0]

    @pl.when((i < n_used) & ((i == 0) | (be_ref[i] != be_ref[jnp.maximum(i - 1, 0)])))
    def _():
        w1b[...] = w1_ref[0].astype(BF16)
        w3b[...] = w3_ref[0].astype(BF16)
        w2b[...] = w2_ref[0].astype(BF16)

    @pl.when(i < n_used)
    def _():
        x = _unpack_bf16_pairs(xd_ref[...])
        act = jax.nn.silu(_dot(x, w1b[...])) * _dot(x, w3b[...])
        yb_ref[...] = _pack_bf16_pairs(_dot(act.astype(BF16), w2b[...]))

    @pl.when(i >= n_used)
    def _():
        yb_ref[...] = jnp.zeros_like(yb_ref)


def _experts(blk_expert, n_used, xd, w1, w3, w2):
    P, Wp = xd.shape
    E, D, De = w1.shape
    nblk = P // MOE_BLK
    return pl.pallas_call(
        _expert_kernel,
        grid_spec=pltpu.PrefetchScalarGridSpec(
            num_scalar_prefetch=2,
            grid=(nblk,),
            in_specs=[
                pl.BlockSpec((MOE_BLK, Wp), lambda i, be, nu: (jnp.minimum(i, nu[0] - 1), 0)),
                pl.BlockSpec((1, D, De), lambda i, be, nu: (be[i], 0, 0)),
                pl.BlockSpec((1, D, De), lambda i, be, nu: (be[i], 0, 0)),
                pl.BlockSpec((1, De, D), lambda i, be, nu: (be[i], 0, 0)),
            ],
            out_specs=pl.BlockSpec((MOE_BLK, D // 2), lambda i, be, nu: (i, 0)),
            scratch_shapes=[pltpu.VMEM((D, De), BF16), pltpu.VMEM((D, De), BF16), pltpu.VMEM((De, D), BF16)],
        ),
        out_shape=jax.ShapeDtypeStruct((P, D // 2), U32),
        compiler_params=_params(("arbitrary",), EXPERT_VMEM_LIMIT),
        name="moe_experts",
    )(blk_expert, n_used, xd, w1, w3, w2)


def _combine_kernel(dest_hbm, route_ref, x1_ref, g_ref, yb_hbm, o_ref, idx_ref, buf_ref, isem, rsem):
    tm, D = x1_ref.shape
    per = EXPERT_TOPK * tm
    i = pl.program_id(0)
    last = pl.num_programs(0) - 1

    def idx_copy(step):
        sl = step % 3
        return pltpu.make_async_copy(dest_hbm.at[step], idx_ref.at[pl.ds(sl * per, per)], isem.at[sl])

    def issue(step, bs, j):
        ib = (step % 3) * per
        for u in range(8):
            for kk in range(EXPERT_TOPK):
                d = idx_ref[ib + kk * tm + 8 * j + u]
                pltpu.make_async_copy(yb_hbm.at[pl.ds(d, 1), :], buf_ref.at[bs, kk, j, pl.ds(u, 1), :],
                                      rsem.at[bs]).start(priority=kk)

    def drain(bs):
        for kk in range(EXPERT_TOPK):
            pltpu.make_async_copy(buf_ref.at[bs, kk], buf_ref.at[bs, kk], rsem.at[bs]).wait()

    @pl.when(i == 0)
    def _():
        idx_copy(0).start()

        @pl.when(last > 0)
        def _():
            idx_copy(1).start()

        idx_copy(0).wait()
        for j in range(tm // 8):
            issue(0, 0, j)

    @pl.when(i + 2 <= last)
    def _():
        idx_copy(i + 2).start()

    @pl.when(i < last)
    def _():
        idx_copy(i + 1).wait()

    cur = i % 2
    drain(cur)
    nxt = jnp.minimum(i + 1, last)
    for j in range(tm // 8):
        issue(nxt, 1 - cur, j)
        rows = slice(8 * j, 8 * j + 8)
        route = route_ref[rows, :]
        y0 = _unpack_bf16_pairs(buf_ref[cur, 0, j], F32)
        y1 = _unpack_bf16_pairs(buf_ref[cur, 1, j], F32)
        o_ref[rows, :] = _rms(x1_ref[rows, :] + (route[:, 2:3] * y0 + route[:, 3:4] * y1), g_ref[...])

    @pl.when(i == last)
    def _():
        drain(1 - cur)


def _combine(dest2, route, x1, g_final, yb):
    T, D = x1.shape
    nt, per = dest2.shape
    tm = per // EXPERT_TOPK
    return pl.pallas_call(
        _combine_kernel,
        grid=(nt,),
        in_specs=[pl.BlockSpec(memory_space=pl.ANY), pl.BlockSpec((tm, ROUTE_COLS), lambda i: (i, 0)),
                  pl.BlockSpec((tm, D), lambda i: (i, 0)), pl.BlockSpec((1, D), lambda i: (0, 0)),
                  pl.BlockSpec(memory_space=pl.ANY)],
        out_specs=pl.BlockSpec((tm, D), lambda i: (i, 0)),
        out_shape=jax.ShapeDtypeStruct((T, D), F32),
        scratch_shapes=[pltpu.SMEM((3 * per,), I32), pltpu.VMEM((2, EXPERT_TOPK, tm // 8, 8, D // 2), U32),
                        pltpu.SemaphoreType.DMA((3,)), pltpu.SemaphoreType.DMA((2,))],
        compiler_params=_params(("arbitrary",)),
        name="moe_combine",
    )(dest2, route, x1, g_final, yb)


def _moe(h2p, route, route_t, counts, x1, g_final, w1, w3, w2):
    T = h2p.shape[0]
    A = T * EXPERT_TOPK
    P = -(-A // MOE_BLK) * MOE_BLK + N_EXPERTS * MOE_BLK
    nblk = P // MOE_BLK
    cnt = counts[0, :N_EXPERTS].astype(I32)
    padded = (cnt + MOE_BLK - 1) // MOE_BLK * MOE_BLK
    pends = jnp.cumsum(padded)
    pstarts = pends - padded
    ids = route_t[0:EXPERT_TOPK].astype(I32)
    ranks = route_t[4:4 + EXPERT_TOPK].astype(I32)
    dest = ranks
    for e in range(N_EXPERTS):
        dest = dest + jnp.where(ids == e, pstarts[e], 0)

    def tiled(tm):
        tm = min(tm, T)
        return dest.reshape(EXPERT_TOPK, T // tm, tm).transpose(1, 0, 2).reshape(T // tm, EXPERT_TOPK * tm)

    blk_row = jnp.arange(nblk, dtype=I32) * MOE_BLK
    blk_expert = jnp.minimum(jnp.sum((pends[None, :] <= blk_row[:, None]).astype(I32), axis=1), N_EXPERTS - 1)
    n_used = (pends[-1:] // MOE_BLK).astype(I32)
    zstart = jnp.where(padded > 0, pends - MOE_BLK, -1).astype(I32)
    xd = _dispatch(zstart, n_used, tiled(DISPATCH_TILE), h2p, P)
    yb = _experts(blk_expert, n_used, xd, w1, w3, w2)
    return _combine(tiled(COMBINE_TILE), route, x1, g_final, yb)


def _rope_tables(seq):
    inv = 1.0 / (ROPE_THETA ** (jnp.arange(0, HEAD_DIM, 2, dtype=F32) / HEAD_DIM))
    ang = jnp.arange(seq, dtype=F32)[:, None] * inv[None, :]
    ang = jnp.concatenate([ang, ang], axis=-1)
    sign = jnp.concatenate([-jnp.ones((HEAD_DIM // 2,), F32), jnp.ones((HEAD_DIM // 2,), F32)])
    reps = LANES // HEAD_DIM
    return jnp.tile(jnp.cos(ang), (1, reps)), jnp.tile(jnp.sin(ang) * sign, (1, reps))


def kernel(x, g_mix, w_in, w_branch_moba, w_branch_diff, w_out, diff_lambda_q1, diff_lambda_k1, diff_lambda_q2, diff_lambda_k2, diff_subln_g, g_ffn, w_group, b_group, w_router, b_router, w_expert_gate, w_expert_up, w_expert_down, g_final):
    B, S, D = x.shape
    assert w_in.shape[0] == 1, "one layer"
    T = B * S
    x2 = x.reshape(T, D)
    n_qkv = w_in.shape[2] - 2 * D
    w_qkv = w_in[0, :, :n_qkv].astype(BF16)
    w_g = w_in[0, :, n_qkv:].astype(BF16)
    cos, sin = _rope_tables(S)

    qa, ka, va, qd, kd, vd = _qkv_proj(x2, g_mix, w_qkv, cos, sin, S)
    W = qa.shape[1]
    o_a = _moba_attention(qa.reshape(B, S, W), ka.reshape(B, S, W), va.reshape(B, S, W)).reshape(T, W)
    o_d = _diff_attention(qd.reshape(B, S, W), kd.reshape(B, S, W), vd.reshape(B, S, W),
                          diff_lambda_q1, diff_lambda_k1, diff_lambda_q2, diff_lambda_k2, diff_subln_g).reshape(T, W)

    w_r = jnp.concatenate([w_group[0], w_router[0]], axis=1)
    w_r = jnp.pad(w_r, ((0, 0), (0, LANES - w_r.shape[1])))
    wr_hi = w_r.astype(BF16)
    w_r2 = jnp.concatenate([wr_hi, (w_r - wr_hi.astype(F32)).astype(BF16)], axis=1)
    b_r = jnp.pad(jnp.concatenate([b_group[0], b_router[0]])[None, :], ((0, 0), (0, LANES - N_GROUPS - N_EXPERTS)))

    x1, h2p, route, route_t, counts = _merge_route(
        x2, o_a, o_d, g_mix, w_g, w_branch_moba[0].astype(BF16), w_branch_diff[0].astype(BF16),
        w_out[0].astype(BF16), g_ffn, w_r2, b_r)

    out = _moe(h2p, route, route_t, counts, x1, g_final[None, :],
               w_expert_gate[0], w_expert_up[0], w_expert_down[0])
    return out.reshape(B, S, D)
```

```python
import jax
import jax.numpy as jnp
from jax import lax
from jax.experimental import pallas as pl
from jax.experimental.pallas import tpu as pltpu

F32 = jnp.float32
BF16 = jnp.bfloat16
I32 = jnp.int32
U32 = jnp.uint32

LANES = 128
SUBLANES = 8
HEAD_DIM = 64
ATT_BLK = 256
ATT_BLK_SHIFT = 8
MOBA_TOPK = 3
N_GROUPS = 4
EXPERTS_PER_GROUP = 8
N_EXPERTS = N_GROUPS * EXPERTS_PER_GROUP
EXPERT_TOPK = 2
MOE_BLK = 512
DISPATCH_TILE = 512
COMBINE_TILE = 1024
MERGE_TILE = 1024
MERGE_SPLIT = 2
QKV_TILE = 1024
ROUTE_COLS = 8
EPS = 1e-6
NEG = -1e30
ROPE_THETA = 10000.0
LAMBDA_INIT = 0.8 - 0.6 * 1.0
ATT_SCALE = HEAD_DIM ** -0.5
VMEM_LIMIT = 48 * 1024 * 1024
EXPERT_VMEM_LIMIT = 56 * 1024 * 1024
MERGE_VMEM_LIMIT = 56 * 1024 * 1024


def _dot(a, b):
    return jnp.dot(a, b, preferred_element_type=F32)


def _dot_nt(a, b):
    return lax.dot_general(a, b, (((1,), (1,)), ((), ())), preferred_element_type=F32)


def _rms(x, g):
    var = jnp.mean(x * x, axis=-1, keepdims=True)
    return (x * lax.rsqrt(var + EPS)) * g


def _params(sem, vmem=VMEM_LIMIT):
    return pltpu.CompilerParams(dimension_semantics=sem, vmem_limit_bytes=vmem)


def _qkv_kernel(x_ref, g_ref, w_ref, cos_ref, sin_ref, qa_ref, ka_ref, va_ref, qd_ref, kd_ref, vd_ref):
    h = _rms(x_ref[...], g_ref[...]).astype(BF16)
    cos = cos_ref[...]
    sin = sin_ref[...]
    lane = lax.broadcasted_iota(I32, cos.shape, 1)
    first = (lane & (HEAD_DIM - 1)) < HEAD_DIM // 2
    width = qa_ref.shape[1]
    outs = ((qa_ref, True, ATT_SCALE), (ka_ref, True, 1.0), (va_ref, False, 1.0),
            (qd_ref, True, ATT_SCALE), (kd_ref, True, 1.0), (vd_ref, False, 1.0))
    for j, (o_ref, rope, scale) in enumerate(outs):
        y = _dot(h, w_ref[:, j * width:(j + 1) * width])
        for c in range(width // LANES):
            yc = y[:, c * LANES:(c + 1) * LANES]
            if rope:
                rot = jnp.where(first, pltpu.roll(yc, LANES - HEAD_DIM // 2, 1), pltpu.roll(yc, HEAD_DIM // 2, 1))
                yc = yc * cos + rot * sin
            if scale != 1.0:
                yc = yc * scale
            o_ref[:, c * LANES:(c + 1) * LANES] = yc.astype(BF16)


def _qkv_proj(x2, g_mix, w_qkv, cos, sin, seq):
    T, D = x2.shape
    width = w_qkv.shape[1] // 6
    tm = min(QKV_TILE, seq)
    assert seq % tm == 0 and T % tm == 0
    spt = seq // tm
    out = jax.ShapeDtypeStruct((T, width), BF16)
    return pl.pallas_call(
        _qkv_kernel,
        grid=(T // tm,),
        in_specs=[
            pl.BlockSpec((tm, D), lambda i: (i, 0)),
            pl.BlockSpec((1, D), lambda i: (0, 0)),
            pl.BlockSpec(w_qkv.shape, lambda i: (0, 0)),
            pl.BlockSpec((tm, LANES), lambda i: (i % spt, 0)),
            pl.BlockSpec((tm, LANES), lambda i: (i % spt, 0)),
        ],
        out_specs=[pl.BlockSpec((tm, width), lambda i: (i, 0))] * 6,
        out_shape=[out] * 6,
        compiler_params=_params(("arbitrary",)),
        name="qkv_proj",
    )(x2, g_mix, w_qkv, cos, sin)


ATT_ROWS = 2 * ATT_BLK


def _attend(qs, k_block, v_rows, s_ref, p_ref, cbias_ref, i):
    mx = None
    for n in range(i + 1):
        sb = _dot_nt(qs, k_block(n))
        if n == i:
            sb = sb + cbias_ref[...]
        s_ref[:, n * ATT_BLK:(n + 1) * ATT_BLK] = sb
        for c in range(ATT_BLK // LANES):
            part = sb[:, c * LANES:(c + 1) * LANES]
            mx = part if mx is None else jnp.maximum(mx, part)
    mb = jnp.broadcast_to(jnp.max(mx, axis=1, keepdims=True), (ATT_ROWS, LANES))
    ps = None
    for c in range((i + 1) * ATT_BLK // LANES):
        p = jnp.exp(s_ref[:, c * LANES:(c + 1) * LANES] - mb)
        ps = p if ps is None else ps + p
        p_ref[:, c * LANES:(c + 1) * LANES] = p.astype(BF16)
    l = jnp.sum(ps, axis=1, keepdims=True)
    L = (i + 1) * ATT_BLK
    return _dot(p_ref[:, :L], v_rows(L)) * (1.0 / l)


def _causal_bias():
    row = lax.broadcasted_iota(I32, (ATT_ROWS, ATT_BLK), 0) & (ATT_BLK - 1)
    col = lax.broadcasted_iota(I32, (ATT_ROWS, ATT_BLK), 1)
    return jnp.where(col <= row, 0.0, NEG).astype(F32)


def _attention_scratch(seq):
    return [pltpu.VMEM((ATT_ROWS, ATT_BLK), F32), pltpu.VMEM((ATT_ROWS, seq), F32), pltpu.VMEM((ATT_ROWS, seq), BF16)]


def _moba_kernel(q_ref, k_ref, v_ref, o_ref, kaug_ref, kmf_ref, kmhi_ref, kmlo_ref, cbias_ref, s_ref, p_ref):
    S = k_ref.shape[1]
    nb = S // ATT_BLK
    k = k_ref[0]
    lane_s = lax.broadcasted_iota(I32, (S, LANES), 1)
    blk_s = lax.broadcasted_iota(I32, (S, LANES), 0) >> ATT_BLK_SHIFT
    kaug_ref[:, :LANES] = k
    kaug_ref[:, LANES:] = jnp.where(lane_s == blk_s, 1.0, 0.0).astype(BF16)
    kmean = jnp.mean(k.astype(F32).reshape(nb, ATT_BLK, LANES), axis=1)
    lane_n = lax.broadcasted_iota(I32, (nb, LANES), 1)
    for hh in range(2):
        own_n = (lane_n < HEAD_DIM) if hh == 0 else (lane_n >= HEAD_DIM)
        kmf_ref[...] = jnp.zeros((LANES, LANES), F32)
        kmf_ref[0:nb, :] = jnp.where(own_n, kmean, 0.0)
        km = kmf_ref[...]
        hi = km.astype(BF16)
        kmhi_ref[hh] = hi
        kmlo_ref[hh] = (km - hi.astype(F32)).astype(BF16)
    cbias_ref[...] = _causal_bias()

    lane = lax.broadcasted_iota(I32, (ATT_BLK, LANES), 1)

    def stacked_queries(i):
        q = q_ref[0, i * ATT_BLK:(i + 1) * ATT_BLK, :].astype(F32)
        rows = []
        for hh in range(2):
            own = (lane < HEAD_DIM) if hh == 0 else (lane >= HEAD_DIM)
            qh = jnp.where(own, q, 0.0)
            if i <= MOBA_TOPK:
                rows.append(qh)
                continue
            qhb = qh.astype(BF16)
            gate = (_dot_nt(kmhi_ref[hh], qhb) + _dot_nt(kmlo_ref[hh], qhb))[:SUBLANES, :]
            blk = lax.broadcasted_iota(I32, (SUBLANES, ATT_BLK), 0)
            cand = blk < i
            g = jnp.where(cand, gate, -jnp.inf)
            rank = jnp.zeros((SUBLANES, ATT_BLK), I32)
            for r in range(1, i):
                rank = rank + jnp.where((blk >= r) & (pltpu.roll(g, r, 0) >= g), 1, 0)
                rank = rank + jnp.where((blk < SUBLANES - r) & (pltpu.roll(g, SUBLANES - r, 0) > g), 1, 0)
            drop_t = jnp.where(cand & (rank >= MOBA_TOPK), NEG, 0.0)
            drop = jnp.concatenate([drop_t, jnp.zeros((LANES - SUBLANES, ATT_BLK), F32)], axis=0).T
            rows.append(jnp.concatenate([qh, drop], axis=1))
        return jnp.concatenate(rows, axis=0).astype(BF16)

    for i in reversed(range(nb)):
        if i > MOBA_TOPK:
            k_block = lambda n: kaug_ref[n * ATT_BLK:(n + 1) * ATT_BLK, :]
        else:
            k_block = lambda n: k_ref[0, n * ATT_BLK:(n + 1) * ATT_BLK, :]
        out = _attend(stacked_queries(i), k_block, lambda L: v_ref[0, :L, :], s_ref, p_ref, cbias_ref, i)
        o_ref[0, i * ATT_BLK:(i + 1) * ATT_BLK, :] = jnp.where(
            lane < HEAD_DIM, out[:ATT_BLK], out[ATT_BLK:]).astype(BF16)


def _moba_attention(qa, ka, va):
    B, S, W = qa.shape
    assert S % ATT_BLK == 0 and S // ATT_BLK <= SUBLANES and W % LANES == 0
    spec = pl.BlockSpec((1, S, LANES), lambda b, j: (b, 0, j))
    return pl.pallas_call(
        _moba_kernel,
        grid=(B, W // LANES),
        in_specs=[spec, spec, spec],
        out_specs=spec,
        out_shape=jax.ShapeDtypeStruct((B, S, W), BF16),
        scratch_shapes=[
            pltpu.VMEM((S, 2 * LANES), BF16),
            pltpu.VMEM((LANES, LANES), F32),
            pltpu.VMEM((2, LANES, LANES), BF16),
            pltpu.VMEM((2, LANES, LANES), BF16),
        ] + _attention_scratch(S),
        compiler_params=_params(("arbitrary", "arbitrary")),
        name="moba_attention",
    )(qa, ka, va)


def _diff_kernel(lq1_ref, lk1_ref, lq2_ref, lk2_ref, g_ref, q_ref, k_ref, v_ref, o_ref, cbias_ref, s_ref, p_ref):
    S = k_ref.shape[1]
    nb = S // ATT_BLK
    lam = (jnp.exp(jnp.sum(lq1_ref[...] * lk1_ref[...], axis=1, keepdims=True))
           - jnp.exp(jnp.sum(lq2_ref[...] * lk2_ref[...], axis=1, keepdims=True)) + LAMBDA_INIT)
    cbias_ref[...] = _causal_bias()
    lane = lax.broadcasted_iota(I32, (ATT_BLK, LANES), 1)

    for i in reversed(range(nb)):
        r0 = i * ATT_BLK
        q = q_ref[0, r0:r0 + ATT_BLK, :].astype(F32)
        qs = jnp.concatenate([jnp.where(lane < HEAD_DIM, q, 0.0), jnp.where(lane >= HEAD_DIM, q, 0.0)],
                             axis=0).astype(BF16)
        out = _attend(qs, lambda n: k_ref[0, n * ATT_BLK:(n + 1) * ATT_BLK, :], lambda L: v_ref[0, :L, :],
                      s_ref, p_ref, cbias_ref, i)
        o = out[:ATT_BLK] - lam * out[ATT_BLK:]
        o_ref[0, r0:r0 + ATT_BLK, :] = (_rms(o, g_ref[...]) * (1.0 - LAMBDA_INIT)).astype(BF16)


def _diff_attention(qd, kd, vd, lq1, lk1, lq2, lk2, subln_g):
    B, S, W = qd.shape
    assert S % ATT_BLK == 0 and W % LANES == 0
    spec = pl.BlockSpec((1, S, LANES), lambda b, j: (b, 0, j))
    small = pl.BlockSpec((1, HEAD_DIM), lambda b, j: (0, 0))
    return pl.pallas_call(
        _diff_kernel,
        grid=(B, W // LANES),
        in_specs=[small, small, small, small, pl.BlockSpec((1, LANES), lambda b, j: (0, 0)), spec, spec, spec],
        out_specs=spec,
        out_shape=jax.ShapeDtypeStruct((B, S, W), BF16),
        scratch_shapes=_attention_scratch(S),
        compiler_params=_params(("arbitrary", "arbitrary")),
        name="diff_attention",
    )(lq1, lk1, lq2, lk2, subln_g, qd, kd, vd)


def _pack_bf16_pairs(x):
    n = x.shape[1] // 2
    lo = lax.bitcast_convert_type(x[:, :n].astype(BF16).astype(F32), U32)
    hi = lax.bitcast_convert_type(x[:, n:].astype(BF16).astype(F32), U32)
    return (lo >> 16) | hi


def _unpack_bf16_pairs(w, dtype=BF16):
    lo = lax.bitcast_convert_type(w << 16, F32)
    hi = lax.bitcast_convert_type(w & jnp.uint32(0xFFFF0000), F32)
    return jnp.concatenate([lo, hi], axis=1).astype(dtype)


def _merge_rows(rows, first, x_ref, oa_ref, od_ref, gmix_ref, wg_ref, wbm_ref, wbd_ref, wout_ref, gffn_ref,
                wr_ref, br_ref,
                x1_ref, h2_ref, route_ref, routet_ref, cnt_ref, base_ref, ltri_ref, lg_ref):
    tm, D = rows.stop - rows.start, x_ref.shape[1]
    step = pl.program_id(0)

    @pl.when((step == 0) & first)
    def _():
        base_ref[...] = jnp.zeros_like(base_ref)
        lg_ref[...] = jnp.zeros_like(lg_ref)
        row = lax.broadcasted_iota(I32, (tm, tm), 0)
        col = lax.broadcasted_iota(I32, (tm, tm), 1)
        ltri_ref[...] = jnp.where(col < row, 1.0, 0.0).astype(BF16)

    lg = lg_ref[rows, :]
    live = step > 0
    lane = lax.broadcasted_iota(I32, (tm, LANES), 1)
    ninf = -jnp.inf

    def first_argmax(vals):
        m = jnp.max(vals, axis=1, keepdims=True)
        return m, jnp.min(jnp.where(vals == m, lane, LANES), axis=1, keepdims=True)

    is_group = lane < N_GROUPS
    gm, gsel = first_argmax(jnp.where(is_group, lg, ninf))
    p_group = 1.0 / jnp.sum(jnp.where(is_group, jnp.exp(lg - gm), 0.0), axis=1, keepdims=True)
    lo = N_GROUPS + gsel * EXPERTS_PER_GROUP
    el = jnp.where((lane >= lo) & (lane < lo + EXPERTS_PER_GROUP), lg, ninf)
    m1, i1 = first_argmax(el)
    m2, i2 = first_argmax(jnp.where(lane == i1, ninf, el))
    e2x = jnp.exp(m2 - m1)
    w1 = p_group / (1.0 + e2x)
    w2 = p_group * e2x / (1.0 + e2x)
    e1 = i1 - N_GROUPS
    e2 = i2 - N_GROUPS

    oh1 = jnp.where((lane == e1) & live, 1.0, 0.0)
    oh2 = jnp.where((lane == e2) & live, 1.0, 0.0)
    cc = _dot(ltri_ref[...], jnp.concatenate([oh1, oh2], axis=1).astype(BF16))
    c1, c2 = cc[:, :LANES], cc[:, LANES:]
    base = base_ref[...]
    tot1 = jnp.sum(oh1, axis=0, keepdims=True)
    tot2 = jnp.sum(oh2, axis=0, keepdims=True)
    r1 = jnp.sum(oh1 * (c1 + base), axis=1, keepdims=True)
    r2 = jnp.sum(oh2 * (c2 + base + tot1), axis=1, keepdims=True)
    base = base + tot1 + tot2
    base_ref[...] = base
    cnt_ref[...] = base

    cols = (e1.astype(F32), e2.astype(F32), w1, w2, r1, r2)
    route = jnp.zeros((tm, LANES), F32)
    for c, val in enumerate(cols):
        route = jnp.where(lane == c, val, route)
    route_ref[rows, :] = route[:, :ROUTE_COLS]
    routet_ref[:, rows] = route.T[:ROUTE_COLS, :]

    x = x_ref[rows, :]
    h = _rms(x, gmix_ref[...]).astype(BF16)
    sig = jax.nn.sigmoid(_dot(h, wg_ref[...]))
    merged = sig[:, :D] * _dot(oa_ref[rows, :], wbm_ref[...]) + sig[:, D:] * _dot(od_ref[rows, :], wbd_ref[...])
    x1 = x + _dot(merged.astype(BF16), wout_ref[...])
    x1_ref[rows, :] = x1
    h2 = _rms(x1, gffn_ref[...])
    h2_ref[rows, :] = _pack_bf16_pairs(h2)
    hhi = h2.astype(BF16)
    hlo = (h2 - hhi.astype(F32)).astype(BF16)
    lg_hi = _dot(hhi, wr_ref[...])
    lg_lo = _dot(hlo, wr_ref[...])
    lg_ref[rows, :] = (lg_hi[:, :LANES] + lg_hi[:, LANES:]) + (lg_lo[:, :LANES] + lg_lo[:, LANES:]) + br_ref[...]


def _merge_kernel(*refs):
    part = refs[0].shape[0] // MERGE_SPLIT
    for g in range(MERGE_SPLIT):
        _merge_rows(slice(g * part, (g + 1) * part), g == 0, *refs)


def _merge_route(x2, o_a, o_d, g_mix, w_g, w_bm, w_bd, w_out, g_ffn, w_r2, b_r):
    T, D = x2.shape
    tm = min(MERGE_TILE, T)
    assert T % tm == 0 and tm % (MERGE_SPLIT * SUBLANES) == 0
    nt = T // tm
    W = o_a.shape[1]
    chain = lambda w: pl.BlockSpec((tm, w), lambda i: (jnp.minimum(i, nt - 1), 0))
    routed = lambda i: jnp.maximum(i - 1, 0)
    full = lambda a: pl.BlockSpec(a.shape, lambda i: (0, 0))
    return pl.pallas_call(
        _merge_kernel,
        grid=(nt + 1,),
        in_specs=[chain(D), chain(W), chain(W), full(g_mix), full(w_g), full(w_bm), full(w_bd), full(w_out),
                  full(g_ffn), full(w_r2), full(b_r)],
        out_specs=[chain(D), chain(D // 2), pl.BlockSpec((tm, ROUTE_COLS), lambda i: (routed(i), 0)),
                   pl.BlockSpec((ROUTE_COLS, tm), lambda i: (0, routed(i))),
                   pl.BlockSpec((1, LANES), lambda i: (0, 0))],
        out_shape=[jax.ShapeDtypeStruct((T, D), F32), jax.ShapeDtypeStruct((T, D // 2), U32),
                   jax.ShapeDtypeStruct((T, ROUTE_COLS), F32), jax.ShapeDtypeStruct((ROUTE_COLS, T), F32),
                   jax.ShapeDtypeStruct((1, LANES), F32)],
        scratch_shapes=[pltpu.VMEM((1, LANES), F32), pltpu.VMEM((tm // MERGE_SPLIT, tm // MERGE_SPLIT), BF16),
                        pltpu.VMEM((tm, LANES), F32)],
        compiler_params=_params(("arbitrary",), MERGE_VMEM_LIMIT),
        name="merge_route",
    )(x2, o_a, o_d, g_mix, w_g, w_bm, w_bd, w_out, g_ffn, w_r2, b_r)


def _index_prefetch(dest_hbm, idx_ref, isem):
    i = pl.program_id(0)
    per = dest_hbm.shape[1]
    slot = i % 2

    def copy(step, sl):
        return pltpu.make_async_copy(dest_hbm.at[step], idx_ref.at[pl.ds(sl * per, per)], isem.at[sl])

    @pl.when(i == 0)
    def _():
        copy(0, 0).start()

    @pl.when(i + 1 < pl.num_programs(0))
    def _():
        copy(i + 1, 1 - slot).start()

    copy(i, slot).wait()
    return slot * per


def _dispatch_kernel(zs_ref, nu_ref, dest_hbm, h2_hbm, xd_hbm, idx_ref, zero_ref, src_ref, isem, ssem, rsem, zsem):
    tm = h2_hbm.shape[1] * 8
    nblk = xd_hbm.shape[0] // MOE_BLK
    i = pl.program_id(0)
    last = pl.num_programs(0) - 1

    @pl.when(i == 0)
    def _():
        zero_ref[...] = jnp.zeros_like(zero_ref)

        def zcopy(row):
            return pltpu.make_async_copy(zero_ref, xd_hbm.at[pl.ds(pl.multiple_of(row, MOE_BLK), MOE_BLK), :], zsem)

        def tail_start(blk, carry):
            zcopy(blk * MOE_BLK).start()
            return carry

        def tail_wait(blk, carry):
            zcopy(0).wait()
            return carry

        for e in range(N_EXPERTS):
            @pl.when(zs_ref[e] >= 0)
            def _():
                zcopy(zs_ref[e]).start()
        lax.fori_loop(nu_ref[0], nblk, tail_start, 0)
        for e in range(N_EXPERTS):
            @pl.when(zs_ref[e] >= 0)
            def _():
                zcopy(0).wait()
        lax.fori_loop(nu_ref[0], nblk, tail_wait, 0)

    def src_copy(step):
        return pltpu.make_async_copy(h2_hbm.at[step], src_ref.at[step % 3], ssem.at[step % 3])

    @pl.when(i == 0)
    def _():
        src_copy(0).start()

    @pl.when(i < last)
    def _():
        src_copy(i + 1).start()

    base = _index_prefetch(dest_hbm, idx_ref, isem)
    src_copy(i).wait()
    for slot in range(3):
        @pl.when(i % 3 == slot)
        def _():
            for r in range(tm):
                for kk in range(EXPERT_TOPK):
                    d = idx_ref[base + kk * tm + r]
                    pltpu.make_async_copy(src_ref.at[slot, r // 8, pl.ds(r % 8, 1), :],
                                          xd_hbm.at[pl.ds(d, 1), :], rsem.at[i % 2]).start(priority=kk)

    def drain(sem_slot):
        for kk in range(EXPERT_TOPK):
            pltpu.make_async_copy(src_ref.at[0], src_ref.at[0], rsem.at[sem_slot]).wait()

    @pl.when(i > 0)
    def _():
        drain((i - 1) % 2)

    @pl.when(i == last)
    def _():
        drain(i % 2)


def _dispatch(zstart, n_used, dest2, h2p, n_rows):
    T, Wp = h2p.shape
    nt, per = dest2.shape
    tm = per // EXPERT_TOPK
    return pl.pallas_call(
        _dispatch_kernel,
        grid_spec=pltpu.PrefetchScalarGridSpec(
            num_scalar_prefetch=2,
            grid=(nt,),
            in_specs=[pl.BlockSpec(memory_space=pl.ANY), pl.BlockSpec(memory_space=pl.ANY)],
            out_specs=pl.BlockSpec(memory_space=pl.ANY),
            scratch_shapes=[pltpu.SMEM((2 * per,), I32), pltpu.VMEM((MOE_BLK, Wp), U32),
                            pltpu.VMEM((3, tm // 8, 8, Wp), U32),
                            pltpu.SemaphoreType.DMA((2,)), pltpu.SemaphoreType.DMA((3,)),
                            pltpu.SemaphoreType.DMA((2,)), pltpu.SemaphoreType.DMA(())],
        ),
        out_shape=jax.ShapeDtypeStruct((n_rows, Wp), U32),
        compiler_params=_params(("arbitrary",)),
        name="moe_dispatch",
    )(zstart, n_used, dest2, h2p.reshape(nt, tm // 8, 8, Wp))


def _expert_kernel(be_ref, nu_ref, xd_ref, w1_ref, w3_ref, w2_ref, yb_ref, w1b, w3b, w2b):
    i = pl.program_id(0)
    n_used = nu_ref[0]

    @pl.when((i < n_used) & ((i == 0) | (be_ref[i] != be_ref[jnp.maximum(i - 1, 0)])))
    def _():
        w1b[...] = w1_ref[0].astype(BF16)
        w3b[...] = w3_ref[0].astype(BF16)
        w2b[...] = w2_ref[0].astype(BF16)

    @pl.when(i < n_used)
    def _():
        x = _unpack_bf16_pairs(xd_ref[...])
        act = jax.nn.silu(_dot(x, w1b[...])) * _dot(x, w3b[...])
        yb_ref[...] = _pack_bf16_pairs(_dot(act.astype(BF16), w2b[...]))

    @pl.when(i >= n_used)
    def _():
        yb_ref[...] = jnp.zeros_like(yb_ref)


def _experts(blk_expert, n_used, xd, w1, w3, w2):
    P, Wp = xd.shape
    E, D, De = w1.shape
    nblk = P // MOE_BLK
    return pl.pallas_call(
        _expert_kernel,
        grid_spec=pltpu.PrefetchScalarGridSpec(
            num_scalar_prefetch=2,
            grid=(nblk,),
            in_specs=[
                pl.BlockSpec((MOE_BLK, Wp), lambda i, be, nu: (jnp.minimum(i, nu[0] - 1), 0)),
                pl.BlockSpec((1, D, De), lambda i, be, nu: (be[i], 0, 0)),
                pl.BlockSpec((1, D, De), lambda i, be, nu: (be[i], 0, 0)),
                pl.BlockSpec((1, De, D), lambda i, be, nu: (be[i], 0, 0)),
            ],
            out_specs=pl.BlockSpec((MOE_BLK, D // 2), lambda i, be, nu: (i, 0)),
            scratch_shapes=[pltpu.VMEM((D, De), BF16), pltpu.VMEM((D, De), BF16), pltpu.VMEM((De, D), BF16)],
        ),
        out_shape=jax.ShapeDtypeStruct((P, D // 2), U32),
        compiler_params=_params(("arbitrary",), EXPERT_VMEM_LIMIT),
        name="moe_experts",
    )(blk_expert, n_used, xd, w1, w3, w2)


def _combine_kernel(dest_hbm, route_ref, x1_ref, g_ref, yb_hbm, o_ref, idx_ref, buf_ref, isem, rsem):
    tm, D = x1_ref.shape
    per = EXPERT_TOPK * tm
    i = pl.program_id(0)
    last = pl.num_programs(0) - 1

    def idx_copy(step):
        sl = step % 3
        return pltpu.make_async_copy(dest_hbm.at[step], idx_ref.at[pl.ds(sl * per, per)], isem.at[sl])

    def issue(step, bs, j):
        ib = (step % 3) * per
        for u in range(8):
            for kk in range(EXPERT_TOPK):
                d = idx_ref[ib + kk * tm + 8 * j + u]
                pltpu.make_async_copy(yb_hbm.at[pl.ds(d, 1), :], buf_ref.at[bs, kk, j, pl.ds(u, 1), :],
                                      rsem.at[bs]).start(priority=kk)

    def drain(bs):
        for kk in range(EXPERT_TOPK):
            pltpu.make_async_copy(buf_ref.at[bs, kk], buf_ref.at[bs, kk], rsem.at[bs]).wait()

    @pl.when(i == 0)
    def _():
        idx_copy(0).start()

        @pl.when(last > 0)
        def _():
            idx_copy(1).start()

        idx_copy(0).wait()
        for j in range(tm // 8):
            issue(0, 0, j)

    @pl.when(i + 2 <= last)
    def _():
        idx_copy(i + 2).start()

    @pl.when(i < last)
    def _():
        idx_copy(i + 1).wait()

    nxt = jnp.minimum(i + 1, last)
    for cur in range(2):
        @pl.when(i % 2 == cur)
        def _():
            drain(cur)
            for j in range(tm // 8):
                issue(nxt, 1 - cur, j)
                rows = slice(8 * j, 8 * j + 8)
                route = route_ref[rows, :]
                y0 = _unpack_bf16_pairs(buf_ref[cur, 0, j], F32)
                y1 = _unpack_bf16_pairs(buf_ref[cur, 1, j], F32)
                o_ref[rows, :] = _rms(x1_ref[rows, :] + (route[:, 2:3] * y0 + route[:, 3:4] * y1), g_ref[...])

            @pl.when(i == last)
            def _():
                drain(1 - cur)


def _combine(dest2, route, x1, g_final, yb):
    T, D = x1.shape
    nt, per = dest2.shape
    tm = per // EXPERT_TOPK
    return pl.pallas_call(
        _combine_kernel,
        grid=(nt,),
        in_specs=[pl.BlockSpec(memory_space=pl.ANY), pl.BlockSpec((tm, ROUTE_COLS), lambda i: (i, 0)),
                  pl.BlockSpec((tm, D), lambda i: (i, 0)), pl.BlockSpec((1, D), lambda i: (0, 0)),
                  pl.BlockSpec(memory_space=pl.ANY)],
        out_specs=pl.BlockSpec((tm, D), lambda i: (i, 0)),
        out_shape=jax.ShapeDtypeStruct((T, D), F32),
        scratch_shapes=[pltpu.SMEM((3 * per,), I32), pltpu.VMEM((2, EXPERT_TOPK, tm // 8, 8, D // 2), U32),
                        pltpu.SemaphoreType.DMA((3,)), pltpu.SemaphoreType.DMA((2,))],
        compiler_params=_params(("arbitrary",)),
        name="moe_combine",
    )(dest2, route, x1, g_final, yb)


def _moe(h2p, route, route_t, counts, x1, g_final, w1, w3, w2):
    T = h2p.shape[0]
    A = T * EXPERT_TOPK
    P = -(-A // MOE_BLK) * MOE_BLK + N_EXPERTS * MOE_BLK
    nblk = P // MOE_BLK
    cnt = counts[0, :N_EXPERTS].astype(I32)
    padded = (cnt + MOE_BLK - 1) // MOE_BLK * MOE_BLK
    pends = jnp.cumsum(padded)
    pstarts = pends - padded
    ids = route_t[0:EXPERT_TOPK].astype(I32)
    ranks = route_t[4:4 + EXPERT_TOPK].astype(I32)
    dest = ranks
    for e in range(N_EXPERTS):
        dest = dest + jnp.where(ids == e, pstarts[e], 0)

    def tiled(tm):
        tm = min(tm, T)
        return dest.reshape(EXPERT_TOPK, T // tm, tm).transpose(1, 0, 2).reshape(T // tm, EXPERT_TOPK * tm)

    blk_row = jnp.arange(nblk, dtype=I32) * MOE_BLK
    blk_expert = jnp.minimum(jnp.sum((pends[None, :] <= blk_row[:, None]).astype(I32), axis=1), N_EXPERTS - 1)
    n_used = (pends[-1:] // MOE_BLK).astype(I32)
    zstart = jnp.where(padded > 0, pends - MOE_BLK, -1).astype(I32)
    xd = _dispatch(zstart, n_used, tiled(DISPATCH_TILE), h2p, P)
    yb = _experts(blk_expert, n_used, xd, w1, w3, w2)
    return _combine(tiled(COMBINE_TILE), route, x1, g_final, yb)


def _rope_tables(seq):
    inv = 1.0 / (ROPE_THETA ** (jnp.arange(0, HEAD_DIM, 2, dtype=F32) / HEAD_DIM))
    ang = jnp.arange(seq, dtype=F32)[:, None] * inv[None, :]
    ang = jnp.concatenate([ang, ang], axis=-1)
    sign = jnp.concatenate([-jnp.ones((HEAD_DIM // 2,), F32), jnp.ones((HEAD_DIM // 2,), F32)])
    reps = LANES // HEAD_DIM
    return jnp.tile(jnp.cos(ang), (1, reps)), jnp.tile(jnp.sin(ang) * sign, (1, reps))


def kernel(x, g_mix, w_in, w_branch_moba, w_branch_diff, w_out, diff_lambda_q1, diff_lambda_k1, diff_lambda_q2, diff_lambda_k2, diff_subln_g, g_ffn, w_group, b_group, w_router, b_router, w_expert_gate, w_expert_up, w_expert_down, g_final):
    B, S, D = x.shape
    assert w_in.shape[0] == 1, "one layer"
    T = B * S
    x2 = x.reshape(T, D)
    n_qkv = w_in.shape[2] - 2 * D
    w_qkv = w_in[0, :, :n_qkv].astype(BF16)
    w_g = w_in[0, :, n_qkv:].astype(BF16)
    cos, sin = _rope_tables(S)

    qa, ka, va, qd, kd, vd = _qkv_proj(x2, g_mix, w_qkv, cos, sin, S)
    W = qa.shape[1]
    o_a = _moba_attention(qa.reshape(B, S, W), ka.reshape(B, S, W), va.reshape(B, S, W)).reshape(T, W)
    o_d = _diff_attention(qd.reshape(B, S, W), kd.reshape(B, S, W), vd.reshape(B, S, W),
                          diff_lambda_q1, diff_lambda_k1, diff_lambda_q2, diff_lambda_k2, diff_subln_g).reshape(T, W)

    w_r = jnp.concatenate([w_group[0], w_router[0]], axis=1)
    w_r = jnp.pad(w_r, ((0, 0), (0, LANES - w_r.shape[1])))
    wr_hi = w_r.astype(BF16)
    w_r2 = jnp.concatenate([wr_hi, (w_r - wr_hi.astype(F32)).astype(BF16)], axis=1)
    b_r = jnp.pad(jnp.concatenate([b_group[0], b_router[0]])[None, :], ((0, 0), (0, LANES - N_GROUPS - N_EXPERTS)))

    x1, h2p, route, route_t, counts = _merge_route(
        x2, o_a, o_d, g_mix, w_g, w_branch_moba[0].astype(BF16), w_branch_diff[0].astype(BF16),
        w_out[0].astype(BF16), g_ffn, w_r2, b_r)

    out = _moe(h2p, route, route_t, counts, x1, g_final[None, :],
               w_expert_gate[0], w_expert_up[0], w_expert_down[0])
    return out.reshape(B, S, D)
```

```python
import jax
import jax.numpy as jnp
from jax import lax
from jax.experimental import pallas as pl
from jax.experimental.pallas import tpu as pltpu

F32 = jnp.float32
BF16 = jnp.bfloat16
I32 = jnp.int32
U32 = jnp.uint32

LANES = 128
SUBLANES = 8
HEAD_DIM = 64
ATT_BLK = 256
ATT_BLK_SHIFT = 8
MOBA_TOPK = 3
N_GROUPS = 4
EXPERTS_PER_GROUP = 8
N_EXPERTS = N_GROUPS * EXPERTS_PER_GROUP
EXPERT_TOPK = 2
MOE_BLK = 512
DISPATCH_TILE = 1024
COMBINE_TILE = 1024
MERGE_TILE = 1024
MERGE_SPLIT = 2
QKV_TILE = 1024
ROUTE_COLS = 8
EPS = 1e-6
NEG = -1e30
ROPE_THETA = 10000.0
LAMBDA_INIT = 0.8 - 0.6 * 1.0
ATT_SCALE = HEAD_DIM ** -0.5
VMEM_LIMIT = 48 * 1024 * 1024
EXPERT_VMEM_LIMIT = 56 * 1024 * 1024
MERGE_VMEM_LIMIT = 56 * 1024 * 1024


def _dot(a, b):
    return jnp.dot(a, b, preferred_element_type=F32)


def _dot_nt(a, b):
    return lax.dot_general(a, b, (((1,), (1,)), ((), ())), preferred_element_type=F32)


def _rms(x, g):
    var = jnp.mean(x * x, axis=-1, keepdims=True)
    return (x * lax.rsqrt(var + EPS)) * g


def _params(sem, vmem=VMEM_LIMIT):
    return pltpu.CompilerParams(dimension_semantics=sem, vmem_limit_bytes=vmem)


def _qkv_kernel(x_ref, g_ref, w_ref, cos_ref, sin_ref, qa_ref, ka_ref, va_ref, qd_ref, kd_ref, vd_ref):
    h = _rms(x_ref[...], g_ref[...]).astype(BF16)
    cos = cos_ref[...]
    sin = sin_ref[...]
    lane = lax.broadcasted_iota(I32, cos.shape, 1)
    first = (lane & (HEAD_DIM - 1)) < HEAD_DIM // 2
    width = qa_ref.shape[1]
    outs = ((qa_ref, True, ATT_SCALE), (ka_ref, True, 1.0), (va_ref, False, 1.0),
            (qd_ref, True, ATT_SCALE), (kd_ref, True, 1.0), (vd_ref, False, 1.0))
    for j, (o_ref, rope, scale) in enumerate(outs):
        y = _dot(h, w_ref[:, j * width:(j + 1) * width])
        for c in range(width // LANES):
            yc = y[:, c * LANES:(c + 1) * LANES]
            if rope:
                rot = jnp.where(first, pltpu.roll(yc, LANES - HEAD_DIM // 2, 1), pltpu.roll(yc, HEAD_DIM // 2, 1))
                yc = yc * cos + rot * sin
            if scale != 1.0:
                yc = yc * scale
            o_ref[:, c * LANES:(c + 1) * LANES] = yc.astype(BF16)


def _qkv_proj(x2, g_mix, w_qkv, cos, sin, seq):
    T, D = x2.shape
    width = w_qkv.shape[1] // 6
    tm = min(QKV_TILE, seq)
    assert seq % tm == 0 and T % tm == 0
    spt = seq // tm
    out = jax.ShapeDtypeStruct((T, width), BF16)
    return pl.pallas_call(
        _qkv_kernel,
        grid=(T // tm,),
        in_specs=[
            pl.BlockSpec((tm, D), lambda i: (i, 0)),
            pl.BlockSpec((1, D), lambda i: (0, 0)),
            pl.BlockSpec(w_qkv.shape, lambda i: (0, 0)),
            pl.BlockSpec((tm, LANES), lambda i: (i % spt, 0)),
            pl.BlockSpec((tm, LANES), lambda i: (i % spt, 0)),
        ],
        out_specs=[pl.BlockSpec((tm, width), lambda i: (i, 0))] * 6,
        out_shape=[out] * 6,
        compiler_params=_params(("arbitrary",)),
        name="qkv_proj",
    )(x2, g_mix, w_qkv, cos, sin)


ATT_ROWS = 2 * ATT_BLK


def _attend(qs, k_block, v_rows, s_ref, p_ref, cbias_ref, i):
    mx = None
    for n in range(i + 1):
        sb = _dot_nt(qs, k_block(n))
        if n == i:
            sb = sb + cbias_ref[...]
        s_ref[:, n * ATT_BLK:(n + 1) * ATT_BLK] = sb
        for c in range(ATT_BLK // LANES):
            part = sb[:, c * LANES:(c + 1) * LANES]
            mx = part if mx is None else jnp.maximum(mx, part)
    mb = jnp.broadcast_to(jnp.max(mx, axis=1, keepdims=True), (ATT_ROWS, LANES))
    ps = None
    for c in range((i + 1) * ATT_BLK // LANES):
        p = jnp.exp(s_ref[:, c * LANES:(c + 1) * LANES] - mb)
        ps = p if ps is None else ps + p
        p_ref[:, c * LANES:(c + 1) * LANES] = p.astype(BF16)
    l = jnp.sum(ps, axis=1, keepdims=True)
    L = (i + 1) * ATT_BLK
    return _dot(p_ref[:, :L], v_rows(L)) * (1.0 / l)


def _causal_bias():
    row = lax.broadcasted_iota(I32, (ATT_ROWS, ATT_BLK), 0) & (ATT_BLK - 1)
    col = lax.broadcasted_iota(I32, (ATT_ROWS, ATT_BLK), 1)
    return jnp.where(col <= row, 0.0, NEG).astype(F32)


def _attention_scratch(seq):
    return [pltpu.VMEM((ATT_ROWS, ATT_BLK), F32), pltpu.VMEM((ATT_ROWS, seq), F32), pltpu.VMEM((ATT_ROWS, seq), BF16)]


def _moba_kernel(q_ref, k_ref, v_ref, o_ref, kaug_ref, kmf_ref, kmhi_ref, kmlo_ref, cbias_ref, s_ref, p_ref):
    S = k_ref.shape[1]
    nb = S // ATT_BLK
    k = k_ref[0]
    lane_s = lax.broadcasted_iota(I32, (S, LANES), 1)
    blk_s = lax.broadcasted_iota(I32, (S, LANES), 0) >> ATT_BLK_SHIFT
    kaug_ref[:, :LANES] = k
    kaug_ref[:, LANES:] = jnp.where(lane_s == blk_s, 1.0, 0.0).astype(BF16)
    kmean = jnp.mean(k.astype(F32).reshape(nb, ATT_BLK, LANES), axis=1)
    lane_n = lax.broadcasted_iota(I32, (nb, LANES), 1)
    for hh in range(2):
        own_n = (lane_n < HEAD_DIM) if hh == 0 else (lane_n >= HEAD_DIM)
        kmf_ref[...] = jnp.zeros((LANES, LANES), F32)
        kmf_ref[0:nb, :] = jnp.where(own_n, kmean, 0.0)
        km = kmf_ref[...]
        hi = km.astype(BF16)
        kmhi_ref[hh] = hi
        kmlo_ref[hh] = (km - hi.astype(F32)).astype(BF16)
    cbias_ref[...] = _causal_bias()

    lane = lax.broadcasted_iota(I32, (ATT_BLK, LANES), 1)

    def stacked_queries(i):
        q = q_ref[0, i * ATT_BLK:(i + 1) * ATT_BLK, :].astype(F32)
        rows = []
        for hh in range(2):
            own = (lane < HEAD_DIM) if hh == 0 else (lane >= HEAD_DIM)
            qh = jnp.where(own, q, 0.0)
            if i <= MOBA_TOPK:
                rows.append(qh)
                continue
            qhb = qh.astype(BF16)
            gate = (_dot_nt(kmhi_ref[hh], qhb) + _dot_nt(kmlo_ref[hh], qhb))[:SUBLANES, :]
            blk = lax.broadcasted_iota(I32, (SUBLANES, ATT_BLK), 0)
            cand = blk < i
            g = jnp.where(cand, gate, -jnp.inf)
            rank = jnp.zeros((SUBLANES, ATT_BLK), I32)
            for r in range(1, i):
                rank = rank + jnp.where((blk >= r) & (pltpu.roll(g, r, 0) >= g), 1, 0)
                rank = rank + jnp.where((blk < SUBLANES - r) & (pltpu.roll(g, SUBLANES - r, 0) > g), 1, 0)
            drop_t = jnp.where(cand & (rank >= MOBA_TOPK), NEG, 0.0)
            drop = jnp.concatenate([drop_t, jnp.zeros((LANES - SUBLANES, ATT_BLK), F32)], axis=0).T
            rows.append(jnp.concatenate([qh, drop], axis=1))
        return jnp.concatenate(rows, axis=0).astype(BF16)

    for i in reversed(range(nb)):
        if i > MOBA_TOPK:
            k_block = lambda n: kaug_ref[n * ATT_BLK:(n + 1) * ATT_BLK, :]
        else:
            k_block = lambda n: k_ref[0, n * ATT_BLK:(n + 1) * ATT_BLK, :]
        out = _attend(stacked_queries(i), k_block, lambda L: v_ref[0, :L, :], s_ref, p_ref, cbias_ref, i)
        o_ref[0, i * ATT_BLK:(i + 1) * ATT_BLK, :] = jnp.where(
            lane < HEAD_DIM, out[:ATT_BLK], out[ATT_BLK:]).astype(BF16)


def _moba_attention(qa, ka, va):
    B, S, W = qa.shape
    assert S % ATT_BLK == 0 and S // ATT_BLK <= SUBLANES and W % LANES == 0
    spec = pl.BlockSpec((1, S, LANES), lambda b, j: (b, 0, j))
    return pl.pallas_call(
        _moba_kernel,
        grid=(B, W // LANES),
        in_specs=[spec, spec, spec],
        out_specs=spec,
        out_shape=jax.ShapeDtypeStruct((B, S, W), BF16),
        scratch_shapes=[
            pltpu.VMEM((S, 2 * LANES), BF16),
            pltpu.VMEM((LANES, LANES), F32),
            pltpu.VMEM((2, LANES, LANES), BF16),
            pltpu.VMEM((2, LANES, LANES), BF16),
        ] + _attention_scratch(S),
        compiler_params=_params(("arbitrary", "arbitrary")),
        name="moba_attention",
    )(qa, ka, va)


def _diff_kernel(lq1_ref, lk1_ref, lq2_ref, lk2_ref, g_ref, q_ref, k_ref, v_ref, o_ref, cbias_ref, s_ref, p_ref):
    S = k_ref.shape[1]
    nb = S // ATT_BLK
    lam = (jnp.exp(jnp.sum(lq1_ref[...] * lk1_ref[...], axis=1, keepdims=True))
           - jnp.exp(jnp.sum(lq2_ref[...] * lk2_ref[...], axis=1, keepdims=True)) + LAMBDA_INIT)
    cbias_ref[...] = _causal_bias()
    lane = lax.broadcasted_iota(I32, (ATT_BLK, LANES), 1)

    for i in reversed(range(nb)):
        r0 = i * ATT_BLK
        q = q_ref[0, r0:r0 + ATT_BLK, :].astype(F32)
        qs = jnp.concatenate([jnp.where(lane < HEAD_DIM, q, 0.0), jnp.where(lane >= HEAD_DIM, q, 0.0)],
                             axis=0).astype(BF16)
        out = _attend(qs, lambda n: k_ref[0, n * ATT_BLK:(n + 1) * ATT_BLK, :], lambda L: v_ref[0, :L, :],
                      s_ref, p_ref, cbias_ref, i)
        o = out[:ATT_BLK] - lam * out[ATT_BLK:]
        o_ref[0, r0:r0 + ATT_BLK, :] = (_rms(o, g_ref[...]) * (1.0 - LAMBDA_INIT)).astype(BF16)


def _diff_attention(qd, kd, vd, lq1, lk1, lq2, lk2, subln_g):
    B, S, W = qd.shape
    assert S % ATT_BLK == 0 and W % LANES == 0
    spec = pl.BlockSpec((1, S, LANES), lambda b, j: (b, 0, j))
    small = pl.BlockSpec((1, HEAD_DIM), lambda b, j: (0, 0))
    return pl.pallas_call(
        _diff_kernel,
        grid=(B, W // LANES),
        in_specs=[small, small, small, small, pl.BlockSpec((1, LANES), lambda b, j: (0, 0)), spec, spec, spec],
        out_specs=spec,
        out_shape=jax.ShapeDtypeStruct((B, S, W), BF16),
        scratch_shapes=_attention_scratch(S),
        compiler_params=_params(("arbitrary", "arbitrary")),
        name="diff_attention",
    )(lq1, lk1, lq2, lk2, subln_g, qd, kd, vd)


def _pack_bf16_pairs(x):
    n = x.shape[1] // 2
    lo = lax.bitcast_convert_type(x[:, :n].astype(BF16).astype(F32), U32)
    hi = lax.bitcast_convert_type(x[:, n:].astype(BF16).astype(F32), U32)
    return (lo >> 16) | hi


def _unpack_bf16_pairs(w, dtype=BF16):
    lo = lax.bitcast_convert_type(w << 16, F32)
    hi = lax.bitcast_convert_type(w & jnp.uint32(0xFFFF0000), F32)
    return jnp.concatenate([lo, hi], axis=1).astype(dtype)


def _merge_rows(rows, first, x_ref, oa_ref, od_ref, gmix_ref, wg_ref, wbm_ref, wbd_ref, wout_ref, gffn_ref,
                wr_ref, br_ref,
                x1_ref, h2_ref, route_ref, routet_ref, cnt_ref, base_ref, ltri_ref, lg_ref):
    tm, D = rows.stop - rows.start, x_ref.shape[1]
    step = pl.program_id(0)

    @pl.when((step == 0) & first)
    def _():
        base_ref[...] = jnp.zeros_like(base_ref)
        lg_ref[...] = jnp.zeros_like(lg_ref)
        row = lax.broadcasted_iota(I32, (tm, tm), 0)
        col = lax.broadcasted_iota(I32, (tm, tm), 1)
        ltri_ref[...] = jnp.where(col < row, 1.0, 0.0).astype(BF16)

    lg = lg_ref[rows, :]
    live = step > 0
    lane = lax.broadcasted_iota(I32, (tm, LANES), 1)
    ninf = -jnp.inf

    def first_argmax(vals):
        m = jnp.max(vals, axis=1, keepdims=True)
        return m, jnp.min(jnp.where(vals == m, lane, LANES), axis=1, keepdims=True)

    is_group = lane < N_GROUPS
    gm, gsel = first_argmax(jnp.where(is_group, lg, ninf))
    p_group = 1.0 / jnp.sum(jnp.where(is_group, jnp.exp(lg - gm), 0.0), axis=1, keepdims=True)
    lo = N_GROUPS + gsel * EXPERTS_PER_GROUP
    el = jnp.where((lane >= lo) & (lane < lo + EXPERTS_PER_GROUP), lg, ninf)
    m1, i1 = first_argmax(el)
    m2, i2 = first_argmax(jnp.where(lane == i1, ninf, el))
    e2x = jnp.exp(m2 - m1)
    w1 = p_group / (1.0 + e2x)
    w2 = p_group * e2x / (1.0 + e2x)
    e1 = i1 - N_GROUPS
    e2 = i2 - N_GROUPS

    oh1 = jnp.where((lane == e1) & live, 1.0, 0.0)
    oh2 = jnp.where((lane == e2) & live, 1.0, 0.0)
    cc = _dot(ltri_ref[...], jnp.concatenate([oh1, oh2], axis=1).astype(BF16))
    c1, c2 = cc[:, :LANES], cc[:, LANES:]
    base = base_ref[...]
    tot1 = jnp.sum(oh1, axis=0, keepdims=True)
    tot2 = jnp.sum(oh2, axis=0, keepdims=True)
    r1 = jnp.sum(oh1 * (c1 + base), axis=1, keepdims=True)
    r2 = jnp.sum(oh2 * (c2 + base + tot1), axis=1, keepdims=True)
    base = base + tot1 + tot2
    base_ref[...] = base
    cnt_ref[...] = base

    cols = (e1.astype(F32), e2.astype(F32), w1, w2, r1, r2)
    route = jnp.zeros((tm, LANES), F32)
    for c, val in enumerate(cols):
        route = jnp.where(lane == c, val, route)
    route_ref[rows, :] = route[:, :ROUTE_COLS]
    routet_ref[:, rows] = route.T[:ROUTE_COLS, :]
    yield

    x = x_ref[rows, :]
    h = _rms(x, gmix_ref[...]).astype(BF16)
    sig = jax.nn.sigmoid(_dot(h, wg_ref[...]))
    yield
    merged = sig[:, :D] * _dot(oa_ref[rows, :], wbm_ref[...]) + sig[:, D:] * _dot(od_ref[rows, :], wbd_ref[...])
    yield
    x1 = x + _dot(merged.astype(BF16), wout_ref[...])
    x1_ref[rows, :] = x1
    yield
    h2 = _rms(x1, gffn_ref[...])
    h2_ref[rows, :] = _pack_bf16_pairs(h2)
    hhi = h2.astype(BF16)
    hlo = (h2 - hhi.astype(F32)).astype(BF16)
    lg_hi = _dot(hhi, wr_ref[...])
    lg_lo = _dot(hlo, wr_ref[...])
    lg_ref[rows, :] = (lg_hi[:, :LANES] + lg_hi[:, LANES:]) + (lg_lo[:, :LANES] + lg_lo[:, LANES:]) + br_ref[...]


def _merge_kernel(*refs):
    part = refs[0].shape[0] // MERGE_SPLIT
    groups = [_merge_rows(slice(g * part, (g + 1) * part), g == 0, *refs) for g in range(MERGE_SPLIT)]
    while groups:
        groups = [g for g in groups if next(g, StopIteration) is not StopIteration]


def _merge_route(x2, o_a, o_d, g_mix, w_g, w_bm, w_bd, w_out, g_ffn, w_r2, b_r):
    T, D = x2.shape
    tm = min(MERGE_TILE, T)
    assert T % tm == 0 and tm % (MERGE_SPLIT * SUBLANES) == 0
    nt = T // tm
    W = o_a.shape[1]
    chain = lambda w: pl.BlockSpec((tm, w), lambda i: (jnp.minimum(i, nt - 1), 0))
    routed = lambda i: jnp.maximum(i - 1, 0)
    full = lambda a: pl.BlockSpec(a.shape, lambda i: (0, 0))
    return pl.pallas_call(
        _merge_kernel,
        grid=(nt + 1,),
        in_specs=[chain(D), chain(W), chain(W), full(g_mix), full(w_g), full(w_bm), full(w_bd), full(w_out),
                  full(g_ffn), full(w_r2), full(b_r)],
        out_specs=[chain(D), chain(D // 2), pl.BlockSpec((tm, ROUTE_COLS), lambda i: (routed(i), 0)),
                   pl.BlockSpec((ROUTE_COLS, tm), lambda i: (0, routed(i))),
                   pl.BlockSpec((1, LANES), lambda i: (0, 0))],
        out_shape=[jax.ShapeDtypeStruct((T, D), F32), jax.ShapeDtypeStruct((T, D // 2), U32),
                   jax.ShapeDtypeStruct((T, ROUTE_COLS), F32), jax.ShapeDtypeStruct((ROUTE_COLS, T), F32),
                   jax.ShapeDtypeStruct((1, LANES), F32)],
        scratch_shapes=[pltpu.VMEM((1, LANES), F32), pltpu.VMEM((tm // MERGE_SPLIT, tm // MERGE_SPLIT), BF16),
                        pltpu.VMEM((tm, LANES), F32)],
        compiler_params=_params(("arbitrary",), MERGE_VMEM_LIMIT),
        name="merge_route",
    )(x2, o_a, o_d, g_mix, w_g, w_bm, w_bd, w_out, g_ffn, w_r2, b_r)


def _index_prefetch(dest_hbm, idx_ref, isem):
    i = pl.program_id(0)
    per = dest_hbm.shape[1]
    slot = i % 2

    def copy(step, sl):
        return pltpu.make_async_copy(dest_hbm.at[step], idx_ref.at[pl.ds(sl * per, per)], isem.at[sl])

    @pl.when(i == 0)
    def _():
        copy(0, 0).start()

    @pl.when(i + 1 < pl.num_programs(0))
    def _():
        copy(i + 1, 1 - slot).start()

    copy(i, slot).wait()
    return slot * per


def _dispatch_kernel(zs_ref, nu_ref, dest_hbm, h2_hbm, xd_hbm, idx_ref, zero_ref, src_ref, isem, ssem, rsem, zsem):
    tm = h2_hbm.shape[1] * 8
    nblk = xd_hbm.shape[0] // MOE_BLK
    i = pl.program_id(0)
    last = pl.num_programs(0) - 1

    @pl.when(i == 0)
    def _():
        zero_ref[...] = jnp.zeros_like(zero_ref)

        def zcopy(row):
            return pltpu.make_async_copy(zero_ref, xd_hbm.at[pl.ds(pl.multiple_of(row, MOE_BLK), MOE_BLK), :], zsem)

        def tail_start(blk, carry):
            zcopy(blk * MOE_BLK).start()
            return carry

        def tail_wait(blk, carry):
            zcopy(0).wait()
            return carry

        for e in range(N_EXPERTS):
            @pl.when(zs_ref[e] >= 0)
            def _():
                zcopy(zs_ref[e]).start()
        lax.fori_loop(nu_ref[0], nblk, tail_start, 0)
        for e in range(N_EXPERTS):
            @pl.when(zs_ref[e] >= 0)
            def _():
                zcopy(0).wait()
        lax.fori_loop(nu_ref[0], nblk, tail_wait, 0)

    def src_copy(step):
        return pltpu.make_async_copy(h2_hbm.at[step], src_ref.at[step % 3], ssem.at[step % 3])

    @pl.when(i == 0)
    def _():
        src_copy(0).start()

    @pl.when(i < last)
    def _():
        src_copy(i + 1).start()

    base = _index_prefetch(dest_hbm, idx_ref, isem)
    src_copy(i).wait()
    for slot in range(3):
        @pl.when(i % 3 == slot)
        def _():
            for r in range(tm):
                for kk in range(EXPERT_TOPK):
                    d = idx_ref[base + kk * tm + r]
                    pltpu.make_async_copy(src_ref.at[slot, r // 8, pl.ds(r % 8, 1), :],
                                          xd_hbm.at[pl.ds(d, 1), :], rsem.at[i % 2]).start(priority=kk)

    def drain(sem_slot):
        for kk in range(EXPERT_TOPK):
            pltpu.make_async_copy(src_ref.at[0], src_ref.at[0], rsem.at[sem_slot]).wait()

    @pl.when(i > 0)
    def _():
        drain((i - 1) % 2)

    @pl.when(i == last)
    def _():
        drain(i % 2)


def _dispatch(zstart, n_used, dest2, h2p, n_rows):
    T, Wp = h2p.shape
    nt, per = dest2.shape
    tm = per // EXPERT_TOPK
    return pl.pallas_call(
        _dispatch_kernel,
        grid_spec=pltpu.PrefetchScalarGridSpec(
            num_scalar_prefetch=2,
            grid=(nt,),
            in_specs=[pl.BlockSpec(memory_space=pl.ANY), pl.BlockSpec(memory_space=pl.ANY)],
            out_specs=pl.BlockSpec(memory_space=pl.ANY),
            scratch_shapes=[pltpu.SMEM((2 * per,), I32), pltpu.VMEM((MOE_BLK, Wp), U32),
                            pltpu.VMEM((3, tm // 8, 8, Wp), U32),
                            pltpu.SemaphoreType.DMA((2,)), pltpu.SemaphoreType.DMA((3,)),
                            pltpu.SemaphoreType.DMA((2,)), pltpu.SemaphoreType.DMA(())],
        ),
        out_shape=jax.ShapeDtypeStruct((n_rows, Wp), U32),
        compiler_params=_params(("arbitrary",)),
        name="moe_dispatch",
    )(zstart, n_used, dest2, h2p.reshape(nt, tm // 8, 8, Wp))


def _expert_kernel(be_ref, nu_ref, xd_ref, w1_ref, w3_ref, w2_ref, yb_ref, w1b, w3b, w2b):
    i = pl.program_id(0)
    n_used = nu_ref[0]

    @pl.when((i < n_used) & ((i == 0) | (be_ref[i] != be_ref[jnp.maximum(i - 1, 0)])))
    def _():
        w1b[...] = w1_ref[0].astype(BF16)
        w3b[...] = w3_ref[0].astype(BF16)
        w2b[...] = w2_ref[0].astype(BF16)

    @pl.when(i < n_used)
    def _():
        x = _unpack_bf16_pairs(xd_ref[...])
        act = jax.nn.silu(_dot(x, w1b[...])) * _dot(x, w3b[...])
        yb_ref[...] = _pack_bf16_pairs(_dot(act.astype(BF16), w2b[...]))

    @pl.when(i >= n_used)
    def _():
        yb_ref[...] = jnp.zeros_like(yb_ref)


def _experts(blk_expert, n_used, xd, w1, w3, w2):
    P, Wp = xd.shape
    E, D, De = w1.shape
    nblk = P // MOE_BLK
    return pl.pallas_call(
        _expert_kernel,
        grid_spec=pltpu.PrefetchScalarGridSpec(
            num_scalar_prefetch=2,
            grid=(nblk,),
            in_specs=[
                pl.BlockSpec((MOE_BLK, Wp), lambda i, be, nu: (jnp.minimum(i, nu[0] - 1), 0)),
                pl.BlockSpec((1, D, De), lambda i, be, nu: (be[i], 0, 0)),
                pl.BlockSpec((1, D, De), lambda i, be, nu: (be[i], 0, 0)),
                pl.BlockSpec((1, De, D), lambda i, be, nu: (be[i], 0, 0)),
            ],
            out_specs=pl.BlockSpec((MOE_BLK, D // 2), lambda i, be, nu: (i, 0)),
            scratch_shapes=[pltpu.VMEM((D, De), BF16), pltpu.VMEM((D, De), BF16), pltpu.VMEM((De, D), BF16)],
        ),
        out_shape=jax.ShapeDtypeStruct((P, D // 2), U32),
        compiler_params=_params(("arbitrary",), EXPERT_VMEM_LIMIT),
        name="moe_experts",
    )(blk_expert, n_used, xd, w1, w3, w2)


def _combine_kernel(dest_hbm, route_ref, x1_ref, g_ref, yb_hbm, o_ref, idx_ref, buf_ref, isem, rsem):
    tm, D = x1_ref.shape
    per = EXPERT_TOPK * tm
    i = pl.program_id(0)
    last = pl.num_programs(0) - 1

    def idx_copy(step):
        sl = step % 3
        return pltpu.make_async_copy(dest_hbm.at[step], idx_ref.at[pl.ds(sl * per, per)], isem.at[sl])

    def issue(step, bs, j):
        ib = (step % 3) * per
        for u in range(8):
            for kk in range(EXPERT_TOPK):
                d = idx_ref[ib + kk * tm + 8 * j + u]
                pltpu.make_async_copy(yb_hbm.at[pl.ds(d, 1), :], buf_ref.at[bs, kk, j, pl.ds(u, 1), :],
                                      rsem.at[bs]).start(priority=kk)

    def drain(bs):
        for kk in range(EXPERT_TOPK):
            pltpu.make_async_copy(buf_ref.at[bs, kk], buf_ref.at[bs, kk], rsem.at[bs]).wait()

    @pl.when(i == 0)
    def _():
        idx_copy(0).start()

        @pl.when(last > 0)
        def _():
            idx_copy(1).start()

        idx_copy(0).wait()
        for j in range(tm // 8):
            issue(0, 0, j)

    @pl.when(i + 2 <= last)
    def _():
        idx_copy(i + 2).start()

    @pl.when(i < last)
    def _():
        idx_copy(i + 1).wait()

    nxt = jnp.minimum(i + 1, last)
    for cur in range(2):
        @pl.when(i % 2 == cur)
        def _():
            drain(cur)
            for j in range(tm // 8):
                issue(nxt, 1 - cur, j)
                rows = slice(8 * j, 8 * j + 8)
                route = route_ref[rows, :]
                y0 = _unpack_bf16_pairs(buf_ref[cur, 0, j], F32)
                y1 = _unpack_bf16_pairs(buf_ref[cur, 1, j], F32)
                o_ref[rows, :] = _rms(x1_ref[rows, :] + (route[:, 2:3] * y0 + route[:, 3:4] * y1), g_ref[...])

            @pl.when(i == last)
            def _():
                drain(1 - cur)


def _combine(dest2, route, x1, g_final, yb):
    T, D = x1.shape
    nt, per = dest2.shape
    tm = per // EXPERT_TOPK
    return pl.pallas_call(
        _combine_kernel,
        grid=(nt,),
        in_specs=[pl.BlockSpec(memory_space=pl.ANY), pl.BlockSpec((tm, ROUTE_COLS), lambda i: (i, 0)),
                  pl.BlockSpec((tm, D), lambda i: (i, 0)), pl.BlockSpec((1, D), lambda i: (0, 0)),
                  pl.BlockSpec(memory_space=pl.ANY)],
        out_specs=pl.BlockSpec((tm, D), lambda i: (i, 0)),
        out_shape=jax.ShapeDtypeStruct((T, D), F32),
        scratch_shapes=[pltpu.SMEM((3 * per,), I32), pltpu.VMEM((2, EXPERT_TOPK, tm // 8, 8, D // 2), U32),
                        pltpu.SemaphoreType.DMA((3,)), pltpu.SemaphoreType.DMA((2,))],
        compiler_params=_params(("arbitrary",)),
        name="moe_combine",
    )(dest2, route, x1, g_final, yb)


def _moe(h2p, route, route_t, counts, x1, g_final, w1, w3, w2):
    T = h2p.shape[0]
    A = T * EXPERT_TOPK
    P = -(-A // MOE_BLK) * MOE_BLK + N_EXPERTS * MOE_BLK
    nblk = P // MOE_BLK
    cnt = counts[0, :N_EXPERTS].astype(I32)
    padded = (cnt + MOE_BLK - 1) // MOE_BLK * MOE_BLK
    pends = jnp.cumsum(padded)
    pstarts = pends - padded
    ids = route_t[0:EXPERT_TOPK].astype(I32)
    ranks = route_t[4:4 + EXPERT_TOPK].astype(I32)
    dest = ranks
    for e in range(N_EXPERTS):
        dest = dest + jnp.where(ids == e, pstarts[e], 0)

    def tiled(tm):
        tm = min(tm, T)
        return dest.reshape(EXPERT_TOPK, T // tm, tm).transpose(1, 0, 2).reshape(T // tm, EXPERT_TOPK * tm)

    blk_row = jnp.arange(nblk, dtype=I32) * MOE_BLK
    blk_expert = jnp.minimum(jnp.sum((pends[None, :] <= blk_row[:, None]).astype(I32), axis=1), N_EXPERTS - 1)
    n_used = (pends[-1:] // MOE_BLK).astype(I32)
    zstart = jnp.where(padded > 0, pends - MOE_BLK, -1).astype(I32)
    xd = _dispatch(zstart, n_used, tiled(DISPATCH_TILE), h2p, P)
    yb = _experts(blk_expert, n_used, xd, w1, w3, w2)
    return _combine(tiled(COMBINE_TILE), route, x1, g_final, yb)


def _rope_tables(seq):
    inv = 1.0 / (ROPE_THETA ** (jnp.arange(0, HEAD_DIM, 2, dtype=F32) / HEAD_DIM))
    ang = jnp.arange(seq, dtype=F32)[:, None] * inv[None, :]
    ang = jnp.concatenate([ang, ang], axis=-1)
    sign = jnp.concatenate([-jnp.ones((HEAD_DIM // 2,), F32), jnp.ones((HEAD_DIM // 2,), F32)])
    reps = LANES // HEAD_DIM
    return jnp.tile(jnp.cos(ang), (1, reps)), jnp.tile(jnp.sin(ang) * sign, (1, reps))


def kernel(x, g_mix, w_in, w_branch_moba, w_branch_diff, w_out, diff_lambda_q1, diff_lambda_k1, diff_lambda_q2, diff_lambda_k2, diff_subln_g, g_ffn, w_group, b_group, w_router, b_router, w_expert_gate, w_expert_up, w_expert_down, g_final):
    B, S, D = x.shape
    assert w_in.shape[0] == 1, "one layer"
    T = B * S
    x2 = x.reshape(T, D)
    n_qkv = w_in.shape[2] - 2 * D
    w_qkv = w_in[0, :, :n_qkv].astype(BF16)
    w_g = w_in[0, :, n_qkv:].astype(BF16)
    cos, sin = _rope_tables(S)

    qa, ka, va, qd, kd, vd = _qkv_proj(x2, g_mix, w_qkv, cos, sin, S)
    W = qa.shape[1]
    o_a = _moba_attention(qa.reshape(B, S, W), ka.reshape(B, S, W), va.reshape(B, S, W)).reshape(T, W)
    o_d = _diff_attention(qd.reshape(B, S, W), kd.reshape(B, S, W), vd.reshape(B, S, W),
                          diff_lambda_q1, diff_lambda_k1, diff_lambda_q2, diff_lambda_k2, diff_subln_g).reshape(T, W)

    w_r = jnp.concatenate([w_group[0], w_router[0]], axis=1)
    w_r = jnp.pad(w_r, ((0, 0), (0, LANES - w_r.shape[1])))
    wr_hi = w_r.astype(BF16)
    w_r2 = jnp.concatenate([wr_hi, (w_r - wr_hi.astype(F32)).astype(BF16)], axis=1)
    b_r = jnp.pad(jnp.concatenate([b_group[0], b_router[0]])[None, :], ((0, 0), (0, LANES - N_GROUPS - N_EXPERTS)))

    x1, h2p, route, route_t, counts = _merge_route(
        x2, o_a, o_d, g_mix, w_g, w_branch_moba[0].astype(BF16), w_branch_diff[0].astype(BF16),
        w_out[0].astype(BF16), g_ffn, w_r2, b_r)

    out = _moe(h2p, route, route_t, counts, x1, g_final[None, :],
               w_expert_gate[0], w_expert_up[0], w_expert_down[0])
    return out.reshape(B, S, D)
```
